```python
import math
import jax
import jax.numpy as jnp
from jax import lax
import numpy as np

D_MODEL = 1024
BATCH = 4
SEQ = 4096
DEPTH = 2
DEC_BATCH = 32
DEC_SEQ = 4
PAST_LEN = 8192
PAGE_SIZE = 128

D_MIX = D_MODEL
D_A = D_MIX // 2
HEAD_A = 64
N_HEADS_A = D_A // HEAD_A
LORA_W = 64
LORA_A = 64
LORA_G = 128
D_A_PROJ = 3 * D_A + LORA_W + LORA_A + LORA_G
SPLITS_A = (D_A, 2 * D_A, 3 * D_A, 3 * D_A + LORA_W, 3 * D_A + LORA_W + LORA_A)
D_B = D_MIX - D_A
S5_GROUP = 16
N_GROUPS_B = D_B // S5_GROUP
S5_STATE = 64
D_IN_AB = D_A_PROJ + D_B
HEAD_C = 128
HALF_C = HEAD_C // 2
N_HEADS_C = D_MIX // HEAD_C
SCALE_C = HALF_C ** -0.5
Q_BLOCK = 128
D_FF = 4 * D_MODEL
RMS_EPS = 1e-6
GN_EPS = 64e-5
DECAY_SCALE = math.exp(-0.5)
LAMBDA_INIT = 0.8 - 0.6 * math.exp(-0.3 * 1)
NEG_INF = -1e30
F32 = jnp.float32

kernel_name = 'hybrid_rwkv7_s5_diffattn_decoder_step'


def rmsnorm(x, g):
    xf = x.astype(F32)
    y = xf * lax.rsqrt(jnp.mean(xf * xf, axis=-1, keepdims=True) + RMS_EPS)
    return (y * g.astype(F32)).astype(x.dtype)


def rwkv7_time_mix(zs, s0, W):
    B_, T, _ = zs.shape
    zf = zs.astype(F32)
    r, k, v, dw, da, dg = jnp.split(zf, SPLITS_A, axis=-1)
    decay = jnp.exp(-DECAY_SCALE * jax.nn.sigmoid(W['rwkv_w0'] + jnp.tanh(dw) @ W['rwkv_w_up']))
    a = jax.nn.sigmoid(W['rwkv_a0'] + da @ W['rwkv_a_up'])
    g = jax.nn.sigmoid(dg) @ W['rwkv_g_up']
    heads = lambda t: t.reshape(B_, T, N_HEADS_A, HEAD_A)
    kk = heads(k * W['rwkv_k_k'])
    kk = kk / jnp.maximum(jnp.sqrt(jnp.sum(kk * kk, axis=-1, keepdims=True)), 1e-12)
    k = k * (1.0 + (a - 1.0) * W['rwkv_k_a'])
    rh, wh, kh, vh, ah = heads(r), heads(decay), heads(k), heads(v), heads(a)

    def step(S, inp):
        r_t, w_t, k_t, v_t, kk_t, a_t = inp
        sa = jnp.einsum('bhvk,bhk->bhv', S, kk_t)
        S = (S * w_t[:, :, None, :]
             - sa[..., None] * (kk_t * a_t)[:, :, None, :]
             + v_t[..., None] * k_t[:, :, None, :])
        return S, jnp.einsum('bhvk,bhk->bhv', S, r_t)

    xs = tuple(jnp.moveaxis(t, 1, 0) for t in (rh, wh, kh, vh, kk, ah))
    s_last, o = lax.scan(step, s0.astype(F32), xs)
    o = jnp.moveaxis(o, 0, 1)
    mu = jnp.mean(o, axis=-1, keepdims=True)
    var = jnp.mean(jnp.square(o - mu), axis=-1, keepdims=True)
    on = ((o - mu) * lax.rsqrt(var + GN_EPS)).reshape(B_, T, D_A)
    on = on * W['rwkv_lnx_w'] + W['rwkv_lnx_b']
    bonus = jnp.sum(rh * kh * W['rwkv_r_k'], axis=-1, keepdims=True) * vh
    out = (on + bonus.reshape(B_, T, D_A)) * g
    return out.astype(zs.dtype), s_last


def _complex_affine_combine(e1, e2):
    a1r, a1i, b1r, b1i = e1
    a2r, a2i, b2r, b2i = e2
    return (a2r * a1r - a2i * a1i, a2r * a1i + a2i * a1r,
            a2r * b1r - a2i * b1i + b2r, a2r * b1i + a2i * b1r + b2i)


def s5_mix(u, h0_re, h0_im, W):
    B_, T, _ = u.shape
    uf = u.astype(F32).reshape(B_, T, N_GROUPS_B, S5_GROUP)
    dt = jnp.exp(W['s5_log_dt'].astype(F32))[:, None]
    lr, li = W['s5_lam_re'].astype(F32), W['s5_lam_im'].astype(F32)
    mag = jnp.exp(lr * dt)
    ar, ai = mag * jnp.cos(li * dt), mag * jnp.sin(li * dt)
    den = lr * lr + li * li
    fr = ((ar - 1.0) * lr + ai * li) / den
    fi = (ai * lr - (ar - 1.0) * li) / den
    b_re, b_im = W['s5_b_re'].astype(F32), W['s5_b_im'].astype(F32)
    bb_re = fr[..., None] * b_re - fi[..., None] * b_im
    bb_im = fr[..., None] * b_im + fi[..., None] * b_re
    bu_re = jnp.einsum('btgn,gpn->btgp', uf, bb_re)
    bu_im = jnp.einsum('btgn,gpn->btgp', uf, bb_im)
    a_re = jnp.broadcast_to(ar, (1, T) + ar.shape)
    a_im = jnp.broadcast_to(ai, (1, T) + ai.shape)
    A_re, A_im, H_re, H_im = lax.associative_scan(
        _complex_affine_combine, (a_re, a_im, bu_re, bu_im), axis=1)
    h0r, h0i = h0_re.astype(F32)[:, None], h0_im.astype(F32)[:, None]
    hr = H_re + A_re * h0r - A_im * h0i
    hi = H_im + A_re * h0i + A_im * h0r
    y = (jnp.einsum('btgp,gnp->btgn', hr, W['s5_c_re'].astype(F32))
         - jnp.einsum('btgp,gnp->btgn', hi, W['s5_c_im'].astype(F32))
         + W['s5_d'].astype(F32) * uf)
    y = jax.nn.gelu(y).reshape(B_, T, D_B)
    out = y * jax.nn.sigmoid(y @ W['s5_w_glu'].astype(F32) + W['s5_b_glu'].astype(F32))
    return out.astype(u.dtype), hr[:, -1], hi[:, -1]


def rwkv_s5_mixer(h, shift_prev, wkv_prev, sre_prev, sim_prev, W):
    z = h @ W['w_in_ab']
    za, u = z[..., :D_A_PROJ], z[..., D_A_PROJ:]
    za_prev = jnp.concatenate([shift_prev[:, None, :].astype(za.dtype), za[:, :-1]], axis=1)
    zs = za + W['rwkv_mu'] * (za_prev - za)
    y_a, wkv_new = rwkv7_time_mix(zs, wkv_prev, W)
    y_b, sre_new, sim_new = s5_mix(u, sre_prev, sim_prev, W)
    return jnp.concatenate([y_a, y_b], axis=-1), za[:, -1], wkv_new, sre_new, sim_new


def diff_weights(s, lam):
    p = jax.nn.softmax(s, axis=-1)
    return p[:, :, 0] - lam * p[:, :, 1]


def attend_prompt(q, k, v, lam):
    B_, T = q.shape[:2]
    nb = T // Q_BLOCK
    qb = jnp.moveaxis((q.astype(F32) * SCALE_C).reshape(B_, nb, Q_BLOCK, N_HEADS_C, 2, HALF_C), 1, 0)
    kf, vf = k.astype(F32), v.astype(F32)
    kpos = jnp.arange(T)

    def one_block(args):
        q_blk, i = args
        s = jnp.einsum('bqhcd,bkhcd->bhcqk', q_blk, kf)
        qpos = i * Q_BLOCK + jnp.arange(Q_BLOCK)
        s = jnp.where(kpos[None, :] <= qpos[:, None], s, NEG_INF)
        return jnp.einsum('bhqk,bkhd->bqhd', diff_weights(s, lam), vf)

    o = lax.map(one_block, (qb, jnp.arange(nb)))
    return jnp.moveaxis(o, 0, 1).reshape(B_, T, N_HEADS_C, HEAD_C)


def attend_sample(q, k, v, lam, k_past, v_past):
    B_, T = q.shape[:2]
    P = k_past.shape[1]
    qf = q.astype(F32) * SCALE_C
    kp = k_past.astype(F32).reshape(B_, P, N_HEADS_C, 2, HALF_C)
    s_past = jnp.einsum('bqhcd,bkhcd->bhcqk', qf, kp)
    s_new = jnp.einsum('bqhcd,bkhcd->bhcqk', qf, k.astype(F32))
    s_new = jnp.where(jnp.tril(jnp.ones((T, T), dtype=bool)), s_new, NEG_INF)
    a = diff_weights(jnp.concatenate([s_past, s_new], axis=-1), lam)
    return (jnp.einsum('bhqk,bkhd->bqhd', a[..., :P], v_past.astype(F32))
            + jnp.einsum('bhqk,bkhd->bqhd', a[..., P:], v.astype(F32)))


def diff_attn_mixer(h, lam, attend, W):
    B_, T, _ = h.shape
    q, k, v = jnp.split(h @ W['diff_w_qkv'], 3, axis=-1)
    q = q.reshape(B_, T, N_HEADS_C, 2, HALF_C)
    k = k.reshape(B_, T, N_HEADS_C, 2, HALF_C)
    v = v.reshape(B_, T, N_HEADS_C, HEAD_C)
    o = attend(q, k, v, lam)
    o = (o * lax.rsqrt(jnp.mean(o * o, axis=-1, keepdims=True) + RMS_EPS)
         * W['diff_subln'].astype(F32) * (1.0 - LAMBDA_INIT))
    return o.reshape(B_, T, D_MIX).astype(h.dtype), k.reshape(B_, T, N_HEADS_C, HEAD_C), v


def trunk(x, c, shift0, wkv0, sre0, sim0, attend, W):
    lam = (jnp.exp(jnp.sum(W['diff_lq1'].astype(F32) * W['diff_lk1'].astype(F32)))
           - jnp.exp(jnp.sum(W['diff_lq2'].astype(F32) * W['diff_lk2'].astype(F32))) + LAMBDA_INIT)
    cond = jax.nn.silu(c)
    for layer in range(DEPTH):
        mod = cond @ W['w_ada'][layer] + W['b_ada'][layer]
        sh_m, sc_m, g_m, sh_f, sc_f, g_f = jnp.split(mod[:, None, :], 6, axis=-1)
        h = rmsnorm(x, W['norm_mix'][layer]) * (1.0 + sc_m) + sh_m
        if layer % 2 == 0:
            y, shift1, wkv1, sre1, sim1 = rwkv_s5_mixer(h, shift0, wkv0, sre0, sim0, W)
        else:
            y, k_rows, v_rows = diff_attn_mixer(h, lam, attend, W)
        x = x + g_m * (y @ W['w_out'][layer])
        h = rmsnorm(x, W['norm_mlp'][layer]) * (1.0 + sc_f) + sh_f
        x = x + g_f * (jnp.square(jax.nn.relu(h @ W['w_up'][layer])) @ W['w_down'][layer])
    return rmsnorm(x, W['norm_f']), shift1, wkv1, sre1, sim1, k_rows, v_rows


def setup_inputs(seed: int = 0) -> dict:
    key = jax.random.key(seed)
    keys = iter(jax.random.split(key, 64))

    def nrm(shape, scale):
        return scale * jax.random.normal(next(keys), shape, jnp.float32)

    def gain(shape):
        return 1.0 + nrm(shape, 0.1)

    n_pages = PAST_LEN // PAGE_SIZE
    n_used = DEC_BATCH * n_pages
    n_pool = (5 * n_used + 3) // 4
    page_table = jax.random.permutation(next(keys), n_pool)[:n_used].reshape(DEC_BATCH, n_pages).astype(jnp.int32)
    lam_im = (jnp.pi * jnp.broadcast_to(jnp.arange(S5_STATE, dtype=jnp.float32), (N_GROUPS_B, S5_STATE))
              + nrm((N_GROUPS_B, S5_STATE), 0.01))
    log_dt = jax.random.uniform(next(keys), (N_GROUPS_B,), jnp.float32, math.log(1e-3), math.log(1e-1))
    return {
        'x_prompt': nrm((BATCH, SEQ, D_MODEL), 1.0),
        'x_sample': nrm((DEC_BATCH, DEC_SEQ, D_MODEL), 1.0),
        'state_shift': nrm((DEC_BATCH, D_A_PROJ), 1.0),
        'state_wkv': nrm((DEC_BATCH, N_HEADS_A, HEAD_A, HEAD_A), 0.1),
        'state_ssm_re': nrm((DEC_BATCH, N_GROUPS_B, S5_STATE), 0.1),
        'state_ssm_im': nrm((DEC_BATCH, N_GROUPS_B, S5_STATE), 0.1),
        'cache_k': nrm((n_pool, PAGE_SIZE, N_HEADS_C, HEAD_C), 1.0),
        'cache_v': nrm((n_pool, PAGE_SIZE, N_HEADS_C, HEAD_C), 1.0),
        'page_table': page_table,
        'c_prompt': nrm((BATCH, D_MODEL), 1.0),
        'c_sample': nrm((DEC_BATCH, D_MODEL), 1.0),
        'norm_mix': gain((DEPTH, D_MODEL)),
        'norm_mlp': gain((DEPTH, D_MODEL)),
        'norm_f': gain((D_MODEL,)),
        'w_ada': nrm((DEPTH, D_MODEL, 6 * D_MODEL), 0.5 * D_MODEL ** -0.5),
        'b_ada': nrm((DEPTH, 6 * D_MODEL), 0.02),
        'w_out': nrm((DEPTH, D_MIX, D_MODEL), D_MIX ** -0.5),
        'w_up': nrm((DEPTH, D_MODEL, D_FF), D_MODEL ** -0.5),
        'w_down': nrm((DEPTH, D_FF, D_MODEL), D_FF ** -0.5),
        'w_in_ab': nrm((D_MODEL, D_IN_AB), D_MODEL ** -0.5),
        'rwkv_mu': jax.random.uniform(next(keys), (D_A_PROJ,), jnp.float32),
        'rwkv_w0': nrm((D_A,), 0.5),
        'rwkv_w_up': nrm((LORA_W, D_A), LORA_W ** -0.5),
        'rwkv_a0': nrm((D_A,), 0.5),
        'rwkv_a_up': nrm((LORA_A, D_A), LORA_A ** -0.5),
        'rwkv_g_up': nrm((LORA_G, D_A), LORA_G ** -0.5),
        'rwkv_k_k': gain((D_A,)),
        'rwkv_k_a': gain((D_A,)),
        'rwkv_r_k': nrm((N_HEADS_A, HEAD_A), 0.1),
        'rwkv_lnx_w': gain((D_A,)),
        'rwkv_lnx_b': nrm((D_A,), 0.01),
        's5_lam_re': -0.5 + nrm((N_GROUPS_B, S5_STATE), 0.01),
        's5_lam_im': lam_im,
        's5_log_dt': log_dt,
        's5_b_re': nrm((N_GROUPS_B, S5_STATE, S5_GROUP), S5_GROUP ** -0.5),
        's5_b_im': nrm((N_GROUPS_B, S5_STATE, S5_GROUP), S5_GROUP ** -0.5),
        's5_c_re': nrm((N_GROUPS_B, S5_GROUP, S5_STATE), 0.5),
        's5_c_im': nrm((N_GROUPS_B, S5_GROUP, S5_STATE), 0.5),
        's5_d': nrm((N_GROUPS_B, S5_GROUP), 0.5),
        's5_w_glu': nrm((D_B, D_B), D_B ** -0.5),
        's5_b_glu': nrm((D_B,), 0.01),
        'diff_w_qkv': nrm((D_MODEL, 3 * D_MIX), D_MODEL ** -0.5),
        'diff_lq1': nrm((HALF_C,), 0.1),
        'diff_lk1': nrm((HALF_C,), 0.1),
        'diff_lq2': nrm((HALF_C,), 0.1),
        'diff_lk2': nrm((HALF_C,), 0.1),
        'diff_subln': gain((HEAD_C,)),
    }


def reference(x_prompt, x_sample, state_shift, state_wkv, state_ssm_re, state_ssm_im,
              cache_k, cache_v, page_table, c_prompt, c_sample,
              norm_mix, norm_mlp, norm_f, w_ada, b_ada, w_out, w_up, w_down,
              w_in_ab, rwkv_mu, rwkv_w0, rwkv_w_up, rwkv_a0, rwkv_a_up, rwkv_g_up,
              rwkv_k_k, rwkv_k_a, rwkv_r_k, rwkv_lnx_w, rwkv_lnx_b,
              s5_lam_re, s5_lam_im, s5_log_dt, s5_b_re, s5_b_im, s5_c_re, s5_c_im,
              s5_d, s5_w_glu, s5_b_glu,
              diff_w_qkv, diff_lq1, diff_lk1, diff_lq2, diff_lk2, diff_subln):
    W = dict(norm_mix=norm_mix, norm_mlp=norm_mlp, norm_f=norm_f, w_ada=w_ada, b_ada=b_ada,
             w_out=w_out, w_up=w_up, w_down=w_down, w_in_ab=w_in_ab,
             rwkv_mu=rwkv_mu, rwkv_w0=rwkv_w0, rwkv_w_up=rwkv_w_up, rwkv_a0=rwkv_a0,
             rwkv_a_up=rwkv_a_up, rwkv_g_up=rwkv_g_up, rwkv_k_k=rwkv_k_k, rwkv_k_a=rwkv_k_a,
             rwkv_r_k=rwkv_r_k, rwkv_lnx_w=rwkv_lnx_w, rwkv_lnx_b=rwkv_lnx_b,
             s5_lam_re=s5_lam_re, s5_lam_im=s5_lam_im, s5_log_dt=s5_log_dt,
             s5_b_re=s5_b_re, s5_b_im=s5_b_im, s5_c_re=s5_c_re, s5_c_im=s5_c_im,
             s5_d=s5_d, s5_w_glu=s5_w_glu, s5_b_glu=s5_b_glu,
             diff_w_qkv=diff_w_qkv, diff_lq1=diff_lq1, diff_lk1=diff_lk1,
             diff_lq2=diff_lq2, diff_lk2=diff_lk2, diff_subln=diff_subln)
    bp = x_prompt.shape[0]
    y_prompt, p_shift, p_wkv, p_ssm_re, p_ssm_im, p_k, p_v = trunk(
        x_prompt, c_prompt,
        jnp.zeros((bp, D_A_PROJ), x_prompt.dtype),
        jnp.zeros((bp, N_HEADS_A, HEAD_A, HEAD_A), F32),
        jnp.zeros((bp, N_GROUPS_B, S5_STATE), F32),
        jnp.zeros((bp, N_GROUPS_B, S5_STATE), F32),
        attend_prompt, W)
    db = page_table.shape[0]
    k_past = cache_k[page_table].reshape(db, -1, N_HEADS_C, HEAD_C)
    v_past = cache_v[page_table].reshape(db, -1, N_HEADS_C, HEAD_C)
    attend_s = lambda q, k, v, lam: attend_sample(q, k, v, lam, k_past, v_past)
    y_sample, s_shift, s_wkv, s_ssm_re, s_ssm_im, s_k, s_v = trunk(
        x_sample, c_sample, state_shift, state_wkv, state_ssm_re, state_ssm_im, attend_s, W)
    return (y_prompt, y_sample, p_shift, p_wkv, p_ssm_re, p_ssm_im, p_k, p_v,
            s_shift, s_wkv, s_ssm_re, s_ssm_im, s_k, s_v)
```

```python
import functools
import math

import jax
import jax.numpy as jnp
from jax import lax
from jax.experimental import pallas as pl
from jax.experimental.pallas import tpu as pltpu

F32 = jnp.float32
BF16 = jnp.bfloat16
HIGHEST = lax.Precision.HIGHEST

HEAD_A = 64
LORA_W, LORA_A, LORA_G = 64, 64, 128
S5_GROUP = 16
S5_STATE = 64
HEAD_C = 128
HALF_C = HEAD_C // 2
SCALE_C = HALF_C ** -0.5
RMS_EPS = 1e-6
GN_EPS = 64e-5
DECAY_SCALE = math.exp(-0.5)
LAMBDA_INIT = 0.8 - 0.6 * math.exp(-0.3 * 1)
NEG_INF = -1e30

SUBLANES = 8
LANES = 128
MXU_DIM = 256
VMEM_LIMIT = 56 * 1024 * 1024

RWKV_CHUNK = 64
NEW_PAD = 16


def _cparams(sem):
    return pltpu.CompilerParams(dimension_semantics=sem, vmem_limit_bytes=VMEM_LIMIT)


def _bdot(a, b):
    return jnp.dot(a.astype(BF16), b.astype(BF16), preferred_element_type=F32)


def _bdot_nt(a, b):
    return lax.dot_general(a.astype(BF16), b.astype(BF16), (((1,), (1,)), ((), ())),
                           preferred_element_type=F32)


def _bdot_tn(a, b):
    return lax.dot_general(a.astype(BF16), b.astype(BF16), (((0,), (0,)), ((), ())),
                           preferred_element_type=F32)


def _sigmoid(x):
    return 1.0 / (1.0 + jnp.exp(-x))


def _ada_body(c_ref, w_ref, b_ref, o_ref):
    c = c_ref[...]
    cond = c * _sigmoid(c)
    o_ref[...] = jnp.dot(cond, w_ref[...], precision=HIGHEST,
                         preferred_element_type=F32) + b_ref[...]


def _ada_mod(c, w_ada, b_ada):
    depth, d, n = w_ada.shape
    rows = c.shape[0]
    tn = 1536
    return pl.pallas_call(
        _ada_body,
        grid=(depth, n // tn),
        in_specs=[pl.BlockSpec((rows, d), lambda l, j: (0, 0)),
                  pl.BlockSpec((None, d, tn), lambda l, j: (l, 0, j)),
                  pl.BlockSpec((None, 1, tn), lambda l, j: (l, 0, j))],
        out_specs=pl.BlockSpec((None, rows, tn), lambda l, j: (l, 0, j)),
        out_shape=jax.ShapeDtypeStruct((depth, rows, n), F32),
        compiler_params=_cparams(("arbitrary", "arbitrary")),
        name="ada_mod",
    )(c, w_ada, b_ada.reshape(depth, 1, n))


def _mod_spec(mod, tm, rows_per_batch, d):
    if mod.ndim == 3:
        tiles = rows_per_batch // tm
        return pl.BlockSpec((None, 1, d), lambda i, *_: (i // tiles, 0, 0))
    return pl.BlockSpec((tm, d), lambda i, *_: (i, 0))


def _norm_mod(x, g, sc, sh):
    ms = jnp.mean(x * x, axis=-1, keepdims=True)
    h = (x * lax.rsqrt(ms + RMS_EPS)) * g
    return h * (1.0 + sc) + sh


def _normmod_mm_body(x_ref, g_ref, sc_ref, sh_ref, w_ref, *o_refs, splits, col_chunk):
    hb = _norm_mod(x_ref[...], g_ref[...], sc_ref[...], sh_ref[...]).astype(BF16)
    for o_ref, (c0, c1) in zip(o_refs, splits):
        for s in range(c0, c1, col_chunk):
            e = min(s + col_chunk, c1)
            o_ref[:, s - c0:e - c0] = jnp.dot(hb, w_ref[:, s:e], preferred_element_type=F32)


def _normmod_mm(x2, g, sc, sh, w_bf16, splits, tm, rows_per_batch):
    m, d = x2.shape
    n = w_bf16.shape[1]
    outs = tuple(jax.ShapeDtypeStruct((m, c1 - c0), F32) for c0, c1 in splits)
    return pl.pallas_call(
        functools.partial(_normmod_mm_body, splits=splits, col_chunk=512),
        grid=(m // tm,),
        in_specs=[pl.BlockSpec((tm, d), lambda i: (i, 0)),
                  pl.BlockSpec((1, d), lambda i: (0, 0)),
                  _mod_spec(sc, tm, rows_per_batch, d),
                  _mod_spec(sh, tm, rows_per_batch, d),
                  pl.BlockSpec((d, n), lambda i: (0, 0))],
        out_specs=tuple(pl.BlockSpec((tm, c1 - c0), lambda i: (i, 0)) for c0, c1 in splits),
        out_shape=outs,
        compiler_params=_cparams(("arbitrary",)),
        name="normmod_mm",
    )(x2, g.reshape(1, d), sc, sh, w_bf16)


def _proj_res_body(*refs, n_in):
    x_ref, gate_ref = refs[0], refs[1]
    y_refs = refs[2:2 + n_in]
    w_refs = refs[2 + n_in:2 + 2 * n_in]
    o_ref = refs[2 + 2 * n_in]
    acc = None
    for y_ref, w_ref in zip(y_refs, w_refs):
        p = jnp.dot(y_ref[...].astype(BF16), w_ref[...], preferred_element_type=F32)
        acc = p if acc is None else acc + p
    o_ref[...] = x_ref[...] + gate_ref[...] * acc


def _proj_res(x2, gate, ys, ws_bf16, tm, rows_per_batch):
    m, d = x2.shape
    n_in = len(ys)
    in_specs = [pl.BlockSpec((tm, d), lambda i: (i, 0)), _mod_spec(gate, tm, rows_per_batch, d)]
    in_specs += [pl.BlockSpec((tm, y.shape[1]), lambda i: (i, 0)) for y in ys]
    in_specs += [pl.BlockSpec(w.shape, lambda i: (0, 0)) for w in ws_bf16]
    return pl.pallas_call(
        functools.partial(_proj_res_body, n_in=n_in),
        grid=(m // tm,),
        in_specs=in_specs,
        out_specs=pl.BlockSpec((tm, d), lambda i: (i, 0)),
        out_shape=jax.ShapeDtypeStruct((m, d), F32),
        compiler_params=_cparams(("arbitrary",)),
        name="proj_res",
    )(x2, gate, *ys, *ws_bf16)


def _mlp_body(x_ref, g_ref, sc_ref, sh_ref, gate_ref, wu_ref, wd_ref, gf_ref, o_ref,
              hb_ref, acc_ref, *, final_norm):
    f = pl.program_id(1)

    @pl.when(f == 0)
    def _():
        hb_ref[...] = _norm_mod(x_ref[...], g_ref[...], sc_ref[...], sh_ref[...]).astype(BF16)
        acc_ref[...] = jnp.zeros_like(acc_ref)

    up = jnp.dot(hb_ref[...], wu_ref[...], preferred_element_type=F32)
    act = jnp.square(jnp.maximum(up, 0.0)).astype(BF16)
    acc_ref[...] += jnp.dot(act, wd_ref[...], preferred_element_type=F32)

    @pl.when(f == pl.num_programs(1) - 1)
    def _():
        xn = x_ref[...] + gate_ref[...] * acc_ref[...]
        if final_norm:
            ms = jnp.mean(xn * xn, axis=-1, keepdims=True)
            xn = (xn * lax.rsqrt(ms + RMS_EPS)) * gf_ref[...]
        o_ref[...] = xn


def _mlp(x2, g, sc, sh, gate, wu_bf16, wd_bf16, gfinal, tm, tf, rows_per_batch, final_norm):
    m, d = x2.shape
    ff = wu_bf16.shape[1]
    return pl.pallas_call(
        functools.partial(_mlp_body, final_norm=final_norm),
        grid=(m // tm, ff // tf),
        in_specs=[pl.BlockSpec((tm, d), lambda i, f: (i, 0)),
                  pl.BlockSpec((1, d), lambda i, f: (0, 0)),
                  _mod_spec(sc, tm, rows_per_batch, d),
                  _mod_spec(sh, tm, rows_per_batch, d),
                  _mod_spec(gate, tm, rows_per_batch, d),
                  pl.BlockSpec((d, tf), lambda i, f: (0, f)),
                  pl.BlockSpec((tf, d), lambda i, f: (f, 0)),
                  pl.BlockSpec((1, d), lambda i, f: (0, 0))],
        out_specs=pl.BlockSpec((tm, d), lambda i, f: (i, 0)),
        out_shape=jax.ShapeDtypeStruct((m, d), F32),
        scratch_shapes=[pltpu.VMEM((tm, d), BF16), pltpu.VMEM((tm, d), F32)],
        compiler_params=_cparams(("arbitrary", "arbitrary")),
        name="mlp",
    )(x2, g.reshape(1, d), sc, sh, gate, wu_bf16, wd_bf16, gfinal.reshape(1, d))


def _rwkv_prep_body(za_ref, shift_ref, mu_ref, w0_ref, wup_ref, a0_ref, aup_ref, gup_ref,
                    r_ref, k_ref, v_ref, lw_ref, a_ref, g_ref, carry_ref, *, d_a):
    @pl.when(pl.program_id(1) == 0)
    def _():
        carry_ref[...] = shift_ref[...]

    za = za_ref[...]
    tm = za.shape[0]
    row = lax.broadcasted_iota(jnp.int32, za.shape, 0)
    prev = jnp.where(row == 0, carry_ref[...], pltpu.roll(za, 1, axis=0))
    carry_ref[...] = za[tm - 1:tm, :]
    zs = za + mu_ref[...] * (prev - za)
    o_w = 3 * d_a
    o_a = o_w + LORA_W
    o_g = o_a + LORA_A
    r_ref[...] = zs[:, 0:d_a]
    k_ref[...] = zs[:, d_a:2 * d_a]
    v_ref[...] = zs[:, 2 * d_a:3 * d_a]
    dw = zs[:, o_w:o_a]
    da = zs[:, o_a:o_g]
    dg = zs[:, o_g:o_g + LORA_G]
    lw_ref[...] = -DECAY_SCALE * _sigmoid(w0_ref[...] + _bdot(jnp.tanh(dw), wup_ref[...]))
    a_ref[...] = _sigmoid(a0_ref[...] + _bdot(da, aup_ref[...]))
    g_ref[...] = _bdot(_sigmoid(dg), gup_ref[...])


def _rwkv_prep(za, shift_prev, mu, w0, w_up, a0, a_up, g_up, tm):
    b, t, dp = za.shape
    d_a = w0.shape[0]
    out = jax.ShapeDtypeStruct((b, t, d_a), F32)
    row = lambda n: pl.BlockSpec((1, n), lambda i, j: (0, 0))
    full = lambda w: pl.BlockSpec(w.shape, lambda i, j: (0, 0))
    tile = pl.BlockSpec((None, tm, d_a), lambda i, j: (i, j, 0))
    return pl.pallas_call(
        functools.partial(_rwkv_prep_body, d_a=d_a),
        grid=(b, t // tm),
        in_specs=[pl.BlockSpec((None, tm, dp), lambda i, j: (i, j, 0)),
                  pl.BlockSpec((None, 1, dp), lambda i, j: (i, 0, 0)),
                  row(dp), row(d_a), full(w_up), row(d_a), full(a_up), full(g_up)],
        out_specs=(tile,) * 6,
        out_shape=(out,) * 6,
        scratch_shapes=[pltpu.VMEM((1, dp), F32)],
        compiler_params=_cparams(("arbitrary", "arbitrary")),
        name="rwkv_prep",
    )(za, shift_prev.reshape(b, 1, dp), mu.reshape(1, dp), w0.reshape(1, d_a), w_up,
      a0.reshape(1, d_a), a_up, g_up)


def _rwkv_head(r, k, v, lw, a, g, kkw, kaw, rkw, lnw, lnb, s0, valid, tri_incl, tri_strict,
               eye, n_doubling):
    c = r.shape[0]
    kk = k * kkw
    nrm = jnp.sqrt(jnp.sum(kk * kk, axis=-1, keepdims=True))
    kk = kk / jnp.maximum(nrm, 1e-12)
    k2 = k * (1.0 + (a - 1.0) * kaw)
    b = kk * a
    if valid is not None:
        lw = jnp.where(valid, lw, 0.0)
        kk = jnp.where(valid, kk, 0.0)
        k2 = jnp.where(valid, k2, 0.0)
        b = jnp.where(valid, b, 0.0)
        v = jnp.where(valid, v, 0.0)
        r = jnp.where(valid, r, 0.0)
    cum = jnp.dot(tri_incl, lw, precision=HIGHEST, preferred_element_type=F32)
    tot = cum[c - 1:c, :]
    e_in = jnp.exp(cum)
    e_out = jnp.exp(-cum)
    kap_t = kk * jnp.exp(cum - lw)
    r_t = r * e_in
    b_t = b * e_out
    k_t = k2 * e_out
    e_end = jnp.exp(tot - cum)
    b_h = b * e_end
    k_h = k2 * e_end

    rhs = jnp.concatenate([b_t, k_t], axis=0)
    gk = _bdot_nt(kap_t, rhs)
    gr = _bdot_nt(r_t, rhs)
    n1 = jnp.where(tri_strict, gk[:, :c], 0.0)
    n2 = jnp.where(tri_strict, gk[:, c:], 0.0)
    m1 = jnp.where(tri_incl > 0, gr[:, :c], 0.0)
    m2 = jnp.where(tri_incl > 0, gr[:, c:], 0.0)

    x = -n1
    tinv = eye + x
    for _ in range(n_doubling):
        x = _bdot(x, x)
        tinv = tinv + _bdot(tinv, x)

    nv = _bdot(jnp.concatenate([n2, m2], axis=0), v)
    n2v, m2v = nv[:c], nv[c:]
    ty = _bdot(tinv, jnp.concatenate([kap_t, n2v], axis=1))
    m1ty = _bdot(m1, ty)
    kt, ul = ty[:, :HEAD_A], ty[:, HEAD_A:]
    rq = r_t - m1ty[:, :HEAD_A]
    ol = m2v - m1ty[:, HEAD_A:]
    tb = _bdot_tn(ty, b_h)
    gmat = tb[:HEAD_A]
    hmat = _bdot_tn(v, k_h) - tb[HEAD_A:]

    o = _bdot_nt(rq, s0) + ol
    s_new = s0 * jnp.exp(tot) - _bdot(s0, gmat) + hmat

    mu = jnp.mean(o, axis=-1, keepdims=True)
    var = jnp.mean(jnp.square(o - mu), axis=-1, keepdims=True)
    on = (o - mu) * lax.rsqrt(var + GN_EPS) * lnw + lnb
    bonus = jnp.sum(r * k2 * rkw, axis=-1, keepdims=True) * v
    return (on + bonus) * g, s_new


def _rwkv_chunk_body(r_ref, k_ref, v_ref, lw_ref, a_ref, g_ref, kkw_ref, kaw_ref, rkw_ref,
                     lnw_ref, lnb_ref, s0_ref, y_ref, sout_ref, s_ref, *, chunk, n_chunks,
                     n_heads, n_valid):
    @pl.when(pl.program_id(1) == 0)
    def _():
        s_ref[...] = s0_ref[...]

    c = chunk
    ri = lax.broadcasted_iota(jnp.int32, (c, c), 0)
    ci = lax.broadcasted_iota(jnp.int32, (c, c), 1)
    tri_incl = (ri >= ci).astype(F32)
    tri_strict = ri > ci
    eye = (ri == ci).astype(F32)
    valid = None
    if n_valid < c:
        valid = lax.broadcasted_iota(jnp.int32, (c, HEAD_A), 0) < n_valid
    n_doubling = max(int(math.log2(c)) - 1, 0)

    def one_chunk(ic, carry):
        t0 = pl.multiple_of(ic * c, c)
        rows = pl.ds(t0, c)
        for h in range(n_heads):
            sl = slice(h * HEAD_A, (h + 1) * HEAD_A)
            y, s_new = _rwkv_head(
                r_ref[rows, sl], k_ref[rows, sl], v_ref[rows, sl], lw_ref[rows, sl],
                a_ref[rows, sl], g_ref[rows, sl], kkw_ref[:, sl], kaw_ref[:, sl],
                rkw_ref[:, sl], lnw_ref[:, sl], lnb_ref[:, sl], s_ref[h], valid,
                tri_incl, tri_strict, eye, n_doubling)
            y_ref[rows, sl] = y
            s_ref[h] = s_new
        return carry

    lax.fori_loop(0, n_chunks, one_chunk, 0)

    @pl.when(pl.program_id(1) == pl.num_programs(1) - 1)
    def _():
        sout_ref[...] = s_ref[...]


def _rwkv_chunked(r, k, v, lw, a, g, k_k, k_a, r_k, lnx_w, lnx_b, s0, tb, chunk, n_valid):
    b, t, d_a = r.shape
    n_heads = d_a // HEAD_A
    tile = pl.BlockSpec((None, tb, d_a), lambda i, j: (i, j, 0))
    row = pl.BlockSpec((1, d_a), lambda i, j: (0, 0))
    st = pl.BlockSpec((None, n_heads, HEAD_A, HEAD_A), lambda i, j: (i, 0, 0, 0))
    return pl.pallas_call(
        functools.partial(_rwkv_chunk_body, chunk=chunk, n_chunks=tb // chunk,
                          n_heads=n_heads, n_valid=n_valid),
        grid=(b, t // tb),
        in_specs=[tile] * 6 + [row] * 5 + [st],
        out_specs=(tile, st),
        out_shape=(jax.ShapeDtypeStruct((b, t, d_a), F32),
                   jax.ShapeDtypeStruct((b, n_heads, HEAD_A, HEAD_A), F32)),
        scratch_shapes=[pltpu.VMEM((n_heads, HEAD_A, HEAD_A), F32)],
        compiler_params=_cparams(("arbitrary", "arbitrary")),
        name="rwkv_chunked",
    )(r, k, v, lw, a, g, k_k.reshape(1, d_a), k_a.reshape(1, d_a), r_k.reshape(1, d_a),
      lnx_w.reshape(1, d_a), lnx_b.reshape(1, d_a), s0)


def _s5_disc_body(lr_ref, li_ref, ldt_ref, bre_ref, bim_ref, ar_ref, ai_ref, bbre_ref, bbim_ref):
    lr, li = lr_ref[...], li_ref[...]
    dt = jnp.exp(ldt_ref[...])
    mag = jnp.exp(lr * dt)
    ar = mag * jnp.cos(li * dt)
    ai = mag * jnp.sin(li * dt)
    den = lr * lr + li * li
    fr = ((ar - 1.0) * lr + ai * li) / den
    fi = (ai * lr - (ar - 1.0) * li) / den
    ar_ref[...] = ar
    ai_ref[...] = ai
    b_re, b_im = bre_ref[...], bim_ref[...]
    bbre_ref[...] = fr[:, None, :] * b_re - fi[:, None, :] * b_im
    bbim_ref[...] = fr[:, None, :] * b_im + fi[:, None, :] * b_re


def _s5_discretise(lam_re, lam_im, log_dt, b_re, b_im):
    g, p = lam_re.shape
    n = b_re.shape[2]
    bt = lambda x: jnp.swapaxes(x, 1, 2)
    return pl.pallas_call(
        _s5_disc_body,
        out_shape=(jax.ShapeDtypeStruct((g, p), F32), jax.ShapeDtypeStruct((g, p), F32),
                   jax.ShapeDtypeStruct((g, n, p), F32), jax.ShapeDtypeStruct((g, n, p), F32)),
        name="s5_discretise",
    )(lam_re, lam_im, log_dt.reshape(g, 1), bt(b_re), bt(b_im))


def _gelu_tanh(x):
    c = math.sqrt(2.0 / math.pi)
    return 0.5 * x * (1.0 + jnp.tanh(c * (x + 0.044715 * (x * x * x))))


def _s5_body(u_ref, h0r_ref, h0i_ref, ar_ref, ai_ref, bdre_ref, bdim_ref, cdre_ref, cdim_ref,
             d_ref, wglu_ref, bglu_ref, y_ref, hre_ref, him_ref,
             xr_ref, xi_ref, apr_ref, api_ref, cr_ref, ci_ref, *, last_row):
    j = pl.program_id(1)
    rows, width = xr_ref.shape
    half_u = u_ref.shape[1] // 2
    half_x = width // 2

    ar, ai = ar_ref[...], ai_ref[...]
    a2r, a2i = ar * ar - ai * ai, 2.0 * ar * ai
    a4r, a4i = a2r * a2r - a2i * a2i, 2.0 * a2r * a2i

    @pl.when(j == 0)
    def _():
        cr_ref[...] = h0r_ref[...]
        ci_ref[...] = h0i_ref[...]
        pr, pi = ar, ai
        for s in range(SUBLANES):
            apr_ref[s:s + 1, :] = pr
            api_ref[s:s + 1, :] = pi
            pr, pi = pr * ar - pi * ai, pr * ai + pi * ar

    ub = u_ref[...].astype(BF16)
    for hf in range(2):
        us = ub[:, hf * half_u:(hf + 1) * half_u]
        cols = slice(hf * half_x, (hf + 1) * half_x)
        xr_ref[:, cols] = jnp.dot(us, bdre_ref[hf], preferred_element_type=F32)
        xi_ref[:, cols] = jnp.dot(us, bdim_ref[hf], preferred_element_type=F32)

    lane_chunk = 4 * LANES
    sub = lax.broadcasted_iota(jnp.int32, (rows, lane_chunk), 0) % SUBLANES
    for c0 in range(0, width, lane_chunk):
        cols = slice(c0, c0 + lane_chunk)
        xr, xi = xr_ref[:, cols], xi_ref[:, cols]
        for d, (pr, pi) in ((1, (ar, ai)), (2, (a2r, a2i)), (4, (a4r, a4i))):
            pr, pi = pr[:, cols], pi[:, cols]
            sr = pltpu.roll(xr, d, axis=0)
            si = pltpu.roll(xi, d, axis=0)
            keep = sub >= d
            xr, xi = (xr + jnp.where(keep, pr * sr - pi * si, 0.0),
                      xi + jnp.where(keep, pr * si + pi * sr, 0.0))
        xr_ref[:, cols] = xr
        xi_ref[:, cols] = xi

    def group(n, carry):
        cr, ci = carry
        r0 = pl.multiple_of(n * SUBLANES, SUBLANES)
        rs = pl.ds(r0, SUBLANES)
        apr, api = apr_ref[...], api_ref[...]
        hr = xr_ref[rs, :] + apr * cr - api * ci
        hi = xi_ref[rs, :] + apr * ci + api * cr
        xr_ref[rs, :] = hr
        xi_ref[rs, :] = hi
        return hr[SUBLANES - 1:SUBLANES, :], hi[SUBLANES - 1:SUBLANES, :]

    cr, ci = lax.fori_loop(0, rows // SUBLANES, group, (cr_ref[...], ci_ref[...]))
    cr_ref[...] = cr
    ci_ref[...] = ci

    @pl.when(j == pl.num_programs(1) - 1)
    def _():
        hre_ref[...] = xr_ref[last_row:last_row + 1, :]
        him_ref[...] = xi_ref[last_row:last_row + 1, :]

    u = u_ref[...]
    for hf in range(2):
        cols = slice(hf * half_x, (hf + 1) * half_x)
        ucols = slice(hf * half_u, (hf + 1) * half_u)
        y = (jnp.dot(xr_ref[:, cols].astype(BF16), cdre_ref[hf], preferred_element_type=F32)
             - jnp.dot(xi_ref[:, cols].astype(BF16), cdim_ref[hf], preferred_element_type=F32)
             + d_ref[:, ucols] * u[:, ucols])
        y_ref[:, ucols] = _gelu_tanh(y)
    y = y_ref[...]
    z = jnp.dot(y.astype(BF16), wglu_ref[...], preferred_element_type=F32) + bglu_ref[...]
    y_ref[...] = y * _sigmoid(z)


def _block_diag_halves(w_gab, transpose):
    g = w_gab.shape[0]
    hg = g // 2
    eye = jnp.eye(hg, dtype=w_gab.dtype)
    halves = []
    for hf in range(2):
        w = w_gab[hf * hg:(hf + 1) * hg]
        if transpose:
            w = jnp.swapaxes(w, 1, 2)
        a, b = w.shape[1], w.shape[2]
        halves.append(jnp.einsum('gab,gh->gahb', w, eye).reshape(hg * a, hg * b))
    return jnp.stack(halves)


def _s5_mix(u, h0_re, h0_im, ar, ai, bbt_re, bbt_im, c_re, c_im, d, w_glu, b_glu, rows, n_valid):
    b, t, d_b = u.shape
    g, p = ar.shape
    width = g * p
    bd_re = _block_diag_halves(bbt_re, False).astype(BF16)
    bd_im = _block_diag_halves(bbt_im, False).astype(BF16)
    cd_re = _block_diag_halves(c_re, True).astype(BF16)
    cd_im = _block_diag_halves(c_im, True).astype(BF16)
    n_blocks = t // rows
    last_row = (n_valid - 1) - (n_blocks - 1) * rows
    full = lambda x: pl.BlockSpec(x.shape, lambda i, j: (0,) * x.ndim)
    st = pl.BlockSpec((None, 1, width), lambda i, j: (i, 0, 0))
    args = (u, h0_re.reshape(b, 1, width), h0_im.reshape(b, 1, width),
            ar.reshape(1, width), ai.reshape(1, width), bd_re, bd_im, cd_re, cd_im,
            d.reshape(1, d_b), w_glu.astype(BF16), b_glu.reshape(1, d_b))
    return pl.pallas_call(
        functools.partial(_s5_body, last_row=last_row),
        grid=(b, n_blocks),
        in_specs=[pl.BlockSpec((None, rows, d_b), lambda i, j: (i, j, 0)), st, st]
                 + [full(x) for x in args[3:]],
        out_specs=(pl.BlockSpec((None, rows, d_b), lambda i, j: (i, j, 0)), st, st),
        out_shape=(jax.ShapeDtypeStruct((b, t, d_b), F32),
                   jax.ShapeDtypeStruct((b, 1, width), F32),
                   jax.ShapeDtypeStruct((b, 1, width), F32)),
        scratch_shapes=[pltpu.VMEM((rows, width), F32), pltpu.VMEM((rows, width), F32),
                        pltpu.VMEM((SUBLANES, width), F32), pltpu.VMEM((SUBLANES, width), F32),
                        pltpu.VMEM((1, width), F32), pltpu.VMEM((1, width), F32)],
        compiler_params=_cparams(("arbitrary", "arbitrary")),
        name="s5_mix",
    )(*args)


def _lambda_full(lq1, lk1, lq2, lk2):
    s1 = jnp.sum(lq1 * lk1, axis=-1, keepdims=True)
    s2 = jnp.sum(lq2 * lk2, axis=-1, keepdims=True)
    return jnp.exp(s1) - jnp.exp(s2) + LAMBDA_INIT


def _sub_ln(o, subln):
    ms = jnp.mean(o * o, axis=-1, keepdims=True)
    return o * lax.rsqrt(ms + RMS_EPS) * subln * (1.0 - LAMBDA_INIT)


def _attn_prompt_body(q_ref, k_ref, v_ref, lq1_ref, lk1_ref, lq2_ref, lk2_ref, subln_ref,
                      o_ref, m_ref, l_ref, acc_ref, *, tq, tk):
    qi = pl.program_id(2)
    q = q_ref[...] * SCALE_C
    lane = lax.broadcasted_iota(jnp.int32, q.shape, 1)
    qs = jnp.concatenate([jnp.where(lane < HALF_C, q, 0.0),
                          jnp.where(lane >= HALF_C, q, 0.0)], axis=0).astype(BF16)
    m_ref[...] = jnp.full_like(m_ref, NEG_INF)
    l_ref[...] = jnp.zeros_like(l_ref)
    acc_ref[...] = jnp.zeros_like(acc_ref)
    q0 = qi * tq
    qpos = q0 + lax.broadcasted_iota(jnp.int32, (2 * tq, tk), 0) % tq
    kcol = lax.broadcasted_iota(jnp.int32, (2 * tq, tk), 1)

    def kv_step(kj, carry):
        k0 = pl.multiple_of(kj * tk, tk)
        kb = k_ref[pl.ds(k0, tk), :].astype(BF16)
        vb = v_ref[pl.ds(k0, tk), :].astype(BF16)
        s = lax.dot_general(qs, kb, (((1,), (1,)), ((), ())), preferred_element_type=F32)
        s = jnp.where(kcol + k0 <= qpos, s, NEG_INF)
        m_prev = m_ref[...]
        m_new = jnp.maximum(m_prev, jnp.max(s, axis=-1, keepdims=True))
        alpha = jnp.exp(m_prev - m_new)
        p = jnp.exp(s - m_new)
        l_ref[...] = alpha * l_ref[...] + jnp.sum(p, axis=-1, keepdims=True)
        acc_ref[...] = alpha * acc_ref[...] + jnp.dot(p.astype(BF16), vb,
                                                      preferred_element_type=F32)
        m_ref[...] = m_new
        return carry

    n_kv = (q0 + tq + tk - 1) // tk
    lax.fori_loop(0, n_kv, kv_step, 0)

    lam = _lambda_full(lq1_ref[...], lk1_ref[...], lq2_ref[...], lk2_ref[...])
    on = acc_ref[...] / l_ref[...]
    o = on[:tq] - lam * on[tq:]
    o_ref[...] = _sub_ln(o, subln_ref[...])


def _attn_prompt(q, k, v, lq1, lk1, lq2, lk2, subln, tq, tk):
    b, t, d = q.shape
    n_heads = d // HEAD_C
    small = lambda n: pl.BlockSpec((1, n), lambda i, h, j: (0, 0))
    return pl.pallas_call(
        functools.partial(_attn_prompt_body, tq=tq, tk=tk),
        grid=(b, n_heads, t // tq),
        in_specs=[pl.BlockSpec((None, tq, HEAD_C), lambda i, h, j: (i, j, h)),
                  pl.BlockSpec((None, t, HEAD_C), lambda i, h, j: (i, 0, h)),
                  pl.BlockSpec((None, t, HEAD_C), lambda i, h, j: (i, 0, h)),
                  small(HALF_C), small(HALF_C), small(HALF_C), small(HALF_C), small(HEAD_C)],
        out_specs=pl.BlockSpec((None, tq, HEAD_C), lambda i, h, j: (i, j, h)),
        out_shape=jax.ShapeDtypeStruct((b, t, d), F32),
        scratch_shapes=[pltpu.VMEM((2 * tq, 1), F32), pltpu.VMEM((2 * tq, 1), F32),
                        pltpu.VMEM((2 * tq, HEAD_C), F32)],
        compiler_params=_cparams(("arbitrary", "arbitrary", "arbitrary")),
        name="attn_prompt",
    )(q, k, v, lq1.reshape(1, -1), lk1.reshape(1, -1), lq2.reshape(1, -1), lk2.reshape(1, -1),
      subln.reshape(1, -1))


def _attn_sample_body(pt_ref, qrep_ref, knew_ref, vnew_ref, *rest, pages_per_step, n_heads,
                      t_new, page):
    kp_refs = rest[:pages_per_step]
    vp_refs = rest[pages_per_step:2 * pages_per_step]
    lq1_ref, lk1_ref, lq2_ref, lk2_ref, subln_ref = rest[2 * pages_per_step:2 * pages_per_step + 5]
    o_ref = rest[2 * pages_per_step + 5]
    qbd_ref, m_ref, l_ref, acc_ref = rest[2 * pages_per_step + 6:]
    j = pl.program_id(1)
    n_rows = qrep_ref.shape[0]

    @pl.when(j == 0)
    def _():
        qr = qrep_ref[...]
        row = lax.broadcasted_iota(jnp.int32, qr.shape, 0)
        lane = lax.broadcasted_iota(jnp.int32, qr.shape, 1)
        qbd_ref[...] = jnp.where(lane // HALF_C == row // t_new, qr * SCALE_C, 0.0).astype(BF16)
        m_ref[...] = jnp.full_like(m_ref, NEG_INF)
        l_ref[...] = jnp.zeros_like(l_ref)
        acc_ref[...] = jnp.zeros_like(acc_ref)

    qbd = qbd_ref[...]

    def update(s, vbs):
        m_prev = m_ref[...]
        m_new = jnp.maximum(m_prev, jnp.max(s, axis=-1, keepdims=True))
        alpha = jnp.exp(m_prev - m_new)
        p = jnp.exp(s - m_new)
        l_ref[...] = alpha * l_ref[...] + jnp.sum(p, axis=-1, keepdims=True)
        pv = None
        w = s.shape[1] // len(vbs)
        for i, vb in enumerate(vbs):
            d = jnp.dot(p[:, i * w:(i + 1) * w].astype(BF16), vb, preferred_element_type=F32)
            pv = d if pv is None else pv + d
        acc_ref[...] = alpha * acc_ref[...] + pv
        m_ref[...] = m_new

    s = jnp.concatenate(
        [lax.dot_general(qbd, kp[...].astype(BF16), (((1,), (1,)), ((), ())),
                         preferred_element_type=F32) for kp in kp_refs], axis=1)
    update(s, [vp[...].astype(BF16) for vp in vp_refs])

    @pl.when(j == pl.num_programs(1) - 1)
    def _():
        kn = knew_ref[...].astype(BF16)
        sn = lax.dot_general(qbd, kn, (((1,), (1,)), ((), ())), preferred_element_type=F32)
        trow = lax.broadcasted_iota(jnp.int32, sn.shape, 0) % t_new
        tcol = lax.broadcasted_iota(jnp.int32, sn.shape, 1)
        sn = jnp.where(tcol <= trow, sn, NEG_INF)
        update(sn, [vnew_ref[...].astype(BF16)])
        lam = _lambda_full(lq1_ref[...], lk1_ref[...], lq2_ref[...], lk2_ref[...])
        on = acc_ref[...] / l_ref[...]
        for h in range(n_heads):
            r0 = h * 2 * t_new
            cols = slice(h * HEAD_C, (h + 1) * HEAD_C)
            o = on[r0:r0 + t_new, cols] - lam * on[r0 + t_new:r0 + 2 * t_new, cols]
            o_ref[:, cols] = _sub_ln(o, subln_ref[...])


def _attn_sample(q, k_new, v_new, cache_k, cache_v, page_table, lq1, lk1, lq2, lk2, subln,
                 pages_per_step):
    db, t_new, d = q.shape
    n_heads = d // HEAD_C
    n_pool, page = cache_k.shape[0], cache_k.shape[1]
    n_pages = page_table.shape[1]
    ck = cache_k.reshape(n_pool, page, d)
    cv = cache_v.reshape(n_pool, page, d)
    n_rows = n_heads * 2 * t_new
    qrep = jnp.tile(q, (1, n_heads * 2, 1))
    pad = ((0, 0), (0, NEW_PAD - t_new), (0, 0))
    knew = jnp.pad(k_new, pad)
    vnew = jnp.pad(v_new, pad)
    pt = page_table.reshape(-1)

    def page_spec(i):
        return pl.BlockSpec((None, page, d),
                            lambda b, j, pt_ref: (pt_ref[b * n_pages + j * pages_per_step + i], 0, 0))

    small = lambda n: pl.BlockSpec((1, n), lambda b, j, pt_ref: (0, 0))
    per_b = lambda r: pl.BlockSpec((None, r, d), lambda b, j, pt_ref: (b, 0, 0))
    grid_spec = pltpu.PrefetchScalarGridSpec(
        num_scalar_prefetch=1,
        grid=(db, n_pages // pages_per_step),
        in_specs=[per_b(n_rows), per_b(NEW_PAD), per_b(NEW_PAD)]
                 + [page_spec(i) for i in range(pages_per_step)] * 2
                 + [small(HALF_C)] * 4 + [small(HEAD_C)],
        out_specs=per_b(t_new),
        scratch_shapes=[pltpu.VMEM((n_rows, d), BF16), pltpu.VMEM((n_rows, 1), F32),
                        pltpu.VMEM((n_rows, 1), F32), pltpu.VMEM((n_rows, d), F32)])
    return pl.pallas_call(
        functools.partial(_attn_sample_body, pages_per_step=pages_per_step, n_heads=n_heads,
                          t_new=t_new, page=page),
        grid_spec=grid_spec,
        out_shape=jax.ShapeDtypeStruct((db, t_new, d), F32),
        compiler_params=_cparams(("arbitrary", "arbitrary")),
        name="attn_sample",
    )(pt, qrep, knew, vnew, *([ck] * pages_per_step), *([cv] * pages_per_step),
      lq1.reshape(1, -1), lk1.reshape(1, -1), lq2.reshape(1, -1), lk2.reshape(1, -1),
      subln.reshape(1, -1))


def _pad_time(x, t_pad):
    return jnp.pad(x, ((0, 0), (0, t_pad - x.shape[1]), (0, 0)))


def _trunk(x, mods, shift0, wkv0, sre0, sim0, attend, W, per_row_mod):
    b, t, d = x.shape
    m = b * t
    tm = min(512, m)
    d_a = W['rwkv_w0'].shape[0]
    d_ap = W['rwkv_mu'].shape[0]

    def mod_arg(v):
        if per_row_mod:
            return jnp.repeat(v, t, axis=0)
        return v.reshape(b, 1, d)

    x2 = x.reshape(m, d)

    sh_m, sc_m, g_m, sh_f, sc_f, g_f = (mod_arg(v) for v in mods[0])
    za2, u2 = _normmod_mm(x2, W['norm_mix'][0], sc_m, sh_m, W['w_in_ab_bf16'],
                          ((0, d_ap), (d_ap, W['w_in_ab_bf16'].shape[1])), tm, t)
    za = za2.reshape(b, t, d_ap)
    u = u2.reshape(b, t, -1)
    shift1 = za[:, -1]

    chunk = RWKV_CHUNK
    t_pad = -(-t // chunk) * chunk
    if t_pad != t:
        za_p, u_p = _pad_time(za, t_pad), _pad_time(u, t_pad)
    else:
        za_p, u_p = za, u
    tb = min(512, t_pad)
    r, k, v, lw, a, g = _rwkv_prep(za_p, shift0, W['rwkv_mu'], W['rwkv_w0'], W['rwkv_w_up'],
                                   W['rwkv_a0'], W['rwkv_a_up'], W['rwkv_g_up'], tb)
    y_a, wkv1 = _rwkv_chunked(r, k, v, lw, a, g, W['rwkv_k_k'], W['rwkv_k_a'], W['rwkv_r_k'],
                              W['rwkv_lnx_w'], W['rwkv_lnx_b'], wkv0, tb, chunk,
                              chunk if t_pad == t else t)
    s5_rows = min(256, t_pad)
    y_b, sre1, sim1 = _s5_mix(u_p, sre0, sim0, W['s5_ar'], W['s5_ai'], W['s5_bbt_re'],
                              W['s5_bbt_im'], W['s5_c_re'], W['s5_c_im'], W['s5_d'],
                              W['s5_w_glu'], W['s5_b_glu'], s5_rows, t)
    y_a2 = y_a[:, :t].reshape(m, d_a)
    y_b2 = y_b[:, :t].reshape(m, -1)
    wo = W['w_out_bf16'][0]
    x2 = _proj_res(x2, g_m, (y_a2, y_b2), (wo[:d_a], wo[d_a:]), tm, t)
    x2 = _mlp(x2, W['norm_mlp'][0], sc_f, sh_f, g_f, W['w_up_bf16'][0], W['w_down_bf16'][0],
              W['norm_f'], tm, 1024, t, False)

    sh_m, sc_m, g_m, sh_f, sc_f, g_f = (mod_arg(v) for v in mods[1])
    q2, k2, v2 = _normmod_mm(x2, W['norm_mix'][1], sc_m, sh_m, W['diff_w_qkv_bf16'],
                             ((0, d), (d, 2 * d), (2 * d, 3 * d)), tm, t)
    q3, k3, v3 = (z.reshape(b, t, d) for z in (q2, k2, v2))
    o = attend(q3, k3, v3)
    x2 = _proj_res(x2, g_m, (o.reshape(m, d),), (W['w_out_bf16'][1],), tm, t)
    y2 = _mlp(x2, W['norm_mlp'][1], sc_f, sh_f, g_f, W['w_up_bf16'][1], W['w_down_bf16'][1],
              W['norm_f'], tm, 1024, t, True)

    n_heads_c = d // HEAD_C
    g_b, p_b = W['s5_ar'].shape
    return (y2.reshape(b, t, d), shift1, wkv1, sre1.reshape(b, g_b, p_b), sim1.reshape(b, g_b, p_b),
            k3.reshape(b, t, n_heads_c, HEAD_C), v3.reshape(b, t, n_heads_c, HEAD_C))


def kernel(x_prompt, x_sample, state_shift, state_wkv, state_ssm_re, state_ssm_im, cache_k, cache_v, page_table, c_prompt, c_sample, norm_mix, norm_mlp, norm_f, w_ada, b_ada, w_out, w_up, w_down, w_in_ab, rwkv_mu, rwkv_w0, rwkv_w_up, rwkv_a0, rwkv_a_up, rwkv_g_up, rwkv_k_k, rwkv_k_a, rwkv_r_k, rwkv_lnx_w, rwkv_lnx_b, s5_lam_re, s5_lam_im, s5_log_dt, s5_b_re, s5_b_im, s5_c_re, s5_c_im, s5_d, s5_w_glu, s5_b_glu, diff_w_qkv, diff_lq1, diff_lk1, diff_lq2, diff_lk2, diff_subln):
    bp, tp, d = x_prompt.shape
    db, ts, _ = x_sample.shape
    depth = w_ada.shape[0]
    n_heads_a = rwkv_r_k.shape[0]
    g_b, p_b = s5_lam_re.shape

    ar, ai, bbt_re, bbt_im = _s5_discretise(s5_lam_re, s5_lam_im, s5_log_dt, s5_b_re, s5_b_im)
    W = dict(norm_mix=norm_mix, norm_mlp=norm_mlp, norm_f=norm_f,
             w_out_bf16=w_out.astype(BF16), w_up_bf16=w_up.astype(BF16),
             w_down_bf16=w_down.astype(BF16), w_in_ab_bf16=w_in_ab.astype(BF16),
             diff_w_qkv_bf16=diff_w_qkv.astype(BF16),
             rwkv_mu=rwkv_mu, rwkv_w0=rwkv_w0, rwkv_w_up=rwkv_w_up, rwkv_a0=rwkv_a0,
             rwkv_a_up=rwkv_a_up, rwkv_g_up=rwkv_g_up, rwkv_k_k=rwkv_k_k, rwkv_k_a=rwkv_k_a,
             rwkv_r_k=rwkv_r_k.reshape(-1), rwkv_lnx_w=rwkv_lnx_w, rwkv_lnx_b=rwkv_lnx_b,
             s5_ar=ar, s5_ai=ai, s5_bbt_re=bbt_re, s5_bbt_im=bbt_im,
             s5_c_re=s5_c_re, s5_c_im=s5_c_im, s5_d=s5_d.reshape(-1), s5_w_glu=s5_w_glu,
             s5_b_glu=s5_b_glu)

    n_c = bp + db
    rows_c = -(-n_c // SUBLANES) * SUBLANES
    c_all = jnp.pad(jnp.concatenate([c_prompt, c_sample], axis=0), ((0, rows_c - n_c), (0, 0)))
    mod = _ada_mod(c_all, w_ada, b_ada)
    mods_p = [tuple(mod[l, :bp, i * d:(i + 1) * d] for i in range(6)) for l in range(depth)]
    mods_s = [tuple(mod[l, bp:n_c, i * d:(i + 1) * d] for i in range(6)) for l in range(depth)]

    lam_args = (diff_lq1, diff_lk1, diff_lq2, diff_lk2, diff_subln)
    attend_p = lambda q, k, v: _attn_prompt(q, k, v, *lam_args, tq=min(256, tp), tk=min(256, tp))
    y_prompt, p_shift, p_wkv, p_re, p_im, p_k, p_v = _trunk(
        x_prompt, mods_p,
        jnp.zeros((bp, rwkv_mu.shape[0]), F32),
        jnp.zeros((bp, n_heads_a, HEAD_A, HEAD_A), F32),
        jnp.zeros((bp, g_b, p_b), F32), jnp.zeros((bp, g_b, p_b), F32),
        attend_p, W, per_row_mod=False)

    attend_s = lambda q, k, v: _attn_sample(q, k, v, cache_k, cache_v, page_table, *lam_args,
                                            pages_per_step=4)
    y_sample, s_shift, s_wkv, s_re, s_im, s_k, s_v = _trunk(
        x_sample, mods_s, state_shift, state_wkv, state_ssm_re, state_ssm_im,
        attend_s, W, per_row_mod=True)

    return (y_prompt, y_sample, p_shift, p_wkv, p_re, p_im, p_k, p_v,
            s_shift, s_wkv, s_re, s_im, s_k, s_v)
```

```python
import functools
import math

import jax
import jax.numpy as jnp
from jax import lax
from jax.experimental import pallas as pl
from jax.experimental.pallas import tpu as pltpu

F32 = jnp.float32
BF16 = jnp.bfloat16
HIGHEST = lax.Precision.HIGHEST

HEAD_A = 64
LORA_W, LORA_A, LORA_G = 64, 64, 128
S5_GROUP = 16
S5_STATE = 64
HEAD_C = 128
HALF_C = HEAD_C // 2
SCALE_C = HALF_C ** -0.5
RMS_EPS = 1e-6
GN_EPS = 64e-5
DECAY_SCALE = math.exp(-0.5)
LAMBDA_INIT = 0.8 - 0.6 * math.exp(-0.3 * 1)
NEG_INF = -1e30

SUBLANES = 8
LANES = 128
MXU_DIM = 256
VMEM_LIMIT = 56 * 1024 * 1024

RWKV_CHUNK = 64
NEW_PAD = 16


def _cparams(sem):
    return pltpu.CompilerParams(dimension_semantics=sem, vmem_limit_bytes=VMEM_LIMIT)


def _bdot(a, b):
    return jnp.dot(a.astype(BF16), b.astype(BF16), preferred_element_type=F32)


def _bdot_nt(a, b):
    return lax.dot_general(a.astype(BF16), b.astype(BF16), (((1,), (1,)), ((), ())),
                           preferred_element_type=F32)


def _bdot_tn(a, b):
    return lax.dot_general(a.astype(BF16), b.astype(BF16), (((0,), (0,)), ((), ())),
                           preferred_element_type=F32)


def _sigmoid(x):
    return 1.0 / (1.0 + jnp.exp(-x))


def _ada_body(c_ref, w_ref, b_ref, o_ref):
    c = c_ref[...]
    cond = c * _sigmoid(c)
    o_ref[...] = jnp.dot(cond, w_ref[...], precision=HIGHEST,
                         preferred_element_type=F32) + b_ref[...]


def _ada_mod(c, w_ada, b_ada):
    depth, d, n = w_ada.shape
    rows = c.shape[0]
    tn = 1536
    return pl.pallas_call(
        _ada_body,
        grid=(depth, n // tn),
        in_specs=[pl.BlockSpec((rows, d), lambda l, j: (0, 0)),
                  pl.BlockSpec((None, d, tn), lambda l, j: (l, 0, j)),
                  pl.BlockSpec((None, 1, tn), lambda l, j: (l, 0, j))],
        out_specs=pl.BlockSpec((None, rows, tn), lambda l, j: (l, 0, j)),
        out_shape=jax.ShapeDtypeStruct((depth, rows, n), F32),
        compiler_params=_cparams(("arbitrary", "arbitrary")),
        name="ada_mod",
    )(c, w_ada, b_ada.reshape(depth, 1, n))


def _mod_spec(mod, tm, rows_per_batch, d):
    if mod.ndim == 3:
        tiles = rows_per_batch // tm
        return pl.BlockSpec((None, 1, d), lambda i, *_: (i // tiles, 0, 0))
    return pl.BlockSpec((tm, d), lambda i, *_: (i, 0))


def _norm_mod(x, g, sc, sh):
    ms = jnp.mean(x * x, axis=-1, keepdims=True)
    h = (x * lax.rsqrt(ms + RMS_EPS)) * g
    return h * (1.0 + sc) + sh


def _normmod_mm_body(x_ref, g_ref, sc_ref, sh_ref, w_ref, *o_refs, splits, col_chunk):
    hb = _norm_mod(x_ref[...], g_ref[...], sc_ref[...], sh_ref[...]).astype(BF16)
    for o_ref, (c0, c1) in zip(o_refs, splits):
        for s in range(c0, c1, col_chunk):
            e = min(s + col_chunk, c1)
            o_ref[:, s - c0:e - c0] = jnp.dot(hb, w_ref[:, s:e], preferred_element_type=F32)


def _normmod_mm(x2, g, sc, sh, w_bf16, splits, tm, rows_per_batch):
    m, d = x2.shape
    n = w_bf16.shape[1]
    outs = tuple(jax.ShapeDtypeStruct((m, c1 - c0), F32) for c0, c1 in splits)
    return pl.pallas_call(
        functools.partial(_normmod_mm_body, splits=splits, col_chunk=512),
        grid=(m // tm,),
        in_specs=[pl.BlockSpec((tm, d), lambda i: (i, 0)),
                  pl.BlockSpec((1, d), lambda i: (0, 0)),
                  _mod_spec(sc, tm, rows_per_batch, d),
                  _mod_spec(sh, tm, rows_per_batch, d),
                  pl.BlockSpec((d, n), lambda i: (0, 0))],
        out_specs=tuple(pl.BlockSpec((tm, c1 - c0), lambda i: (i, 0)) for c0, c1 in splits),
        out_shape=outs,
        compiler_params=_cparams(("arbitrary",)),
        name="normmod_mm",
    )(x2, g.reshape(1, d), sc, sh, w_bf16)


def _proj_res_body(*refs, n_in):
    x_ref, gate_ref = refs[0], refs[1]
    y_refs = refs[2:2 + n_in]
    w_refs = refs[2 + n_in:2 + 2 * n_in]
    o_ref = refs[2 + 2 * n_in]
    acc = None
    for y_ref, w_ref in zip(y_refs, w_refs):
        p = jnp.dot(y_ref[...].astype(BF16), w_ref[...], preferred_element_type=F32)
        acc = p if acc is None else acc + p
    o_ref[...] = x_ref[...] + gate_ref[...] * acc


def _proj_res(x2, gate, ys, ws_bf16, tm, rows_per_batch):
    m, d = x2.shape
    n_in = len(ys)
    in_specs = [pl.BlockSpec((tm, d), lambda i: (i, 0)), _mod_spec(gate, tm, rows_per_batch, d)]
    in_specs += [pl.BlockSpec((tm, y.shape[1]), lambda i: (i, 0)) for y in ys]
    in_specs += [pl.BlockSpec(w.shape, lambda i: (0, 0)) for w in ws_bf16]
    return pl.pallas_call(
        functools.partial(_proj_res_body, n_in=n_in),
        grid=(m // tm,),
        in_specs=in_specs,
        out_specs=pl.BlockSpec((tm, d), lambda i: (i, 0)),
        out_shape=jax.ShapeDtypeStruct((m, d), F32),
        compiler_params=_cparams(("arbitrary",)),
        name="proj_res",
    )(x2, gate, *ys, *ws_bf16)


def _mlp_body(x_ref, g_ref, sc_ref, sh_ref, gate_ref, wu_ref, wd_ref, gf_ref, o_ref,
              hb_ref, acc_ref, *, final_norm):
    f = pl.program_id(1)

    @pl.when(f == 0)
    def _():
        hb_ref[...] = _norm_mod(x_ref[...], g_ref[...], sc_ref[...], sh_ref[...]).astype(BF16)
        acc_ref[...] = jnp.zeros_like(acc_ref)

    up = jnp.dot(hb_ref[...], wu_ref[...], preferred_element_type=F32)
    act = jnp.square(jnp.maximum(up, 0.0)).astype(BF16)
    acc_ref[...] += jnp.dot(act, wd_ref[...], preferred_element_type=F32)

    @pl.when(f == pl.num_programs(1) - 1)
    def _():
        xn = x_ref[...] + gate_ref[...] * acc_ref[...]
        if final_norm:
            ms = jnp.mean(xn * xn, axis=-1, keepdims=True)
            xn = (xn * lax.rsqrt(ms + RMS_EPS)) * gf_ref[...]
        o_ref[...] = xn


def _mlp(x2, g, sc, sh, gate, wu_bf16, wd_bf16, gfinal, tm, tf, rows_per_batch, final_norm):
    m, d = x2.shape
    ff = wu_bf16.shape[1]
    return pl.pallas_call(
        functools.partial(_mlp_body, final_norm=final_norm),
        grid=(m // tm, ff // tf),
        in_specs=[pl.BlockSpec((tm, d), lambda i, f: (i, 0)),
                  pl.BlockSpec((1, d), lambda i, f: (0, 0)),
                  _mod_spec(sc, tm, rows_per_batch, d),
                  _mod_spec(sh, tm, rows_per_batch, d),
                  _mod_spec(gate, tm, rows_per_batch, d),
                  pl.BlockSpec((d, tf), lambda i, f: (0, f)),
                  pl.BlockSpec((tf, d), lambda i, f: (f, 0)),
                  pl.BlockSpec((1, d), lambda i, f: (0, 0))],
        out_specs=pl.BlockSpec((tm, d), lambda i, f: (i, 0)),
        out_shape=jax.ShapeDtypeStruct((m, d), F32),
        scratch_shapes=[pltpu.VMEM((tm, d), BF16), pltpu.VMEM((tm, d), F32)],
        compiler_params=_cparams(("arbitrary", "arbitrary")),
        name="mlp",
    )(x2, g.reshape(1, d), sc, sh, gate, wu_bf16, wd_bf16, gfinal.reshape(1, d))


def _rwkv_prep_body(za_ref, shift_ref, mu_ref, w0_ref, wup_ref, a0_ref, aup_ref, gup_ref,
                    r_ref, k_ref, v_ref, lw_ref, a_ref, g_ref, carry_ref, *, d_a):
    @pl.when(pl.program_id(1) == 0)
    def _():
        carry_ref[...] = shift_ref[...]

    za = za_ref[...]
    tm = za.shape[0]
    row = lax.broadcasted_iota(jnp.int32, za.shape, 0)
    prev = jnp.where(row == 0, carry_ref[...], pltpu.roll(za, 1, axis=0))
    carry_ref[...] = za[tm - 1:tm, :]
    zs = za + mu_ref[...] * (prev - za)
    o_w = 3 * d_a
    o_a = o_w + LORA_W
    o_g = o_a + LORA_A
    r_ref[...] = zs[:, 0:d_a]
    k_ref[...] = zs[:, d_a:2 * d_a]
    v_ref[...] = zs[:, 2 * d_a:3 * d_a]
    dw = zs[:, o_w:o_a]
    da = zs[:, o_a:o_g]
    dg = zs[:, o_g:o_g + LORA_G]
    lw_ref[...] = -DECAY_SCALE * _sigmoid(w0_ref[...] + _bdot(jnp.tanh(dw), wup_ref[...]))
    a_ref[...] = _sigmoid(a0_ref[...] + _bdot(da, aup_ref[...]))
    g_ref[...] = _bdot(_sigmoid(dg), gup_ref[...])


def _rwkv_prep(za, shift_prev, mu, w0, w_up, a0, a_up, g_up, tm):
    b, t, dp = za.shape
    d_a = w0.shape[0]
    out = jax.ShapeDtypeStruct((b, t, d_a), F32)
    row = lambda n: pl.BlockSpec((1, n), lambda i, j: (0, 0))
    full = lambda w: pl.BlockSpec(w.shape, lambda i, j: (0, 0))
    tile = pl.BlockSpec((None, tm, d_a), lambda i, j: (i, j, 0))
    return pl.pallas_call(
        functools.partial(_rwkv_prep_body, d_a=d_a),
        grid=(b, t // tm),
        in_specs=[pl.BlockSpec((None, tm, dp), lambda i, j: (i, j, 0)),
                  pl.BlockSpec((None, 1, dp), lambda i, j: (i, 0, 0)),
                  row(dp), row(d_a), full(w_up), row(d_a), full(a_up), full(g_up)],
        out_specs=(tile,) * 6,
        out_shape=(out,) * 6,
        scratch_shapes=[pltpu.VMEM((1, dp), F32)],
        compiler_params=_cparams(("arbitrary", "arbitrary")),
        name="rwkv_prep",
    )(za, shift_prev.reshape(b, 1, dp), mu.reshape(1, dp), w0.reshape(1, d_a), w_up,
      a0.reshape(1, d_a), a_up, g_up)


def _rwkv_heads(r, k, v, lw, a, g, kkw, kaw, rkw, lnw, lnb, s0, valid, tri_incl, tri_strict,
                eye, n_doubling):
    hs = range(len(r))
    c = r[0].shape[0]
    each = lambda f, *ls: [f(*xs) for xs in zip(*ls)]

    kk = each(lambda k_, w: k_ * w, k, kkw)
    kk = each(lambda x: x / jnp.maximum(jnp.sqrt(jnp.sum(x * x, axis=-1, keepdims=True)), 1e-12), kk)
    k2 = each(lambda k_, a_, w: k_ * (1.0 + (a_ - 1.0) * w), k, a, kaw)
    b = each(lambda x, a_: x * a_, kk, a)
    if valid is not None:
        zero = lambda x: jnp.where(valid, x, 0.0)
        lw, kk, k2, b, v, r = (each(zero, x) for x in (lw, kk, k2, b, v, r))
    cum = each(lambda x: jnp.dot(tri_incl, x, precision=HIGHEST, preferred_element_type=F32), lw)
    tot = each(lambda x: x[c - 1:c, :], cum)
    e_in = each(jnp.exp, cum)
    e_out = each(lambda x: jnp.exp(-x), cum)
    e_end = each(lambda t_, x: jnp.exp(t_ - x), tot, cum)
    kap_t = each(lambda x, cu, l_: x * jnp.exp(cu - l_), kk, cum, lw)
    r_t = each(lambda x, e: x * e, r, e_in)
    b_t = each(lambda x, e: x * e, b, e_out)
    k_t = each(lambda x, e: x * e, k2, e_out)
    b_h = each(lambda x, e: x * e, b, e_end)
    k_h = each(lambda x, e: x * e, k2, e_end)

    rhs = each(lambda x, y: jnp.concatenate([x, y], axis=0), b_t, k_t)
    gk = each(_bdot_nt, kap_t, rhs)
    gr = each(_bdot_nt, r_t, rhs)
    n1 = each(lambda x: jnp.where(tri_strict, x[:, :c], 0.0), gk)
    n2 = each(lambda x: jnp.where(tri_strict, x[:, c:], 0.0), gk)
    m1 = each(lambda x: jnp.where(tri_incl > 0, x[:, :c], 0.0), gr)
    m2 = each(lambda x: jnp.where(tri_incl > 0, x[:, c:], 0.0), gr)

    x = each(lambda n: -n, n1)
    tinv = each(lambda x_: eye + x_, x)
    for _ in range(n_doubling):
        x = each(_bdot, x, x)
        tinv = each(lambda t_, x_: t_ + _bdot(t_, x_), tinv, x)

    nv = each(lambda n, m_, v_: _bdot(jnp.concatenate([n, m_], axis=0), v_), n2, m2, v)
    ty = each(lambda t_, kp, nv_: _bdot(t_, jnp.concatenate([kp, nv_[:c]], axis=1)),
              tinv, kap_t, nv)
    m1ty = each(_bdot, m1, ty)
    rq = each(lambda r_, m_: r_ - m_[:, :HEAD_A], r_t, m1ty)
    ol = each(lambda nv_, m_: nv_[c:] - m_[:, HEAD_A:], nv, m1ty)
    tb = each(_bdot_tn, ty, b_h)
    vk = each(_bdot_tn, v, k_h)

    o = each(lambda rq_, s_, ol_: _bdot_nt(rq_, s_) + ol_, rq, s0, ol)
    s_new = each(lambda s_, t_, tb_, vk_: s_ * jnp.exp(t_) - _bdot(s_, tb_[:HEAD_A])
                 + (vk_ - tb_[HEAD_A:]), s0, tot, tb, vk)

    def finish(o_, r_, k2_, v_, g_, rkw_, lnw_, lnb_):
        mu = jnp.mean(o_, axis=-1, keepdims=True)
        var = jnp.mean(jnp.square(o_ - mu), axis=-1, keepdims=True)
        on = (o_ - mu) * lax.rsqrt(var + GN_EPS) * lnw_ + lnb_
        bonus = jnp.sum(r_ * k2_ * rkw_, axis=-1, keepdims=True) * v_
        return (on + bonus) * g_

    y = each(finish, o, r, k2, v, g, rkw, lnw, lnb)
    return y, s_new


def _rwkv_chunk_body(r_ref, k_ref, v_ref, lw_ref, a_ref, g_ref, kkw_ref, kaw_ref, rkw_ref,
                     lnw_ref, lnb_ref, s0_ref, y_ref, sout_ref, s_ref, *, chunk, n_chunks,
                     n_heads, n_valid):
    @pl.when(pl.program_id(1) == 0)
    def _():
        s_ref[...] = s0_ref[...]

    c = chunk
    ri = lax.broadcasted_iota(jnp.int32, (c, c), 0)
    ci = lax.broadcasted_iota(jnp.int32, (c, c), 1)
    tri_incl = (ri >= ci).astype(F32)
    tri_strict = ri > ci
    eye = (ri == ci).astype(F32)
    valid = None
    if n_valid < c:
        valid = lax.broadcasted_iota(jnp.int32, (c, HEAD_A), 0) < n_valid
    n_doubling = max(int(math.log2(c)) - 1, 0)

    def one_chunk(ic, carry):
        t0 = pl.multiple_of(ic * c, c)
        rows = pl.ds(t0, c)
        sls = [slice(h * HEAD_A, (h + 1) * HEAD_A) for h in range(n_heads)]
        per_head = lambda ref: [ref[rows, sl] for sl in sls]
        per_head_row = lambda ref: [ref[:, sl] for sl in sls]
        y, s_new = _rwkv_heads(
            per_head(r_ref), per_head(k_ref), per_head(v_ref), per_head(lw_ref),
            per_head(a_ref), per_head(g_ref), per_head_row(kkw_ref), per_head_row(kaw_ref),
            per_head_row(rkw_ref), per_head_row(lnw_ref), per_head_row(lnb_ref),
            [s_ref[h] for h in range(n_heads)], valid, tri_incl, tri_strict, eye, n_doubling)
        for h, sl in enumerate(sls):
            y_ref[rows, sl] = y[h]
            s_ref[h] = s_new[h]
        return carry

    lax.fori_loop(0, n_chunks, one_chunk, 0)

    @pl.when(pl.program_id(1) == pl.num_programs(1) - 1)
    def _():
        sout_ref[...] = s_ref[...]


def _rwkv_chunked(r, k, v, lw, a, g, k_k, k_a, r_k, lnx_w, lnx_b, s0, tb, chunk, n_valid):
    b, t, d_a = r.shape
    n_heads = d_a // HEAD_A
    tile = pl.BlockSpec((None, tb, d_a), lambda i, j: (i, j, 0))
    row = pl.BlockSpec((1, d_a), lambda i, j: (0, 0))
    st = pl.BlockSpec((None, n_heads, HEAD_A, HEAD_A), lambda i, j: (i, 0, 0, 0))
    return pl.pallas_call(
        functools.partial(_rwkv_chunk_body, chunk=chunk, n_chunks=tb // chunk,
                          n_heads=n_heads, n_valid=n_valid),
        grid=(b, t // tb),
        in_specs=[tile] * 6 + [row] * 5 + [st],
        out_specs=(tile, st),
        out_shape=(jax.ShapeDtypeStruct((b, t, d_a), F32),
                   jax.ShapeDtypeStruct((b, n_heads, HEAD_A, HEAD_A), F32)),
        scratch_shapes=[pltpu.VMEM((n_heads, HEAD_A, HEAD_A), F32)],
        compiler_params=_cparams(("arbitrary", "arbitrary")),
        name="rwkv_chunked",
    )(r, k, v, lw, a, g, k_k.reshape(1, d_a), k_a.reshape(1, d_a), r_k.reshape(1, d_a),
      lnx_w.reshape(1, d_a), lnx_b.reshape(1, d_a), s0)


def _s5_disc_body(lr_ref, li_ref, ldt_ref, bre_ref, bim_ref, ar_ref, ai_ref, bbre_ref, bbim_ref):
    lr, li = lr_ref[...], li_ref[...]
    dt = jnp.exp(ldt_ref[...])
    mag = jnp.exp(lr * dt)
    ar = mag * jnp.cos(li * dt)
    ai = mag * jnp.sin(li * dt)
    den = lr * lr + li * li
    fr = ((ar - 1.0) * lr + ai * li) / den
    fi = (ai * lr - (ar - 1.0) * li) / den
    ar_ref[...] = ar
    ai_ref[...] = ai
    b_re, b_im = bre_ref[...], bim_ref[...]
    bbre_ref[...] = fr[:, None, :] * b_re - fi[:, None, :] * b_im
    bbim_ref[...] = fr[:, None, :] * b_im + fi[:, None, :] * b_re


def _s5_discretise(lam_re, lam_im, log_dt, b_re, b_im):
    g, p = lam_re.shape
    n = b_re.shape[2]
    bt = lambda x: jnp.swapaxes(x, 1, 2)
    return pl.pallas_call(
        _s5_disc_body,
        out_shape=(jax.ShapeDtypeStruct((g, p), F32), jax.ShapeDtypeStruct((g, p), F32),
                   jax.ShapeDtypeStruct((g, n, p), F32), jax.ShapeDtypeStruct((g, n, p), F32)),
        name="s5_discretise",
    )(lam_re, lam_im, log_dt.reshape(g, 1), bt(b_re), bt(b_im))


def _gelu_tanh(x):
    c = math.sqrt(2.0 / math.pi)
    return 0.5 * x * (1.0 + jnp.tanh(c * (x + 0.044715 * (x * x * x))))


def _s5_body(u_ref, h0r_ref, h0i_ref, ar_ref, ai_ref, bdre_ref, bdim_ref, cdre_ref, cdim_ref,
             d_ref, wglu_ref, bglu_ref, y_ref, hre_ref, him_ref,
             xr_ref, xi_ref, apr_ref, api_ref, cr_ref, ci_ref, *, last_row):
    j = pl.program_id(1)
    rows, width = xr_ref.shape
    half_u = u_ref.shape[1] // 2
    half_x = width // 2

    ar, ai = ar_ref[...], ai_ref[...]
    a2r, a2i = ar * ar - ai * ai, 2.0 * ar * ai
    a4r, a4i = a2r * a2r - a2i * a2i, 2.0 * a2r * a2i

    @pl.when(j == 0)
    def _():
        cr_ref[...] = h0r_ref[...]
        ci_ref[...] = h0i_ref[...]
        pr, pi = ar, ai
        for s in range(SUBLANES):
            apr_ref[s:s + 1, :] = pr
            api_ref[s:s + 1, :] = pi
            pr, pi = pr * ar - pi * ai, pr * ai + pi * ar

    ub = u_ref[...].astype(BF16)
    for hf in range(2):
        us = ub[:, hf * half_u:(hf + 1) * half_u]
        cols = slice(hf * half_x, (hf + 1) * half_x)
        xr_ref[:, cols] = jnp.dot(us, bdre_ref[hf], preferred_element_type=F32)
        xi_ref[:, cols] = jnp.dot(us, bdim_ref[hf], preferred_element_type=F32)

    lane_chunk = 4 * LANES
    sub = lax.broadcasted_iota(jnp.int32, (rows, lane_chunk), 0) % SUBLANES
    for c0 in range(0, width, lane_chunk):
        cols = slice(c0, c0 + lane_chunk)
        xr, xi = xr_ref[:, cols], xi_ref[:, cols]
        for d, (pr, pi) in ((1, (ar, ai)), (2, (a2r, a2i)), (4, (a4r, a4i))):
            pr, pi = pr[:, cols], pi[:, cols]
            sr = pltpu.roll(xr, d, axis=0)
            si = pltpu.roll(xi, d, axis=0)
            keep = sub >= d
            xr, xi = (xr + jnp.where(keep, pr * sr - pi * si, 0.0),
                      xi + jnp.where(keep, pr * si + pi * sr, 0.0))
        xr_ref[:, cols] = xr
        xi_ref[:, cols] = xi

    def group(n, carry):
        cr, ci = carry
        r0 = pl.multiple_of(n * SUBLANES, SUBLANES)
        rs = pl.ds(r0, SUBLANES)
        apr, api = apr_ref[...], api_ref[...]
        hr = xr_ref[rs, :] + apr * cr - api * ci
        hi = xi_ref[rs, :] + apr * ci + api * cr
        xr_ref[rs, :] = hr
        xi_ref[rs, :] = hi
        return hr[SUBLANES - 1:SUBLANES, :], hi[SUBLANES - 1:SUBLANES, :]

    cr, ci = lax.fori_loop(0, rows // SUBLANES, group, (cr_ref[...], ci_ref[...]))
    cr_ref[...] = cr
    ci_ref[...] = ci

    @pl.when(j == pl.num_programs(1) - 1)
    def _():
        hre_ref[...] = xr_ref[last_row:last_row + 1, :]
        him_ref[...] = xi_ref[last_row:last_row + 1, :]

    u = u_ref[...]
    for hf in range(2):
        cols = slice(hf * half_x, (hf + 1) * half_x)
        ucols = slice(hf * half_u, (hf + 1) * half_u)
        y = (jnp.dot(xr_ref[:, cols].astype(BF16), cdre_ref[hf], preferred_element_type=F32)
             - jnp.dot(xi_ref[:, cols].astype(BF16), cdim_ref[hf], preferred_element_type=F32)
             + d_ref[:, ucols] * u[:, ucols])
        y_ref[:, ucols] = _gelu_tanh(y)
    y = y_ref[...]
    z = jnp.dot(y.astype(BF16), wglu_ref[...], preferred_element_type=F32) + bglu_ref[...]
    y_ref[...] = y * _sigmoid(z)


def _block_diag_halves(w_gab, transpose):
    g = w_gab.shape[0]
    hg = g // 2
    eye = jnp.eye(hg, dtype=w_gab.dtype)
    halves = []
    for hf in range(2):
        w = w_gab[hf * hg:(hf + 1) * hg]
        if transpose:
            w = jnp.swapaxes(w, 1, 2)
        a, b = w.shape[1], w.shape[2]
        halves.append(jnp.einsum('gab,gh->gahb', w, eye).reshape(hg * a, hg * b))
    return jnp.stack(halves)


def _s5_mix(u, h0_re, h0_im, ar, ai, bbt_re, bbt_im, c_re, c_im, d, w_glu, b_glu, rows, n_valid):
    b, t, d_b = u.shape
    g, p = ar.shape
    width = g * p
    bd_re = _block_diag_halves(bbt_re, False).astype(BF16)
    bd_im = _block_diag_halves(bbt_im, False).astype(BF16)
    cd_re = _block_diag_halves(c_re, True).astype(BF16)
    cd_im = _block_diag_halves(c_im, True).astype(BF16)
    n_blocks = t // rows
    last_row = (n_valid - 1) - (n_blocks - 1) * rows
    full = lambda x: pl.BlockSpec(x.shape, lambda i, j: (0,) * x.ndim)
    st = pl.BlockSpec((None, 1, width), lambda i, j: (i, 0, 0))
    args = (u, h0_re.reshape(b, 1, width), h0_im.reshape(b, 1, width),
            ar.reshape(1, width), ai.reshape(1, width), bd_re, bd_im, cd_re, cd_im,
            d.reshape(1, d_b), w_glu.astype(BF16), b_glu.reshape(1, d_b))
    return pl.pallas_call(
        functools.partial(_s5_body, last_row=last_row),
        grid=(b, n_blocks),
        in_specs=[pl.BlockSpec((None, rows, d_b), lambda i, j: (i, j, 0)), st, st]
                 + [full(x) for x in args[3:]],
        out_specs=(pl.BlockSpec((None, rows, d_b), lambda i, j: (i, j, 0)), st, st),
        out_shape=(jax.ShapeDtypeStruct((b, t, d_b), F32),
                   jax.ShapeDtypeStruct((b, 1, width), F32),
                   jax.ShapeDtypeStruct((b, 1, width), F32)),
        scratch_shapes=[pltpu.VMEM((rows, width), F32), pltpu.VMEM((rows, width), F32),
                        pltpu.VMEM((SUBLANES, width), F32), pltpu.VMEM((SUBLANES, width), F32),
                        pltpu.VMEM((1, width), F32), pltpu.VMEM((1, width), F32)],
        compiler_params=_cparams(("arbitrary", "arbitrary")),
        name="s5_mix",
    )(*args)


def _lambda_full(lq1, lk1, lq2, lk2):
    s1 = jnp.sum(lq1 * lk1, axis=-1, keepdims=True)
    s2 = jnp.sum(lq2 * lk2, axis=-1, keepdims=True)
    return jnp.exp(s1) - jnp.exp(s2) + LAMBDA_INIT


def _sub_ln(o, subln):
    ms = jnp.mean(o * o, axis=-1, keepdims=True)
    return o * lax.rsqrt(ms + RMS_EPS) * subln * (1.0 - LAMBDA_INIT)


def _attn_prompt_body(q_ref, k_ref, v_ref, lq1_ref, lk1_ref, lq2_ref, lk2_ref, subln_ref,
                      o_ref, kb_ref, vt_ref, m_ref, l_ref, acc_ref, *, tile, n_tiles):
    qi = pl.program_id(2)

    @pl.when(qi == 0)
    def _():
        for c in range(n_tiles):
            rows = slice(c * tile, (c + 1) * tile)
            kb_ref[rows, :] = k_ref[rows, :].astype(BF16)
            vt_ref[c] = v_ref[rows, :].T.astype(BF16)

    q = q_ref[...] * (SCALE_C * math.log2(math.e))
    lane = lax.broadcasted_iota(jnp.int32, q.shape, 1)
    qs = jnp.concatenate([jnp.where(lane < HALF_C, q, 0.0),
                          jnp.where(lane >= HALF_C, q, 0.0)], axis=0).astype(BF16)
    m_ref[...] = jnp.full_like(m_ref, NEG_INF)
    l_ref[...] = jnp.zeros_like(l_ref)
    acc_ref[...] = jnp.zeros_like(acc_ref)

    def kv_step(kj, diagonal):
        k0 = pl.multiple_of(kj * tile, tile)
        st = lax.dot_general(kb_ref[pl.ds(k0, tile), :], qs, (((1,), (1,)), ((), ())),
                             preferred_element_type=F32)
        if diagonal:
            krow = lax.broadcasted_iota(jnp.int32, st.shape, 0)
            qcol = lax.broadcasted_iota(jnp.int32, st.shape, 1) % tile
            st = jnp.where(krow <= qcol, st, NEG_INF)
        m_prev = m_ref[...]
        m_new = jnp.maximum(m_prev, jnp.max(st, axis=0, keepdims=True))
        alpha = jnp.exp2(m_prev - m_new)
        p = jnp.exp2(st - m_new)
        l_ref[...] = alpha * l_ref[...] + jnp.sum(p, axis=0, keepdims=True)
        acc_ref[...] = alpha * acc_ref[...] + jnp.dot(vt_ref[kj], p.astype(BF16),
                                                      preferred_element_type=F32)
        m_ref[...] = m_new

    def full_step(kj, carry):
        kv_step(kj, False)
        return carry

    lax.fori_loop(0, qi, full_step, 0)
    kv_step(qi, True)

    lam = _lambda_full(lq1_ref[...], lk1_ref[...], lq2_ref[...], lk2_ref[...])
    on = acc_ref[...] / l_ref[...]
    ot = on[:, :tile] - lam * on[:, tile:]
    ms = jnp.mean(ot * ot, axis=0, keepdims=True)
    ot = ot * lax.rsqrt(ms + RMS_EPS) * subln_ref[...] * (1.0 - LAMBDA_INIT)
    o_ref[...] = ot.T


def _attn_prompt(q, k, v, lq1, lk1, lq2, lk2, subln, tile):
    b, t, d = q.shape
    n_heads = d // HEAD_C
    n_tiles = t // tile
    small = lambda n: pl.BlockSpec((1, n), lambda i, h, j: (0, 0))
    return pl.pallas_call(
        functools.partial(_attn_prompt_body, tile=tile, n_tiles=n_tiles),
        grid=(b, n_heads, n_tiles),
        in_specs=[pl.BlockSpec((None, tile, HEAD_C), lambda i, h, j: (i, j, h)),
                  pl.BlockSpec((None, t, HEAD_C), lambda i, h, j: (i, 0, h)),
                  pl.BlockSpec((None, t, HEAD_C), lambda i, h, j: (i, 0, h)),
                  small(HALF_C), small(HALF_C), small(HALF_C), small(HALF_C),
                  pl.BlockSpec((HEAD_C, 1), lambda i, h, j: (0, 0))],
        out_specs=pl.BlockSpec((None, tile, HEAD_C), lambda i, h, j: (i, j, h)),
        out_shape=jax.ShapeDtypeStruct((b, t, d), F32),
        scratch_shapes=[pltpu.VMEM((t, HEAD_C), BF16), pltpu.VMEM((n_tiles, HEAD_C, tile), BF16),
                        pltpu.VMEM((1, 2 * tile), F32), pltpu.VMEM((1, 2 * tile), F32),
                        pltpu.VMEM((HEAD_C, 2 * tile), F32)],
        compiler_params=_cparams(("arbitrary", "arbitrary", "arbitrary")),
        name="attn_prompt",
    )(q, k, v, lq1.reshape(1, -1), lk1.reshape(1, -1), lq2.reshape(1, -1), lk2.reshape(1, -1),
      subln.reshape(-1, 1))


def _attn_sample_body(pt_ref, qrep_ref, knew_ref, vnew_ref, *rest, pages_per_step, n_heads,
                      t_new, page):
    kp_refs = rest[:pages_per_step]
    vp_refs = rest[pages_per_step:2 * pages_per_step]
    lq1_ref, lk1_ref, lq2_ref, lk2_ref, subln_ref = rest[2 * pages_per_step:2 * pages_per_step + 5]
    o_ref = rest[2 * pages_per_step + 5]
    qb_ref, m_ref, l_ref, acc_ref = rest[2 * pages_per_step + 6:]
    j = pl.program_id(1)
    heads = range(n_heads)

    @pl.when(j == 0)
    def _():
        qr = qrep_ref[...]
        row = lax.broadcasted_iota(jnp.int32, qr.shape, 1)
        lane = lax.broadcasted_iota(jnp.int32, qr.shape, 2)
        qb_ref[...] = jnp.where(lane // HALF_C == row // t_new, qr * SCALE_C, 0.0).astype(BF16)
        m_ref[...] = jnp.full_like(m_ref, NEG_INF)
        l_ref[...] = jnp.zeros_like(l_ref)
        acc_ref[...] = jnp.zeros_like(acc_ref)

    def head_rows(ref, h):
        return ref[pl.ds(h, page, stride=n_heads), :].astype(BF16)

    def update(s, values):
        m_prev = [m_ref[h] for h in heads]
        m_new = [jnp.maximum(m_prev[h], jnp.max(s[h], axis=-1, keepdims=True)) for h in heads]
        alpha = [jnp.exp(m_prev[h] - m_new[h]) for h in heads]
        p = [jnp.exp(s[h] - m_new[h]) for h in heads]
        pv = []
        for h in heads:
            w = p[h].shape[1] // len(values[h])
            acc = None
            for i, vb in enumerate(values[h]):
                d = jnp.dot(p[h][:, i * w:(i + 1) * w].astype(BF16), vb,
                            preferred_element_type=F32)
                acc = d if acc is None else acc + d
            pv.append(acc)
        for h in heads:
            l_ref[h] = alpha[h] * l_ref[h] + jnp.sum(p[h], axis=-1, keepdims=True)
            acc_ref[h] = alpha[h] * acc_ref[h] + pv[h]
            m_ref[h] = m_new[h]

    s = [jnp.concatenate(
            [lax.dot_general(qb_ref[h], head_rows(kp, h), (((1,), (1,)), ((), ())),
                             preferred_element_type=F32) for kp in kp_refs], axis=1)
         for h in heads]
    update(s, [[head_rows(vp, h) for vp in vp_refs] for h in heads])

    @pl.when(j == pl.num_programs(1) - 1)
    def _():
        sn = []
        for h in heads:
            x = lax.dot_general(qb_ref[h], knew_ref[h].astype(BF16), (((1,), (1,)), ((), ())),
                                preferred_element_type=F32)
            trow = lax.broadcasted_iota(jnp.int32, x.shape, 0) % t_new
            tcol = lax.broadcasted_iota(jnp.int32, x.shape, 1)
            sn.append(jnp.where(tcol <= trow, x, NEG_INF))
        update(sn, [[vnew_ref[h].astype(BF16)] for h in heads])
        lam = _lambda_full(lq1_ref[...], lk1_ref[...], lq2_ref[...], lk2_ref[...])
        for h in heads:
            on = acc_ref[h] / l_ref[h]
            o = on[0:t_new] - lam * on[t_new:2 * t_new]
            o_ref[:, h * HEAD_C:(h + 1) * HEAD_C] = _sub_ln(o, subln_ref[...])


def _attn_sample(q, k_new, v_new, cache_k, cache_v, page_table, lq1, lk1, lq2, lk2, subln,
                 pages_per_step):
    db, t_new, d = q.shape
    n_heads = d // HEAD_C
    n_pool, page = cache_k.shape[0], cache_k.shape[1]
    n_pages = page_table.shape[1]
    ck = cache_k.reshape(n_pool, page * n_heads, HEAD_C)
    cv = cache_v.reshape(n_pool, page * n_heads, HEAD_C)
    by_head = lambda x: jnp.swapaxes(x.reshape(db, t_new, n_heads, HEAD_C), 1, 2)
    q_rows = NEW_PAD
    qrep = jnp.tile(by_head(q), (1, 1, q_rows // t_new, 1))
    pad = ((0, 0), (0, 0), (0, NEW_PAD - t_new), (0, 0))
    knew = jnp.pad(by_head(k_new), pad)
    vnew = jnp.pad(by_head(v_new), pad)
    pt = page_table.reshape(-1)

    def page_spec(i):
        return pl.BlockSpec((None, page * n_heads, HEAD_C),
                            lambda b, j, pt_ref: (pt_ref[b * n_pages + j * pages_per_step + i], 0, 0))

    small = lambda n: pl.BlockSpec((1, n), lambda b, j, pt_ref: (0, 0))
    per_b = lambda r: pl.BlockSpec((None, n_heads, r, HEAD_C), lambda b, j, pt_ref: (b, 0, 0, 0))
    grid_spec = pltpu.PrefetchScalarGridSpec(
        num_scalar_prefetch=1,
        grid=(db, n_pages // pages_per_step),
        in_specs=[per_b(q_rows), per_b(NEW_PAD), per_b(NEW_PAD)]
                 + [page_spec(i) for i in range(pages_per_step)] * 2
                 + [small(HALF_C)] * 4 + [small(HEAD_C)],
        out_specs=pl.BlockSpec((None, t_new, d), lambda b, j, pt_ref: (b, 0, 0)),
        scratch_shapes=[pltpu.VMEM((n_heads, q_rows, HEAD_C), BF16),
                        pltpu.VMEM((n_heads, q_rows, 1), F32),
                        pltpu.VMEM((n_heads, q_rows, 1), F32),
                        pltpu.VMEM((n_heads, q_rows, HEAD_C), F32)])
    return pl.pallas_call(
        functools.partial(_attn_sample_body, pages_per_step=pages_per_step, n_heads=n_heads,
                          t_new=t_new, page=page),
        grid_spec=grid_spec,
        out_shape=jax.ShapeDtypeStruct((db, t_new, d), F32),
        compiler_params=_cparams(("arbitrary", "arbitrary")),
        name="attn_sample",
    )(pt, qrep, knew, vnew, *([ck] * pages_per_step), *([cv] * pages_per_step),
      lq1.reshape(1, -1), lk1.reshape(1, -1), lq2.reshape(1, -1), lk2.reshape(1, -1),
      subln.reshape(1, -1))


def _pad_time(x, t_pad):
    return jnp.pad(x, ((0, 0), (0, t_pad - x.shape[1]), (0, 0)))


def _trunk(x, mods, shift0, wkv0, sre0, sim0, attend, W, per_row_mod):
    b, t, d = x.shape
    m = b * t
    tm = min(512, m)
    d_a = W['rwkv_w0'].shape[0]
    d_ap = W['rwkv_mu'].shape[0]

    def mod_arg(v):
        if per_row_mod:
            return jnp.repeat(v, t, axis=0)
        return v.reshape(b, 1, d)

    x2 = x.reshape(m, d)

    sh_m, sc_m, g_m, sh_f, sc_f, g_f = (mod_arg(v) for v in mods[0])
    za2, u2 = _normmod_mm(x2, W['norm_mix'][0], sc_m, sh_m, W['w_in_ab_bf16'],
                          ((0, d_ap), (d_ap, W['w_in_ab_bf16'].shape[1])), tm, t)
    za = za2.reshape(b, t, d_ap)
    u = u2.reshape(b, t, -1)
    shift1 = za[:, -1]

    chunk = RWKV_CHUNK
    t_pad = -(-t // chunk) * chunk
    if t_pad != t:
        za_p, u_p = _pad_time(za, t_pad), _pad_time(u, t_pad)
    else:
        za_p, u_p = za, u
    tb = min(512, t_pad)
    r, k, v, lw, a, g = _rwkv_prep(za_p, shift0, W['rwkv_mu'], W['rwkv_w0'], W['rwkv_w_up'],
                                   W['rwkv_a0'], W['rwkv_a_up'], W['rwkv_g_up'], tb)
    y_a, wkv1 = _rwkv_chunked(r, k, v, lw, a, g, W['rwkv_k_k'], W['rwkv_k_a'], W['rwkv_r_k'],
                              W['rwkv_lnx_w'], W['rwkv_lnx_b'], wkv0, tb, chunk,
                              chunk if t_pad == t else t)
    s5_rows = min(256, t_pad)
    y_b, sre1, sim1 = _s5_mix(u_p, sre0, sim0, W['s5_ar'], W['s5_ai'], W['s5_bbt_re'],
                              W['s5_bbt_im'], W['s5_c_re'], W['s5_c_im'], W['s5_d'],
                              W['s5_w_glu'], W['s5_b_glu'], s5_rows, t)
    y_a2 = y_a[:, :t].reshape(m, d_a)
    y_b2 = y_b[:, :t].reshape(m, -1)
    wo = W['w_out_bf16'][0]
    x2 = _proj_res(x2, g_m, (y_a2, y_b2), (wo[:d_a], wo[d_a:]), tm, t)
    x2 = _mlp(x2, W['norm_mlp'][0], sc_f, sh_f, g_f, W['w_up_bf16'][0], W['w_down_bf16'][0],
              W['norm_f'], tm, 1024, t, False)

    sh_m, sc_m, g_m, sh_f, sc_f, g_f = (mod_arg(v) for v in mods[1])
    q2, k2, v2 = _normmod_mm(x2, W['norm_mix'][1], sc_m, sh_m, W['diff_w_qkv_bf16'],
                             ((0, d), (d, 2 * d), (2 * d, 3 * d)), tm, t)
    q3, k3, v3 = (z.reshape(b, t, d) for z in (q2, k2, v2))
    o = attend(q3, k3, v3)
    x2 = _proj_res(x2, g_m, (o.reshape(m, d),), (W['w_out_bf16'][1],), tm, t)
    y2 = _mlp(x2, W['norm_mlp'][1], sc_f, sh_f, g_f, W['w_up_bf16'][1], W['w_down_bf16'][1],
              W['norm_f'], tm, 1024, t, True)

    n_heads_c = d // HEAD_C
    g_b, p_b = W['s5_ar'].shape
    return (y2.reshape(b, t, d), shift1, wkv1, sre1.reshape(b, g_b, p_b), sim1.reshape(b, g_b, p_b),
            k3.reshape(b, t, n_heads_c, HEAD_C), v3.reshape(b, t, n_heads_c, HEAD_C))


def kernel(x_prompt, x_sample, state_shift, state_wkv, state_ssm_re, state_ssm_im, cache_k, cache_v, page_table, c_prompt, c_sample, norm_mix, norm_mlp, norm_f, w_ada, b_ada, w_out, w_up, w_down, w_in_ab, rwkv_mu, rwkv_w0, rwkv_w_up, rwkv_a0, rwkv_a_up, rwkv_g_up, rwkv_k_k, rwkv_k_a, rwkv_r_k, rwkv_lnx_w, rwkv_lnx_b, s5_lam_re, s5_lam_im, s5_log_dt, s5_b_re, s5_b_im, s5_c_re, s5_c_im, s5_d, s5_w_glu, s5_b_glu, diff_w_qkv, diff_lq1, diff_lk1, diff_lq2, diff_lk2, diff_subln):
    bp, tp, d = x_prompt.shape
    db, ts, _ = x_sample.shape
    depth = w_ada.shape[0]
    n_heads_a = rwkv_r_k.shape[0]
    g_b, p_b = s5_lam_re.shape

    ar, ai, bbt_re, bbt_im = _s5_discretise(s5_lam_re, s5_lam_im, s5_log_dt, s5_b_re, s5_b_im)
    W = dict(norm_mix=norm_mix, norm_mlp=norm_mlp, norm_f=norm_f,
             w_out_bf16=w_out.astype(BF16), w_up_bf16=w_up.astype(BF16),
             w_down_bf16=w_down.astype(BF16), w_in_ab_bf16=w_in_ab.astype(BF16),
             diff_w_qkv_bf16=diff_w_qkv.astype(BF16),
             rwkv_mu=rwkv_mu, rwkv_w0=rwkv_w0, rwkv_w_up=rwkv_w_up, rwkv_a0=rwkv_a0,
             rwkv_a_up=rwkv_a_up, rwkv_g_up=rwkv_g_up, rwkv_k_k=rwkv_k_k, rwkv_k_a=rwkv_k_a,
             rwkv_r_k=rwkv_r_k.reshape(-1), rwkv_lnx_w=rwkv_lnx_w, rwkv_lnx_b=rwkv_lnx_b,
             s5_ar=ar, s5_ai=ai, s5_bbt_re=bbt_re, s5_bbt_im=bbt_im,
             s5_c_re=s5_c_re, s5_c_im=s5_c_im, s5_d=s5_d.reshape(-1), s5_w_glu=s5_w_glu,
             s5_b_glu=s5_b_glu)

    n_c = bp + db
    rows_c = -(-n_c // SUBLANES) * SUBLANES
    c_all = jnp.pad(jnp.concatenate([c_prompt, c_sample], axis=0), ((0, rows_c - n_c), (0, 0)))
    mod = _ada_mod(c_all, w_ada, b_ada)
    mods_p = [tuple(mod[l, :bp, i * d:(i + 1) * d] for i in range(6)) for l in range(depth)]
    mods_s = [tuple(mod[l, bp:n_c, i * d:(i + 1) * d] for i in range(6)) for l in range(depth)]

    lam_args = (diff_lq1, diff_lk1, diff_lq2, diff_lk2, diff_subln)
    attend_p = lambda q, k, v: _attn_prompt(q, k, v, *lam_args, tile=min(256, tp))
    y_prompt, p_shift, p_wkv, p_re, p_im, p_k, p_v = _trunk(
        x_prompt, mods_p,
        jnp.zeros((bp, rwkv_mu.shape[0]), F32),
        jnp.zeros((bp, n_heads_a, HEAD_A, HEAD_A), F32),
        jnp.zeros((bp, g_b, p_b), F32), jnp.zeros((bp, g_b, p_b), F32),
        attend_p, W, per_row_mod=False)

    attend_s = lambda q, k, v: _attn_sample(q, k, v, cache_k, cache_v, page_table, *lam_args,
                                            pages_per_step=4)
    y_sample, s_shift, s_wkv, s_re, s_im, s_k, s_v = _trunk(
        x_sample, mods_s, state_shift, state_wkv, state_ssm_re, state_ssm_im,
        attend_s, W, per_row_mod=True)

    return (y_prompt, y_sample, p_shift, p_wkv, p_re, p_im, p_k, p_v,
            s_shift, s_wkv, s_re, s_im, s_k, s_v)
```

```python
import functools
import math

import jax
import jax.numpy as jnp
from jax import lax
from jax.experimental import pallas as pl
from jax.experimental.pallas import tpu as pltpu

F32 = jnp.float32
BF16 = jnp.bfloat16
HIGHEST = lax.Precision.HIGHEST

HEAD_A = 64
LORA_W, LORA_A, LORA_G = 64, 64, 128
S5_GROUP = 16
S5_STATE = 64
HEAD_C = 128
HALF_C = HEAD_C // 2
SCALE_C = HALF_C ** -0.5
RMS_EPS = 1e-6
GN_EPS = 64e-5
DECAY_SCALE = math.exp(-0.5)
LAMBDA_INIT = 0.8 - 0.6 * math.exp(-0.3 * 1)
NEG_INF = -1e30

SUBLANES = 8
LANES = 128
MXU_DIM = 256
VMEM_LIMIT = 56 * 1024 * 1024

RWKV_CHUNK = 64
NEW_PAD = 16


def _cparams(sem):
    return pltpu.CompilerParams(dimension_semantics=sem, vmem_limit_bytes=VMEM_LIMIT)


def _bdot(a, b):
    return jnp.dot(a.astype(BF16), b.astype(BF16), preferred_element_type=F32)


def _bdot_nt(a, b):
    return lax.dot_general(a.astype(BF16), b.astype(BF16), (((1,), (1,)), ((), ())),
                           preferred_element_type=F32)


def _bdot_tn(a, b):
    return lax.dot_general(a.astype(BF16), b.astype(BF16), (((0,), (0,)), ((), ())),
                           preferred_element_type=F32)


def _sigmoid(x):
    return 1.0 / (1.0 + jnp.exp(-x))


def _ada_body(c_ref, w_ref, b_ref, o_ref):
    c = c_ref[...]
    cond = c * _sigmoid(c)
    o_ref[...] = jnp.dot(cond, w_ref[...], precision=HIGHEST,
                         preferred_element_type=F32) + b_ref[...]


def _ada_mod(c, w_ada, b_ada):
    depth, d, n = w_ada.shape
    rows = c.shape[0]
    tn = 1536
    return pl.pallas_call(
        _ada_body,
        grid=(depth, n // tn),
        in_specs=[pl.BlockSpec((rows, d), lambda l, j: (0, 0)),
                  pl.BlockSpec((None, d, tn), lambda l, j: (l, 0, j)),
                  pl.BlockSpec((None, 1, tn), lambda l, j: (l, 0, j))],
        out_specs=pl.BlockSpec((None, rows, tn), lambda l, j: (l, 0, j)),
        out_shape=jax.ShapeDtypeStruct((depth, rows, n), F32),
        compiler_params=_cparams(("arbitrary", "arbitrary")),
        name="ada_mod",
    )(c, w_ada, b_ada.reshape(depth, 1, n))


def _mod_spec(mod, tm, rows_per_batch, d):
    if mod.ndim == 3:
        tiles = rows_per_batch // tm
        return pl.BlockSpec((None, 1, d), lambda i, *_: (i // tiles, 0, 0))
    return pl.BlockSpec((tm, d), lambda i, *_: (i, 0))


def _norm_mod(x, g, sc, sh):
    ms = jnp.mean(x * x, axis=-1, keepdims=True)
    h = (x * lax.rsqrt(ms + RMS_EPS)) * g
    return h * (1.0 + sc) + sh


def _normmod_mm_body(x_ref, g_ref, sc_ref, sh_ref, w_ref, *o_refs, splits, col_chunk):
    hb = _norm_mod(x_ref[...], g_ref[...], sc_ref[...], sh_ref[...]).astype(BF16)
    for o_ref, (c0, c1) in zip(o_refs, splits):
        for s in range(c0, c1, col_chunk):
            e = min(s + col_chunk, c1)
            o_ref[:, s - c0:e - c0] = jnp.dot(hb, w_ref[:, s:e], preferred_element_type=F32)


def _normmod_mm(x2, g, sc, sh, w_bf16, splits, tm, rows_per_batch):
    m, d = x2.shape
    n = w_bf16.shape[1]
    outs = tuple(jax.ShapeDtypeStruct((m, c1 - c0), F32) for c0, c1 in splits)
    return pl.pallas_call(
        functools.partial(_normmod_mm_body, splits=splits, col_chunk=512),
        grid=(m // tm,),
        in_specs=[pl.BlockSpec((tm, d), lambda i: (i, 0)),
                  pl.BlockSpec((1, d), lambda i: (0, 0)),
                  _mod_spec(sc, tm, rows_per_batch, d),
                  _mod_spec(sh, tm, rows_per_batch, d),
                  pl.BlockSpec((d, n), lambda i: (0, 0))],
        out_specs=tuple(pl.BlockSpec((tm, c1 - c0), lambda i: (i, 0)) for c0, c1 in splits),
        out_shape=outs,
        compiler_params=_cparams(("arbitrary",)),
        name="normmod_mm",
    )(x2, g.reshape(1, d), sc, sh, w_bf16)


def _proj_res_body(*refs, n_in):
    x_ref, gate_ref = refs[0], refs[1]
    y_refs = refs[2:2 + n_in]
    w_refs = refs[2 + n_in:2 + 2 * n_in]
    o_ref = refs[2 + 2 * n_in]
    acc = None
    for y_ref, w_ref in zip(y_refs, w_refs):
        p = jnp.dot(y_ref[...].astype(BF16), w_ref[...], preferred_element_type=F32)
        acc = p if acc is None else acc + p
    o_ref[...] = x_ref[...] + gate_ref[...] * acc


def _proj_res(x2, gate, ys, ws_bf16, tm, rows_per_batch):
    m, d = x2.shape
    n_in = len(ys)
    in_specs = [pl.BlockSpec((tm, d), lambda i: (i, 0)), _mod_spec(gate, tm, rows_per_batch, d)]
    in_specs += [pl.BlockSpec((tm, y.shape[1]), lambda i: (i, 0)) for y in ys]
    in_specs += [pl.BlockSpec(w.shape, lambda i: (0, 0)) for w in ws_bf16]
    return pl.pallas_call(
        functools.partial(_proj_res_body, n_in=n_in),
        grid=(m // tm,),
        in_specs=in_specs,
        out_specs=pl.BlockSpec((tm, d), lambda i: (i, 0)),
        out_shape=jax.ShapeDtypeStruct((m, d), F32),
        compiler_params=_cparams(("arbitrary",)),
        name="proj_res",
    )(x2, gate, *ys, *ws_bf16)


def _mlp_body(x_ref, g_ref, sc_ref, sh_ref, gate_ref, wu_ref, wd_ref, gf_ref, o_ref,
              hb_ref, acc_ref, *, final_norm):
    f = pl.program_id(1)

    @pl.when(f == 0)
    def _():
        hb_ref[...] = _norm_mod(x_ref[...], g_ref[...], sc_ref[...], sh_ref[...]).astype(BF16)
        acc_ref[...] = jnp.zeros_like(acc_ref)

    up = jnp.dot(hb_ref[...], wu_ref[...], preferred_element_type=F32)
    act = jnp.square(jnp.maximum(up, 0.0)).astype(BF16)
    acc_ref[...] += jnp.dot(act, wd_ref[...], preferred_element_type=F32)

    @pl.when(f == pl.num_programs(1) - 1)
    def _():
        xn = x_ref[...] + gate_ref[...] * acc_ref[...]
        if final_norm:
            ms = jnp.mean(xn * xn, axis=-1, keepdims=True)
            xn = (xn * lax.rsqrt(ms + RMS_EPS)) * gf_ref[...]
        o_ref[...] = xn


def _mlp(x2, g, sc, sh, gate, wu_bf16, wd_bf16, gfinal, tm, tf, rows_per_batch, final_norm):
    m, d = x2.shape
    ff = wu_bf16.shape[1]
    return pl.pallas_call(
        functools.partial(_mlp_body, final_norm=final_norm),
        grid=(m // tm, ff // tf),
        in_specs=[pl.BlockSpec((tm, d), lambda i, f: (i, 0)),
                  pl.BlockSpec((1, d), lambda i, f: (0, 0)),
                  _mod_spec(sc, tm, rows_per_batch, d),
                  _mod_spec(sh, tm, rows_per_batch, d),
                  _mod_spec(gate, tm, rows_per_batch, d),
                  pl.BlockSpec((d, tf), lambda i, f: (0, f)),
                  pl.BlockSpec((tf, d), lambda i, f: (f, 0)),
                  pl.BlockSpec((1, d), lambda i, f: (0, 0))],
        out_specs=pl.BlockSpec((tm, d), lambda i, f: (i, 0)),
        out_shape=jax.ShapeDtypeStruct((m, d), F32),
        scratch_shapes=[pltpu.VMEM((tm, d), BF16), pltpu.VMEM((tm, d), F32)],
        compiler_params=_cparams(("arbitrary", "arbitrary")),
        name="mlp",
    )(x2, g.reshape(1, d), sc, sh, gate, wu_bf16, wd_bf16, gfinal.reshape(1, d))


def _rwkv_prep_body(za_ref, shift_ref, mu_ref, w0_ref, wup_ref, a0_ref, aup_ref, gup_ref,
                    r_ref, k_ref, v_ref, lw_ref, a_ref, g_ref, carry_ref, *, d_a):
    @pl.when(pl.program_id(1) == 0)
    def _():
        carry_ref[...] = shift_ref[...]

    za = za_ref[...]
    tm = za.shape[0]
    row = lax.broadcasted_iota(jnp.int32, za.shape, 0)
    prev = jnp.where(row == 0, carry_ref[...], pltpu.roll(za, 1, axis=0))
    carry_ref[...] = za[tm - 1:tm, :]
    zs = za + mu_ref[...] * (prev - za)
    o_w = 3 * d_a
    o_a = o_w + LORA_W
    o_g = o_a + LORA_A
    r_ref[...] = zs[:, 0:d_a]
    k_ref[...] = zs[:, d_a:2 * d_a]
    v_ref[...] = zs[:, 2 * d_a:3 * d_a]
    dw = zs[:, o_w:o_a]
    da = zs[:, o_a:o_g]
    dg = zs[:, o_g:o_g + LORA_G]
    lw_ref[...] = -DECAY_SCALE * _sigmoid(w0_ref[...] + _bdot(jnp.tanh(dw), wup_ref[...]))
    a_ref[...] = _sigmoid(a0_ref[...] + _bdot(da, aup_ref[...]))
    g_ref[...] = _bdot(_sigmoid(dg), gup_ref[...])


def _rwkv_prep(za, shift_prev, mu, w0, w_up, a0, a_up, g_up, tm):
    b, t, dp = za.shape
    d_a = w0.shape[0]
    out = jax.ShapeDtypeStruct((b, t, d_a), F32)
    row = lambda n: pl.BlockSpec((1, n), lambda i, j: (0, 0))
    full = lambda w: pl.BlockSpec(w.shape, lambda i, j: (0, 0))
    tile = pl.BlockSpec((None, tm, d_a), lambda i, j: (i, j, 0))
    return pl.pallas_call(
        functools.partial(_rwkv_prep_body, d_a=d_a),
        grid=(b, t // tm),
        in_specs=[pl.BlockSpec((None, tm, dp), lambda i, j: (i, j, 0)),
                  pl.BlockSpec((None, 1, dp), lambda i, j: (i, 0, 0)),
                  row(dp), row(d_a), full(w_up), row(d_a), full(a_up), full(g_up)],
        out_specs=(tile,) * 6,
        out_shape=(out,) * 6,
        scratch_shapes=[pltpu.VMEM((1, dp), F32)],
        compiler_params=_cparams(("arbitrary", "arbitrary")),
        name="rwkv_prep",
    )(za, shift_prev.reshape(b, 1, dp), mu.reshape(1, dp), w0.reshape(1, d_a), w_up,
      a0.reshape(1, d_a), a_up, g_up)


def _rwkv_heads(r, k, v, lw, a, g, kkw, kaw, rkw, lnw, lnb, s0, valid, tri_incl, tri_strict,
                eye, n_doubling):
    hs = range(len(r))
    c = r[0].shape[0]
    each = lambda f, *ls: [f(*xs) for xs in zip(*ls)]

    kk = each(lambda k_, w: k_ * w, k, kkw)
    kk = each(lambda x: x / jnp.maximum(jnp.sqrt(jnp.sum(x * x, axis=-1, keepdims=True)), 1e-12), kk)
    k2 = each(lambda k_, a_, w: k_ * (1.0 + (a_ - 1.0) * w), k, a, kaw)
    b = each(lambda x, a_: x * a_, kk, a)
    if valid is not None:
        zero = lambda x: jnp.where(valid, x, 0.0)
        lw, kk, k2, b, v, r = (each(zero, x) for x in (lw, kk, k2, b, v, r))
    cum = each(lambda x: jnp.dot(tri_incl, x, precision=HIGHEST, preferred_element_type=F32), lw)
    tot = each(lambda x: x[c - 1:c, :], cum)
    e_in = each(jnp.exp, cum)
    e_out = each(lambda x: jnp.exp(-x), cum)
    e_end = each(lambda t_, x: jnp.exp(t_ - x), tot, cum)
    kap_t = each(lambda x, cu, l_: x * jnp.exp(cu - l_), kk, cum, lw)
    r_t = each(lambda x, e: x * e, r, e_in)
    b_t = each(lambda x, e: x * e, b, e_out)
    k_t = each(lambda x, e: x * e, k2, e_out)
    b_h = each(lambda x, e: x * e, b, e_end)
    k_h = each(lambda x, e: x * e, k2, e_end)

    rhs = each(lambda x, y: jnp.concatenate([x, y], axis=0), b_t, k_t)
    gk = each(_bdot_nt, kap_t, rhs)
    gr = each(_bdot_nt, r_t, rhs)
    n1 = each(lambda x: jnp.where(tri_strict, x[:, :c], 0.0), gk)
    n2 = each(lambda x: jnp.where(tri_strict, x[:, c:], 0.0), gk)
    m1 = each(lambda x: jnp.where(tri_incl > 0, x[:, :c], 0.0), gr)
    m2 = each(lambda x: jnp.where(tri_incl > 0, x[:, c:], 0.0), gr)

    x = each(lambda n: -n, n1)
    tinv = each(lambda x_: eye + x_, x)
    for _ in range(n_doubling):
        x = each(_bdot, x, x)
        tinv = each(lambda t_, x_: t_ + _bdot(t_, x_), tinv, x)

    nv = each(lambda n, m_, v_: _bdot(jnp.concatenate([n, m_], axis=0), v_), n2, m2, v)
    ty = each(lambda t_, kp, nv_: _bdot(t_, jnp.concatenate([kp, nv_[:c]], axis=1)),
              tinv, kap_t, nv)
    m1ty = each(_bdot, m1, ty)
    rq = each(lambda r_, m_: r_ - m_[:, :HEAD_A], r_t, m1ty)
    ol = each(lambda nv_, m_: nv_[c:] - m_[:, HEAD_A:], nv, m1ty)
    tb = each(_bdot_tn, ty, b_h)
    vk = each(_bdot_tn, v, k_h)

    o = each(lambda rq_, s_, ol_: _bdot_nt(rq_, s_) + ol_, rq, s0, ol)
    s_new = each(lambda s_, t_, tb_, vk_: s_ * jnp.exp(t_) - _bdot(s_, tb_[:HEAD_A])
                 + (vk_ - tb_[HEAD_A:]), s0, tot, tb, vk)

    def finish(o_, r_, k2_, v_, g_, rkw_, lnw_, lnb_):
        mu = jnp.mean(o_, axis=-1, keepdims=True)
        var = jnp.mean(jnp.square(o_ - mu), axis=-1, keepdims=True)
        on = (o_ - mu) * lax.rsqrt(var + GN_EPS) * lnw_ + lnb_
        bonus = jnp.sum(r_ * k2_ * rkw_, axis=-1, keepdims=True) * v_
        return (on + bonus) * g_

    y = each(finish, o, r, k2, v, g, rkw, lnw, lnb)
    return y, s_new


def _rwkv_chunk_body(r_ref, k_ref, v_ref, lw_ref, a_ref, g_ref, kkw_ref, kaw_ref, rkw_ref,
                     lnw_ref, lnb_ref, s0_ref, y_ref, sout_ref, s_ref, *, chunk, n_chunks,
                     n_heads, n_valid):
    @pl.when(pl.program_id(1) == 0)
    def _():
        s_ref[...] = s0_ref[...]

    c = chunk
    ri = lax.broadcasted_iota(jnp.int32, (c, c), 0)
    ci = lax.broadcasted_iota(jnp.int32, (c, c), 1)
    tri_incl = (ri >= ci).astype(F32)
    tri_strict = ri > ci
    eye = (ri == ci).astype(F32)
    valid = None
    if n_valid < c:
        valid = lax.broadcasted_iota(jnp.int32, (c, HEAD_A), 0) < n_valid
    n_doubling = max(int(math.log2(c)) - 1, 0)

    def one_chunk(ic, carry):
        t0 = pl.multiple_of(ic * c, c)
        rows = pl.ds(t0, c)
        sls = [slice(h * HEAD_A, (h + 1) * HEAD_A) for h in range(n_heads)]
        per_head = lambda ref: [ref[rows, sl] for sl in sls]
        per_head_row = lambda ref: [ref[:, sl] for sl in sls]
        y, s_new = _rwkv_heads(
            per_head(r_ref), per_head(k_ref), per_head(v_ref), per_head(lw_ref),
            per_head(a_ref), per_head(g_ref), per_head_row(kkw_ref), per_head_row(kaw_ref),
            per_head_row(rkw_ref), per_head_row(lnw_ref), per_head_row(lnb_ref),
            [s_ref[h] for h in range(n_heads)], valid, tri_incl, tri_strict, eye, n_doubling)
        for h, sl in enumerate(sls):
            y_ref[rows, sl] = y[h]
            s_ref[h] = s_new[h]
        return carry

    lax.fori_loop(0, n_chunks, one_chunk, 0)

    @pl.when(pl.program_id(1) == pl.num_programs(1) - 1)
    def _():
        sout_ref[...] = s_ref[...]


def _rwkv_chunked(r, k, v, lw, a, g, k_k, k_a, r_k, lnx_w, lnx_b, s0, tb, chunk, n_valid):
    b, t, d_a = r.shape
    n_heads = d_a // HEAD_A
    tile = pl.BlockSpec((None, tb, d_a), lambda i, j: (i, j, 0))
    row = pl.BlockSpec((1, d_a), lambda i, j: (0, 0))
    st = pl.BlockSpec((None, n_heads, HEAD_A, HEAD_A), lambda i, j: (i, 0, 0, 0))
    return pl.pallas_call(
        functools.partial(_rwkv_chunk_body, chunk=chunk, n_chunks=tb // chunk,
                          n_heads=n_heads, n_valid=n_valid),
        grid=(b, t // tb),
        in_specs=[tile] * 6 + [row] * 5 + [st],
        out_specs=(tile, st),
        out_shape=(jax.ShapeDtypeStruct((b, t, d_a), F32),
                   jax.ShapeDtypeStruct((b, n_heads, HEAD_A, HEAD_A), F32)),
        scratch_shapes=[pltpu.VMEM((n_heads, HEAD_A, HEAD_A), F32)],
        compiler_params=_cparams(("arbitrary", "arbitrary")),
        name="rwkv_chunked",
    )(r, k, v, lw, a, g, k_k.reshape(1, d_a), k_a.reshape(1, d_a), r_k.reshape(1, d_a),
      lnx_w.reshape(1, d_a), lnx_b.reshape(1, d_a), s0)


def _s5_disc_body(lr_ref, li_ref, ldt_ref, bre_ref, bim_ref, ar_ref, ai_ref, bbre_ref, bbim_ref):
    lr, li = lr_ref[...], li_ref[...]
    dt = jnp.exp(ldt_ref[...])
    mag = jnp.exp(lr * dt)
    ar = mag * jnp.cos(li * dt)
    ai = mag * jnp.sin(li * dt)
    den = lr * lr + li * li
    fr = ((ar - 1.0) * lr + ai * li) / den
    fi = (ai * lr - (ar - 1.0) * li) / den
    ar_ref[...] = ar
    ai_ref[...] = ai
    b_re, b_im = bre_ref[...], bim_ref[...]
    bbre_ref[...] = fr[:, None, :] * b_re - fi[:, None, :] * b_im
    bbim_ref[...] = fr[:, None, :] * b_im + fi[:, None, :] * b_re


def _s5_discretise(lam_re, lam_im, log_dt, b_re, b_im):
    g, p = lam_re.shape
    n = b_re.shape[2]
    bt = lambda x: jnp.swapaxes(x, 1, 2)
    return pl.pallas_call(
        _s5_disc_body,
        out_shape=(jax.ShapeDtypeStruct((g, p), F32), jax.ShapeDtypeStruct((g, p), F32),
                   jax.ShapeDtypeStruct((g, n, p), F32), jax.ShapeDtypeStruct((g, n, p), F32)),
        name="s5_discretise",
    )(lam_re, lam_im, log_dt.reshape(g, 1), bt(b_re), bt(b_im))


def _gelu_tanh(x):
    c = math.sqrt(2.0 / math.pi)
    return 0.5 * x * (1.0 + jnp.tanh(c * (x + 0.044715 * (x * x * x))))


def _s5_body(u_ref, h0r_ref, h0i_ref, ar_ref, ai_ref, bdre_ref, bdim_ref, cdre_ref, cdim_ref,
             d_ref, wglu_ref, bglu_ref, y_ref, hre_ref, him_ref,
             xr_ref, xi_ref, apr_ref, api_ref, mr_ref, mi_ref, cr_ref, ci_ref, *, last_row):
    j = pl.program_id(1)
    rows, width = xr_ref.shape
    half_u = u_ref.shape[1] // 2
    half_x = width // 2

    ar, ai = ar_ref[...], ai_ref[...]
    a2r, a2i = ar * ar - ai * ai, 2.0 * ar * ai
    a4r, a4i = a2r * a2r - a2i * a2i, 2.0 * a2r * a2i

    @pl.when(j == 0)
    def _():
        cr_ref[...] = h0r_ref[...]
        ci_ref[...] = h0i_ref[...]
        pr, pi = ar, ai
        for s in range(SUBLANES):
            apr_ref[s:s + 1, :] = pr
            api_ref[s:s + 1, :] = pi
            pr, pi = pr * ar - pi * ai, pr * ai + pi * ar
        sub8 = lax.broadcasted_iota(jnp.int32, (SUBLANES, width), 0)
        for rd, (d, pr, pi) in enumerate(((1, ar, ai), (2, a2r, a2i), (4, a4r, a4i))):
            mr_ref[rd] = jnp.where(sub8 >= d, pr, 0.0)
            mi_ref[rd] = jnp.where(sub8 >= d, pi, 0.0)

    ub = u_ref[...].astype(BF16)
    for hf in range(2):
        us = ub[:, hf * half_u:(hf + 1) * half_u]
        cols = slice(hf * half_x, (hf + 1) * half_x)
        xr_ref[:, cols] = jnp.dot(us, bdre_ref[hf], preferred_element_type=F32)
        xi_ref[:, cols] = jnp.dot(us, bdim_ref[hf], preferred_element_type=F32)

    lane_chunk = 4 * LANES
    grouped = lambda x: x.reshape(rows // SUBLANES, SUBLANES, lane_chunk)
    for c0 in range(0, width, lane_chunk):
        cols = slice(c0, c0 + lane_chunk)
        xr, xi = xr_ref[:, cols], xi_ref[:, cols]
        for rd, d in enumerate((1, 2, 4)):
            pr, pi = mr_ref[rd][:, cols][None], mi_ref[rd][:, cols][None]
            sr = grouped(pltpu.roll(xr, d, axis=0))
            si = grouped(pltpu.roll(xi, d, axis=0))
            xr, xi = ((grouped(xr) + (pr * sr - pi * si)).reshape(rows, lane_chunk),
                      (grouped(xi) + (pr * si + pi * sr)).reshape(rows, lane_chunk))
        xr_ref[:, cols] = xr
        xi_ref[:, cols] = xi

    def group(n, carry):
        cr, ci = carry
        r0 = pl.multiple_of(n * SUBLANES, SUBLANES)
        rs = pl.ds(r0, SUBLANES)
        apr, api = apr_ref[...], api_ref[...]
        hr = xr_ref[rs, :] + apr * cr - api * ci
        hi = xi_ref[rs, :] + apr * ci + api * cr
        xr_ref[rs, :] = hr
        xi_ref[rs, :] = hi
        return hr[SUBLANES - 1:SUBLANES, :], hi[SUBLANES - 1:SUBLANES, :]

    cr, ci = lax.fori_loop(0, rows // SUBLANES, group, (cr_ref[...], ci_ref[...]))
    cr_ref[...] = cr
    ci_ref[...] = ci

    @pl.when(j == pl.num_programs(1) - 1)
    def _():
        hre_ref[...] = xr_ref[last_row:last_row + 1, :]
        him_ref[...] = xi_ref[last_row:last_row + 1, :]

    u = u_ref[...]
    for hf in range(2):
        cols = slice(hf * half_x, (hf + 1) * half_x)
        ucols = slice(hf * half_u, (hf + 1) * half_u)
        y = (jnp.dot(xr_ref[:, cols].astype(BF16), cdre_ref[hf], preferred_element_type=F32)
             - jnp.dot(xi_ref[:, cols].astype(BF16), cdim_ref[hf], preferred_element_type=F32)
             + d_ref[:, ucols] * u[:, ucols])
        y_ref[:, ucols] = _gelu_tanh(y)
    y = y_ref[...]
    z = jnp.dot(y.astype(BF16), wglu_ref[...], preferred_element_type=F32) + bglu_ref[...]
    y_ref[...] = y * _sigmoid(z)


def _block_diag_halves(w_gab, transpose):
    g = w_gab.shape[0]
    hg = g // 2
    eye = jnp.eye(hg, dtype=w_gab.dtype)
    halves = []
    for hf in range(2):
        w = w_gab[hf * hg:(hf + 1) * hg]
        if transpose:
            w = jnp.swapaxes(w, 1, 2)
        a, b = w.shape[1], w.shape[2]
        halves.append(jnp.einsum('gab,gh->gahb', w, eye).reshape(hg * a, hg * b))
    return jnp.stack(halves)


def _s5_mix(u, h0_re, h0_im, ar, ai, bbt_re, bbt_im, c_re, c_im, d, w_glu, b_glu, rows, n_valid):
    b, t, d_b = u.shape
    g, p = ar.shape
    width = g * p
    bd_re = _block_diag_halves(bbt_re, False).astype(BF16)
    bd_im = _block_diag_halves(bbt_im, False).astype(BF16)
    cd_re = _block_diag_halves(c_re, True).astype(BF16)
    cd_im = _block_diag_halves(c_im, True).astype(BF16)
    n_blocks = t // rows
    last_row = (n_valid - 1) - (n_blocks - 1) * rows
    full = lambda x: pl.BlockSpec(x.shape, lambda i, j: (0,) * x.ndim)
    st = pl.BlockSpec((None, 1, width), lambda i, j: (i, 0, 0))
    args = (u, h0_re.reshape(b, 1, width), h0_im.reshape(b, 1, width),
            ar.reshape(1, width), ai.reshape(1, width), bd_re, bd_im, cd_re, cd_im,
            d.reshape(1, d_b), w_glu.astype(BF16), b_glu.reshape(1, d_b))
    return pl.pallas_call(
        functools.partial(_s5_body, last_row=last_row),
        grid=(b, n_blocks),
        in_specs=[pl.BlockSpec((None, rows, d_b), lambda i, j: (i, j, 0)), st, st]
                 + [full(x) for x in args[3:]],
        out_specs=(pl.BlockSpec((None, rows, d_b), lambda i, j: (i, j, 0)), st, st),
        out_shape=(jax.ShapeDtypeStruct((b, t, d_b), F32),
                   jax.ShapeDtypeStruct((b, 1, width), F32),
                   jax.ShapeDtypeStruct((b, 1, width), F32)),
        scratch_shapes=[pltpu.VMEM((rows, width), F32), pltpu.VMEM((rows, width), F32),
                        pltpu.VMEM((SUBLANES, width), F32), pltpu.VMEM((SUBLANES, width), F32),
                        pltpu.VMEM((3, SUBLANES, width), F32), pltpu.VMEM((3, SUBLANES, width), F32),
                        pltpu.VMEM((1, width), F32), pltpu.VMEM((1, width), F32)],
        compiler_params=_cparams(("arbitrary", "arbitrary")),
        name="s5_mix",
    )(*args)


def _lambda_full(lq1, lk1, lq2, lk2):
    s1 = jnp.sum(lq1 * lk1, axis=-1, keepdims=True)
    s2 = jnp.sum(lq2 * lk2, axis=-1, keepdims=True)
    return jnp.exp(s1) - jnp.exp(s2) + LAMBDA_INIT


def _sub_ln(o, subln):
    ms = jnp.mean(o * o, axis=-1, keepdims=True)
    return o * lax.rsqrt(ms + RMS_EPS) * subln * (1.0 - LAMBDA_INIT)


def _attn_prompt_body(q_ref, k_ref, v_ref, lq1_ref, lk1_ref, lq2_ref, lk2_ref, subln_ref,
                      o_ref, kb_ref, vt_ref, m_ref, l_ref, acc_ref, *, tile, n_tiles):
    qi = pl.program_id(2)
    halves = range(2)

    @pl.when(qi == 0)
    def _():
        for c in range(n_tiles):
            rows = slice(c * tile, (c + 1) * tile)
            kb_ref[rows, :] = k_ref[rows, :].astype(BF16)
            vt_ref[c] = v_ref[rows, :].T.astype(BF16)

    q = q_ref[...] * (SCALE_C * math.log2(math.e))
    lane = lax.broadcasted_iota(jnp.int32, q.shape, 1)
    qs = [jnp.where((lane >= HALF_C) == bool(c), q, 0.0).astype(BF16) for c in halves]
    m_ref[...] = jnp.full_like(m_ref, NEG_INF)
    l_ref[...] = jnp.zeros_like(l_ref)
    acc_ref[...] = jnp.zeros_like(acc_ref)

    def kv_step(kj, diagonal):
        k0 = pl.multiple_of(kj * tile, tile)
        kb = kb_ref[pl.ds(k0, tile), :]
        st = [lax.dot_general(kb, qs[c], (((1,), (1,)), ((), ())),
                              preferred_element_type=F32) for c in halves]
        if diagonal:
            krow = lax.broadcasted_iota(jnp.int32, st[0].shape, 0)
            qcol = lax.broadcasted_iota(jnp.int32, st[0].shape, 1)
            st = [jnp.where(krow <= qcol, x, NEG_INF) for x in st]
        m_prev = [m_ref[c] for c in halves]
        m_new = [jnp.maximum(m_prev[c], jnp.max(st[c], axis=0, keepdims=True)) for c in halves]
        alpha = [jnp.exp2(m_prev[c] - m_new[c]) for c in halves]
        p = [jnp.exp2(st[c] - m_new[c]) for c in halves]
        vt = vt_ref[kj]
        pv = [jnp.dot(vt, p[c].astype(BF16), preferred_element_type=F32) for c in halves]
        for c in halves:
            l_ref[c] = alpha[c] * l_ref[c] + jnp.sum(p[c], axis=0, keepdims=True)
            acc_ref[c] = alpha[c] * acc_ref[c] + pv[c]
            m_ref[c] = m_new[c]

    def full_step(kj, carry):
        kv_step(kj, False)
        return carry

    lax.fori_loop(0, qi, full_step, 0)
    kv_step(qi, True)

    lam = _lambda_full(lq1_ref[...], lk1_ref[...], lq2_ref[...], lk2_ref[...])
    ot = acc_ref[0] / l_ref[0] - lam * (acc_ref[1] / l_ref[1])
    ms = jnp.mean(ot * ot, axis=0, keepdims=True)
    ot = ot * lax.rsqrt(ms + RMS_EPS) * subln_ref[...] * (1.0 - LAMBDA_INIT)
    o_ref[...] = ot.T


def _attn_prompt(q, k, v, lq1, lk1, lq2, lk2, subln, tile):
    b, t, d = q.shape
    n_heads = d // HEAD_C
    n_tiles = t // tile
    small = lambda n: pl.BlockSpec((1, n), lambda i, h, j: (0, 0))
    return pl.pallas_call(
        functools.partial(_attn_prompt_body, tile=tile, n_tiles=n_tiles),
        grid=(b, n_heads, n_tiles),
        in_specs=[pl.BlockSpec((None, tile, HEAD_C), lambda i, h, j: (i, j, h)),
                  pl.BlockSpec((None, t, HEAD_C), lambda i, h, j: (i, 0, h)),
                  pl.BlockSpec((None, t, HEAD_C), lambda i, h, j: (i, 0, h)),
                  small(HALF_C), small(HALF_C), small(HALF_C), small(HALF_C),
                  pl.BlockSpec((HEAD_C, 1), lambda i, h, j: (0, 0))],
        out_specs=pl.BlockSpec((None, tile, HEAD_C), lambda i, h, j: (i, j, h)),
        out_shape=jax.ShapeDtypeStruct((b, t, d), F32),
        scratch_shapes=[pltpu.VMEM((t, HEAD_C), BF16), pltpu.VMEM((n_tiles, HEAD_C, tile), BF16),
                        pltpu.VMEM((2, 1, tile), F32), pltpu.VMEM((2, 1, tile), F32),
                        pltpu.VMEM((2, HEAD_C, tile), F32)],
        compiler_params=_cparams(("arbitrary", "arbitrary", "arbitrary")),
        name="attn_prompt",
    )(q, k, v, lq1.reshape(1, -1), lk1.reshape(1, -1), lq2.reshape(1, -1), lk2.reshape(1, -1),
      subln.reshape(-1, 1))


def _attn_sample_body(pt_ref, qrep_ref, knew_ref, vnew_ref, *rest, pages_per_step, n_heads,
                      t_new, page):
    kp_refs = rest[:pages_per_step]
    vp_refs = rest[pages_per_step:2 * pages_per_step]
    lq1_ref, lk1_ref, lq2_ref, lk2_ref, subln_ref = rest[2 * pages_per_step:2 * pages_per_step + 5]
    o_ref = rest[2 * pages_per_step + 5]
    qw_ref, m_ref, l_ref, acc_ref = rest[2 * pages_per_step + 6:]
    j = pl.program_id(1)
    pairs = range(n_heads // 2)
    q_rows = qrep_ref.shape[1]

    @pl.when(j == 0)
    def _():
        qr = qrep_ref[...]
        row = lax.broadcasted_iota(jnp.int32, qr.shape, 1)
        lane = lax.broadcasted_iota(jnp.int32, qr.shape, 2)
        qm = jnp.where(lane // HALF_C == row // t_new, qr * SCALE_C, 0.0)
        zero = jnp.zeros((q_rows, HEAD_C), F32)
        for pr in pairs:
            qw_ref[pr] = jnp.concatenate(
                [jnp.concatenate([qm[2 * pr], zero], axis=1),
                 jnp.concatenate([zero, qm[2 * pr + 1]], axis=1)], axis=0).astype(BF16)
        m_ref[...] = jnp.full_like(m_ref, NEG_INF)
        l_ref[...] = jnp.zeros_like(l_ref)
        acc_ref[...] = jnp.zeros_like(acc_ref)

    def pair_rows(ref, pr):
        return jnp.concatenate([ref[pl.ds(2 * pr + i, page, stride=n_heads), :] for i in range(2)],
                               axis=1).astype(BF16)

    def update(s, values):
        m_prev = [m_ref[pr] for pr in pairs]
        m_new = [jnp.maximum(m_prev[pr], jnp.max(s[pr], axis=-1, keepdims=True)) for pr in pairs]
        alpha = [jnp.exp(m_prev[pr] - m_new[pr]) for pr in pairs]
        p = [jnp.exp(s[pr] - m_new[pr]) for pr in pairs]
        pv = []
        for pr in pairs:
            w = p[pr].shape[1] // len(values[pr])
            acc = None
            for i, vb in enumerate(values[pr]):
                d = jnp.dot(p[pr][:, i * w:(i + 1) * w].astype(BF16), vb,
                            preferred_element_type=F32)
                acc = d if acc is None else acc + d
            pv.append(acc)
        for pr in pairs:
            l_ref[pr] = alpha[pr] * l_ref[pr] + jnp.sum(p[pr], axis=-1, keepdims=True)
            acc_ref[pr] = alpha[pr] * acc_ref[pr] + pv[pr]
            m_ref[pr] = m_new[pr]

    s = [jnp.concatenate(
            [lax.dot_general(qw_ref[pr], pair_rows(kp, pr), (((1,), (1,)), ((), ())),
                             preferred_element_type=F32) for kp in kp_refs], axis=1)
         for pr in pairs]
    update(s, [[pair_rows(vp, pr) for vp in vp_refs] for pr in pairs])

    @pl.when(j == pl.num_programs(1) - 1)
    def _():
        both = lambda ref, pr: jnp.concatenate([ref[2 * pr], ref[2 * pr + 1]], axis=1).astype(BF16)
        sn = []
        for pr in pairs:
            x = lax.dot_general(qw_ref[pr], both(knew_ref, pr), (((1,), (1,)), ((), ())),
                                preferred_element_type=F32)
            trow = lax.broadcasted_iota(jnp.int32, x.shape, 0) % t_new
            tcol = lax.broadcasted_iota(jnp.int32, x.shape, 1)
            sn.append(jnp.where(tcol <= trow, x, NEG_INF))
        update(sn, [[both(vnew_ref, pr)] for pr in pairs])
        lam = _lambda_full(lq1_ref[...], lk1_ref[...], lq2_ref[...], lk2_ref[...])
        for pr in pairs:
            on = acc_ref[pr] / l_ref[pr]
            for i in range(2):
                h = 2 * pr + i
                r0 = i * q_rows
                oh = on[r0:r0 + 2 * t_new, i * HEAD_C:(i + 1) * HEAD_C]
                o = oh[0:t_new] - lam * oh[t_new:2 * t_new]
                o_ref[:, h * HEAD_C:(h + 1) * HEAD_C] = _sub_ln(o, subln_ref[...])


def _attn_sample(q, k_new, v_new, cache_k, cache_v, page_table, lq1, lk1, lq2, lk2, subln,
                 pages_per_step):
    db, t_new, d = q.shape
    n_heads = d // HEAD_C
    n_pool, page = cache_k.shape[0], cache_k.shape[1]
    n_pages = page_table.shape[1]
    ck = cache_k.reshape(n_pool, page * n_heads, HEAD_C)
    cv = cache_v.reshape(n_pool, page * n_heads, HEAD_C)
    by_head = lambda x: jnp.swapaxes(x.reshape(db, t_new, n_heads, HEAD_C), 1, 2)
    q_rows = NEW_PAD
    qrep = jnp.tile(by_head(q), (1, 1, q_rows // t_new, 1))
    pad = ((0, 0), (0, 0), (0, NEW_PAD - t_new), (0, 0))
    knew = jnp.pad(by_head(k_new), pad)
    vnew = jnp.pad(by_head(v_new), pad)
    pt = page_table.reshape(-1)

    def page_spec(i):
        return pl.BlockSpec((None, page * n_heads, HEAD_C),
                            lambda b, j, pt_ref: (pt_ref[b * n_pages + j * pages_per_step + i], 0, 0))

    small = lambda n: pl.BlockSpec((1, n), lambda b, j, pt_ref: (0, 0))
    per_b = lambda r: pl.BlockSpec((None, n_heads, r, HEAD_C), lambda b, j, pt_ref: (b, 0, 0, 0))
    grid_spec = pltpu.PrefetchScalarGridSpec(
        num_scalar_prefetch=1,
        grid=(db, n_pages // pages_per_step),
        in_specs=[per_b(q_rows), per_b(NEW_PAD), per_b(NEW_PAD)]
                 + [page_spec(i) for i in range(pages_per_step)] * 2
                 + [small(HALF_C)] * 4 + [small(HEAD_C)],
        out_specs=pl.BlockSpec((None, t_new, d), lambda b, j, pt_ref: (b, 0, 0)),
        scratch_shapes=[pltpu.VMEM((n_heads // 2, 2 * q_rows, 2 * HEAD_C), BF16),
                        pltpu.VMEM((n_heads // 2, 2 * q_rows, 1), F32),
                        pltpu.VMEM((n_heads // 2, 2 * q_rows, 1), F32),
                        pltpu.VMEM((n_heads // 2, 2 * q_rows, 2 * HEAD_C), F32)])
    return pl.pallas_call(
        functools.partial(_attn_sample_body, pages_per_step=pages_per_step, n_heads=n_heads,
                          t_new=t_new, page=page),
        grid_spec=grid_spec,
        out_shape=jax.ShapeDtypeStruct((db, t_new, d), F32),
        compiler_params=_cparams(("arbitrary", "arbitrary")),
        name="attn_sample",
    )(pt, qrep, knew, vnew, *([ck] * pages_per_step), *([cv] * pages_per_step),
      lq1.reshape(1, -1), lk1.reshape(1, -1), lq2.reshape(1, -1), lk2.reshape(1, -1),
      subln.reshape(1, -1))


def _pad_time(x, t_pad):
    return jnp.pad(x, ((0, 0), (0, t_pad - x.shape[1]), (0, 0)))


def _trunk(x, mods, shift0, wkv0, sre0, sim0, attend, W, per_row_mod):
    b, t, d = x.shape
    m = b * t
    rows_mod = m if per_row_mod else t
    tm = min(512, rows_mod)
    tm_mlp = min(1024, rows_mod)
    d_a = W['rwkv_w0'].shape[0]
    d_ap = W['rwkv_mu'].shape[0]

    def mod_arg(v):
        if per_row_mod:
            return jnp.repeat(v, t, axis=0)
        return v.reshape(b, 1, d)

    x2 = x.reshape(m, d)

    sh_m, sc_m, g_m, sh_f, sc_f, g_f = (mod_arg(v) for v in mods[0])
    za2, u2 = _normmod_mm(x2, W['norm_mix'][0], sc_m, sh_m, W['w_in_ab_bf16'],
                          ((0, d_ap), (d_ap, W['w_in_ab_bf16'].shape[1])), tm, t)
    za = za2.reshape(b, t, d_ap)
    u = u2.reshape(b, t, -1)
    shift1 = za[:, -1]

    chunk = RWKV_CHUNK
    t_pad = -(-t // chunk) * chunk
    if t_pad != t:
        za_p, u_p = _pad_time(za, t_pad), _pad_time(u, t_pad)
    else:
        za_p, u_p = za, u
    tb = min(512, t_pad)
    r, k, v, lw, a, g = _rwkv_prep(za_p, shift0, W['rwkv_mu'], W['rwkv_w0'], W['rwkv_w_up'],
                                   W['rwkv_a0'], W['rwkv_a_up'], W['rwkv_g_up'], tb)
    y_a, wkv1 = _rwkv_chunked(r, k, v, lw, a, g, W['rwkv_k_k'], W['rwkv_k_a'], W['rwkv_r_k'],
                              W['rwkv_lnx_w'], W['rwkv_lnx_b'], wkv0, tb, chunk,
                              chunk if t_pad == t else t)
    s5_rows = min(256, t_pad)
    y_b, sre1, sim1 = _s5_mix(u_p, sre0, sim0, W['s5_ar'], W['s5_ai'], W['s5_bbt_re'],
                              W['s5_bbt_im'], W['s5_c_re'], W['s5_c_im'], W['s5_d'],
                              W['s5_w_glu'], W['s5_b_glu'], s5_rows, t)
    y_a2 = y_a[:, :t].reshape(m, d_a)
    y_b2 = y_b[:, :t].reshape(m, -1)
    wo = W['w_out_bf16'][0]
    x2 = _proj_res(x2, g_m, (y_a2, y_b2), (wo[:d_a], wo[d_a:]), tm, t)
    x2 = _mlp(x2, W['norm_mlp'][0], sc_f, sh_f, g_f, W['w_up_bf16'][0], W['w_down_bf16'][0],
              W['norm_f'], tm_mlp, 1024, t, False)

    sh_m, sc_m, g_m, sh_f, sc_f, g_f = (mod_arg(v) for v in mods[1])
    q2, k2, v2 = _normmod_mm(x2, W['norm_mix'][1], sc_m, sh_m, W['diff_w_qkv_bf16'],
                             ((0, d), (d, 2 * d), (2 * d, 3 * d)), tm, t)
    q3, k3, v3 = (z.reshape(b, t, d) for z in (q2, k2, v2))
    o = attend(q3, k3, v3)
    x2 = _proj_res(x2, g_m, (o.reshape(m, d),), (W['w_out_bf16'][1],), tm, t)
    y2 = _mlp(x2, W['norm_mlp'][1], sc_f, sh_f, g_f, W['w_up_bf16'][1], W['w_down_bf16'][1],
              W['norm_f'], tm_mlp, 1024, t, True)

    n_heads_c = d // HEAD_C
    g_b, p_b = W['s5_ar'].shape
    return (y2.reshape(b, t, d), shift1, wkv1, sre1.reshape(b, g_b, p_b), sim1.reshape(b, g_b, p_b),
            k3.reshape(b, t, n_heads_c, HEAD_C), v3.reshape(b, t, n_heads_c, HEAD_C))


def kernel(x_prompt, x_sample, state_shift, state_wkv, state_ssm_re, state_ssm_im, cache_k, cache_v, page_table, c_prompt, c_sample, norm_mix, norm_mlp, norm_f, w_ada, b_ada, w_out, w_up, w_down, w_in_ab, rwkv_mu, rwkv_w0, rwkv_w_up, rwkv_a0, rwkv_a_up, rwkv_g_up, rwkv_k_k, rwkv_k_a, rwkv_r_k, rwkv_lnx_w, rwkv_lnx_b, s5_lam_re, s5_lam_im, s5_log_dt, s5_b_re, s5_b_im, s5_c_re, s5_c_im, s5_d, s5_w_glu, s5_b_glu, diff_w_qkv, diff_lq1, diff_lk1, diff_lq2, diff_lk2, diff_subln):
    bp, tp, d = x_prompt.shape
    db, ts, _ = x_sample.shape
    depth = w_ada.shape[0]
    n_heads_a = rwkv_r_k.shape[0]
    g_b, p_b = s5_lam_re.shape

    ar, ai, bbt_re, bbt_im = _s5_discretise(s5_lam_re, s5_lam_im, s5_log_dt, s5_b_re, s5_b_im)
    W = dict(norm_mix=norm_mix, norm_mlp=norm_mlp, norm_f=norm_f,
             w_out_bf16=w_out.astype(BF16), w_up_bf16=w_up.astype(BF16),
             w_down_bf16=w_down.astype(BF16), w_in_ab_bf16=w_in_ab.astype(BF16),
             diff_w_qkv_bf16=diff_w_qkv.astype(BF16),
             rwkv_mu=rwkv_mu, rwkv_w0=rwkv_w0, rwkv_w_up=rwkv_w_up, rwkv_a0=rwkv_a0,
             rwkv_a_up=rwkv_a_up, rwkv_g_up=rwkv_g_up, rwkv_k_k=rwkv_k_k, rwkv_k_a=rwkv_k_a,
             rwkv_r_k=rwkv_r_k.reshape(-1), rwkv_lnx_w=rwkv_lnx_w, rwkv_lnx_b=rwkv_lnx_b,
             s5_ar=ar, s5_ai=ai, s5_bbt_re=bbt_re, s5_bbt_im=bbt_im,
             s5_c_re=s5_c_re, s5_c_im=s5_c_im, s5_d=s5_d.reshape(-1), s5_w_glu=s5_w_glu,
             s5_b_glu=s5_b_glu)

    n_c = bp + db
    rows_c = -(-n_c // SUBLANES) * SUBLANES
    c_all = jnp.pad(jnp.concatenate([c_prompt, c_sample], axis=0), ((0, rows_c - n_c), (0, 0)))
    mod = _ada_mod(c_all, w_ada, b_ada)
    mods_p = [tuple(mod[l, :bp, i * d:(i + 1) * d] for i in range(6)) for l in range(depth)]
    mods_s = [tuple(mod[l, bp:n_c, i * d:(i + 1) * d] for i in range(6)) for l in range(depth)]

    lam_args = (diff_lq1, diff_lk1, diff_lq2, diff_lk2, diff_subln)
    attend_p = lambda q, k, v: _attn_prompt(q, k, v, *lam_args, tile=min(512, tp))
    y_prompt, p_shift, p_wkv, p_re, p_im, p_k, p_v = _trunk(
        x_prompt, mods_p,
        jnp.zeros((bp, rwkv_mu.shape[0]), F32),
        jnp.zeros((bp, n_heads_a, HEAD_A, HEAD_A), F32),
        jnp.zeros((bp, g_b, p_b), F32), jnp.zeros((bp, g_b, p_b), F32),
        attend_p, W, per_row_mod=False)

    attend_s = lambda q, k, v: _attn_sample(q, k, v, cache_k, cache_v, page_table, *lam_args,
                                            pages_per_step=8)
    y_sample, s_shift, s_wkv, s_re, s_im, s_k, s_v = _trunk(
        x_sample, mods_s, state_shift, state_wkv, state_ssm_re, state_ssm_im,
        attend_s, W, per_row_mod=True)

    return (y_prompt, y_sample, p_shift, p_wkv, p_re, p_im, p_k, p_v,
            s_shift, s_wkv, s_re, s_im, s_k, s_v)
```

```python
import functools
import math

import jax
import jax.numpy as jnp
from jax import lax
from jax.experimental import pallas as pl
from jax.experimental.pallas import tpu as pltpu

F32 = jnp.float32
BF16 = jnp.bfloat16
HIGHEST = lax.Precision.HIGHEST

HEAD_A = 64
LORA_W, LORA_A, LORA_G = 64, 64, 128
S5_GROUP = 16
S5_STATE = 64
HEAD_C = 128
HALF_C = HEAD_C // 2
SCALE_C = HALF_C ** -0.5
RMS_EPS = 1e-6
GN_EPS = 64e-5
DECAY_SCALE = math.exp(-0.5)
LAMBDA_INIT = 0.8 - 0.6 * math.exp(-0.3 * 1)
NEG_INF = -1e30

SUBLANES = 8
LANES = 128
MXU_DIM = 256
VMEM_LIMIT = 56 * 1024 * 1024

RWKV_CHUNK = 64
NEW_PAD = 16


def _cparams(sem):
    return pltpu.CompilerParams(dimension_semantics=sem, vmem_limit_bytes=VMEM_LIMIT)


def _bdot(a, b):
    return jnp.dot(a.astype(BF16), b.astype(BF16), preferred_element_type=F32)


def _bdot_nt(a, b):
    return lax.dot_general(a.astype(BF16), b.astype(BF16), (((1,), (1,)), ((), ())),
                           preferred_element_type=F32)


def _bdot_tn(a, b):
    return lax.dot_general(a.astype(BF16), b.astype(BF16), (((0,), (0,)), ((), ())),
                           preferred_element_type=F32)


def _sigmoid(x):
    return 1.0 / (1.0 + jnp.exp(-x))


def _ada_body(c_ref, w_ref, b_ref, o_ref):
    c = c_ref[...]
    cond = c * _sigmoid(c)
    o_ref[...] = jnp.dot(cond, w_ref[...], precision=HIGHEST,
                         preferred_element_type=F32) + b_ref[...]


def _ada_mod(c, w_ada, b_ada):
    depth, d, n = w_ada.shape
    rows = c.shape[0]
    tn = 1536
    return pl.pallas_call(
        _ada_body,
        grid=(depth, n // tn),
        in_specs=[pl.BlockSpec((rows, d), lambda l, j: (0, 0)),
                  pl.BlockSpec((None, d, tn), lambda l, j: (l, 0, j)),
                  pl.BlockSpec((None, 1, tn), lambda l, j: (l, 0, j))],
        out_specs=pl.BlockSpec((None, rows, tn), lambda l, j: (l, 0, j)),
        out_shape=jax.ShapeDtypeStruct((depth, rows, n), F32),
        compiler_params=_cparams(("arbitrary", "arbitrary")),
        name="ada_mod",
    )(c, w_ada, b_ada.reshape(depth, 1, n))


def _mod_spec(mod, tm, rows_per_batch, d):
    if mod.ndim == 3:
        tiles = rows_per_batch // tm
        return pl.BlockSpec((None, 1, d), lambda i, *_: (i // tiles, 0, 0))
    return pl.BlockSpec((tm, d), lambda i, *_: (i, 0))


def _norm_mod(x, g, sc, sh):
    ms = jnp.mean(x * x, axis=-1, keepdims=True)
    h = (x * lax.rsqrt(ms + RMS_EPS)) * g
    return h * (1.0 + sc) + sh


def _normmod_mm_body(x_ref, g_ref, sc_ref, sh_ref, w_ref, *o_refs, splits, col_chunk):
    hb = _norm_mod(x_ref[...], g_ref[...], sc_ref[...], sh_ref[...]).astype(BF16)
    for o_ref, (c0, c1) in zip(o_refs, splits):
        for s in range(c0, c1, col_chunk):
            e = min(s + col_chunk, c1)
            o_ref[:, s - c0:e - c0] = jnp.dot(hb, w_ref[:, s:e], preferred_element_type=F32)


def _normmod_mm(x2, g, sc, sh, w_bf16, splits, tm, rows_per_batch):
    m, d = x2.shape
    n = w_bf16.shape[1]
    outs = tuple(jax.ShapeDtypeStruct((m, c1 - c0), F32) for c0, c1 in splits)
    return pl.pallas_call(
        functools.partial(_normmod_mm_body, splits=splits, col_chunk=512),
        grid=(m // tm,),
        in_specs=[pl.BlockSpec((tm, d), lambda i: (i, 0)),
                  pl.BlockSpec((1, d), lambda i: (0, 0)),
                  _mod_spec(sc, tm, rows_per_batch, d),
                  _mod_spec(sh, tm, rows_per_batch, d),
                  pl.BlockSpec((d, n), lambda i: (0, 0))],
        out_specs=tuple(pl.BlockSpec((tm, c1 - c0), lambda i: (i, 0)) for c0, c1 in splits),
        out_shape=outs,
        compiler_params=_cparams(("arbitrary",)),
        name="normmod_mm",
    )(x2, g.reshape(1, d), sc, sh, w_bf16)


def _proj_res_body(*refs, n_in):
    x_ref, gate_ref = refs[0], refs[1]
    y_refs = refs[2:2 + n_in]
    w_refs = refs[2 + n_in:2 + 2 * n_in]
    o_ref = refs[2 + 2 * n_in]
    acc = None
    for y_ref, w_ref in zip(y_refs, w_refs):
        p = jnp.dot(y_ref[...].astype(BF16), w_ref[...], preferred_element_type=F32)
        acc = p if acc is None else acc + p
    o_ref[...] = x_ref[...] + gate_ref[...] * acc


def _proj_res(x2, gate, ys, ws_bf16, tm, rows_per_batch):
    m, d = x2.shape
    n_in = len(ys)
    in_specs = [pl.BlockSpec((tm, d), lambda i: (i, 0)), _mod_spec(gate, tm, rows_per_batch, d)]
    in_specs += [pl.BlockSpec((tm, y.shape[1]), lambda i: (i, 0)) for y in ys]
    in_specs += [pl.BlockSpec(w.shape, lambda i: (0, 0)) for w in ws_bf16]
    return pl.pallas_call(
        functools.partial(_proj_res_body, n_in=n_in),
        grid=(m // tm,),
        in_specs=in_specs,
        out_specs=pl.BlockSpec((tm, d), lambda i: (i, 0)),
        out_shape=jax.ShapeDtypeStruct((m, d), F32),
        compiler_params=_cparams(("arbitrary",)),
        name="proj_res",
    )(x2, gate, *ys, *ws_bf16)


def _mlp_body(x_ref, g_ref, sc_ref, sh_ref, gate_ref, wu_ref, wd_ref, gf_ref, o_ref,
              hb_ref, acc_ref, *, final_norm):
    f = pl.program_id(1)

    @pl.when(f == 0)
    def _():
        hb_ref[...] = _norm_mod(x_ref[...], g_ref[...], sc_ref[...], sh_ref[...]).astype(BF16)
        acc_ref[...] = jnp.zeros_like(acc_ref)

    up = jnp.dot(hb_ref[...], wu_ref[...], preferred_element_type=F32)
    act = jnp.square(jnp.maximum(up, 0.0)).astype(BF16)
    acc_ref[...] += jnp.dot(act, wd_ref[...], preferred_element_type=F32)

    @pl.when(f == pl.num_programs(1) - 1)
    def _():
        xn = x_ref[...] + gate_ref[...] * acc_ref[...]
        if final_norm:
            ms = jnp.mean(xn * xn, axis=-1, keepdims=True)
            xn = (xn * lax.rsqrt(ms + RMS_EPS)) * gf_ref[...]
        o_ref[...] = xn


def _mlp(x2, g, sc, sh, gate, wu_bf16, wd_bf16, gfinal, tm, tf, rows_per_batch, final_norm):
    m, d = x2.shape
    ff = wu_bf16.shape[1]
    return pl.pallas_call(
        functools.partial(_mlp_body, final_norm=final_norm),
        grid=(m // tm, ff // tf),
        in_specs=[pl.BlockSpec((tm, d), lambda i, f: (i, 0)),
                  pl.BlockSpec((1, d), lambda i, f: (0, 0)),
                  _mod_spec(sc, tm, rows_per_batch, d),
                  _mod_spec(sh, tm, rows_per_batch, d),
                  _mod_spec(gate, tm, rows_per_batch, d),
                  pl.BlockSpec((d, tf), lambda i, f: (0, f)),
                  pl.BlockSpec((tf, d), lambda i, f: (f, 0)),
                  pl.BlockSpec((1, d), lambda i, f: (0, 0))],
        out_specs=pl.BlockSpec((tm, d), lambda i, f: (i, 0)),
        out_shape=jax.ShapeDtypeStruct((m, d), F32),
        scratch_shapes=[pltpu.VMEM((tm, d), BF16), pltpu.VMEM((tm, d), F32)],
        compiler_params=_cparams(("arbitrary", "arbitrary")),
        name="mlp",
    )(x2, g.reshape(1, d), sc, sh, gate, wu_bf16, wd_bf16, gfinal.reshape(1, d))


def _rwkv_prep_body(za_ref, shift_ref, mu_ref, w0_ref, wup_ref, a0_ref, aup_ref, gup_ref,
                    r_ref, k_ref, v_ref, lw_ref, a_ref, g_ref, carry_ref, *, d_a):
    @pl.when(pl.program_id(1) == 0)
    def _():
        carry_ref[...] = shift_ref[...]

    za = za_ref[...]
    tm = za.shape[0]
    row = lax.broadcasted_iota(jnp.int32, za.shape, 0)
    prev = jnp.where(row == 0, carry_ref[...], pltpu.roll(za, 1, axis=0))
    carry_ref[...] = za[tm - 1:tm, :]
    zs = za + mu_ref[...] * (prev - za)
    o_w = 3 * d_a
    o_a = o_w + LORA_W
    o_g = o_a + LORA_A
    r_ref[...] = zs[:, 0:d_a]
    k_ref[...] = zs[:, d_a:2 * d_a]
    v_ref[...] = zs[:, 2 * d_a:3 * d_a]
    dw = zs[:, o_w:o_a]
    da = zs[:, o_a:o_g]
    dg = zs[:, o_g:o_g + LORA_G]
    lw_ref[...] = -DECAY_SCALE * _sigmoid(w0_ref[...] + _bdot(jnp.tanh(dw), wup_ref[...]))
    a_ref[...] = _sigmoid(a0_ref[...] + _bdot(da, aup_ref[...]))
    g_ref[...] = _bdot(_sigmoid(dg), gup_ref[...])


def _rwkv_prep(za, shift_prev, mu, w0, w_up, a0, a_up, g_up, tm):
    b, t, dp = za.shape
    d_a = w0.shape[0]
    out = jax.ShapeDtypeStruct((b, t, d_a), F32)
    row = lambda n: pl.BlockSpec((1, n), lambda i, j: (0, 0))
    full = lambda w: pl.BlockSpec(w.shape, lambda i, j: (0, 0))
    tile = pl.BlockSpec((None, tm, d_a), lambda i, j: (i, j, 0))
    return pl.pallas_call(
        functools.partial(_rwkv_prep_body, d_a=d_a),
        grid=(b, t // tm),
        in_specs=[pl.BlockSpec((None, tm, dp), lambda i, j: (i, j, 0)),
                  pl.BlockSpec((None, 1, dp), lambda i, j: (i, 0, 0)),
                  row(dp), row(d_a), full(w_up), row(d_a), full(a_up), full(g_up)],
        out_specs=(tile,) * 6,
        out_shape=(out,) * 6,
        scratch_shapes=[pltpu.VMEM((1, dp), F32)],
        compiler_params=_cparams(("arbitrary", "arbitrary")),
        name="rwkv_prep",
    )(za, shift_prev.reshape(b, 1, dp), mu.reshape(1, dp), w0.reshape(1, d_a), w_up,
      a0.reshape(1, d_a), a_up, g_up)


def _segsum(x, ones_blk):
    hi = x.astype(BF16)
    lo = (x - hi.astype(F32)).astype(BF16)
    w = ones_blk.shape[0]
    parts = [jnp.dot(hi[:, c0:c0 + w], ones_blk, preferred_element_type=F32)
             + jnp.dot(lo[:, c0:c0 + w], ones_blk, preferred_element_type=F32)
             for c0 in range(0, x.shape[1], w)]
    return jnp.concatenate(parts, axis=1)


def _rwkv_prepare(r, k, v, lw, a, kkw, kaw, valid, tri_incl, ones_blk):
    c = r.shape[0]
    kk = k * kkw
    kk = kk / jnp.maximum(jnp.sqrt(_segsum(kk * kk, ones_blk)), 1e-12)
    k2 = k * (1.0 + (a - 1.0) * kaw)
    b = kk * a
    if valid is not None:
        zero = lambda x: jnp.where(valid, x, 0.0)
        lw, kk, k2, b, v, r = (zero(x) for x in (lw, kk, k2, b, v, r))
    p1 = lw.astype(BF16)
    d1 = lw - p1.astype(F32)
    p2 = d1.astype(BF16)
    p3 = (d1 - p2.astype(F32)).astype(BF16)
    tri = tri_incl.astype(BF16)
    cum = sum(jnp.dot(tri, p, preferred_element_type=F32) for p in (p1, p2, p3))
    tot = cum[c - 1:c, :]
    e_in = jnp.exp(cum)
    e_out = jnp.exp(-cum)
    e_end = jnp.exp(tot - cum)
    return dict(r=r, k2=k2, v=v, kap_t=kk * jnp.exp(cum - lw), r_t=r * e_in, b_t=b * e_out,
                k_t=k2 * e_out, b_h=b * e_end, k_h=k2 * e_end, e_tot=jnp.exp(tot))


def _rwkv_local(kap_t, r_t, b_t, k_t, b_h, k_h, v, tri_incl, tri_strict, eye, n_doubling):
    c = kap_t[0].shape[0]
    each = lambda f, *ls: [f(*xs) for xs in zip(*ls)]

    rhs = each(lambda x, y: jnp.concatenate([x, y], axis=0), b_t, k_t)
    gk = each(_bdot_nt, kap_t, rhs)
    gr = each(_bdot_nt, r_t, rhs)
    n1 = each(lambda x: jnp.where(tri_strict, x[:, :c], 0.0), gk)
    n2 = each(lambda x: jnp.where(tri_strict, x[:, c:], 0.0), gk)
    m1 = each(lambda x: jnp.where(tri_incl > 0, x[:, :c], 0.0), gr)
    m2 = each(lambda x: jnp.where(tri_incl > 0, x[:, c:], 0.0), gr)

    x = each(lambda n: -n, n1)
    tinv = each(lambda x_: eye + x_, x)
    for _ in range(n_doubling):
        x = each(_bdot, x, x)
        tinv = each(lambda t_, x_: t_ + _bdot(t_, x_), tinv, x)

    nv = each(lambda n, m_, v_: _bdot(jnp.concatenate([n, m_], axis=0), v_), n2, m2, v)
    ty = each(lambda t_, kp, nv_: _bdot(t_, jnp.concatenate([kp, nv_[:c]], axis=1)),
              tinv, kap_t, nv)
    m1ty = each(_bdot, m1, ty)
    rq = each(lambda r_, m_: r_ - m_[:, :HEAD_A], r_t, m1ty)
    ol = each(lambda nv_, m_: nv_[c:] - m_[:, HEAD_A:], nv, m1ty)
    tb = each(_bdot_tn, ty, b_h)
    vk = each(_bdot_tn, v, k_h)
    gmat = each(lambda tb_: tb_[:HEAD_A], tb)
    hmat = each(lambda vk_, tb_: vk_ - tb_[HEAD_A:], vk, tb)
    return rq, ol, gmat, hmat


def _rwkv_chunk_body(r_ref, k_ref, v_ref, lw_ref, a_ref, g_ref, kkw_ref, kaw_ref, rkw_ref,
                     lnw_ref, lnb_ref, s0_ref, y_ref, sout_ref, s_ref, *, chunk, n_chunks,
                     n_heads, n_valid, group):
    @pl.when(pl.program_id(1) == 0)
    def _():
        s_ref[...] = s0_ref[...]

    c = chunk
    d_a = n_heads * HEAD_A
    ri = lax.broadcasted_iota(jnp.int32, (c, c), 0)
    ci = lax.broadcasted_iota(jnp.int32, (c, c), 1)
    tri_incl = (ri >= ci).astype(F32)
    tri_strict = ri > ci
    eye = (ri == ci).astype(F32)
    bi = lax.broadcasted_iota(jnp.int32, (MXU_DIM, MXU_DIM), 0) // HEAD_A
    bj = lax.broadcasted_iota(jnp.int32, (MXU_DIM, MXU_DIM), 1) // HEAD_A
    ones_blk = (bi == bj).astype(BF16)
    valid = None
    if n_valid < c:
        valid = lax.broadcasted_iota(jnp.int32, (c, d_a), 0) < n_valid
    n_doubling = max(int(math.log2(c)) - 1, 0)
    heads = range(n_heads)
    sls = [slice(h * HEAD_A, (h + 1) * HEAD_A) for h in heads]

    def one_group(ig, carry):
        rows, pre = [], []
        for gi in range(group):
            rws = pl.ds(pl.multiple_of((ig * group + gi) * c, c), c)
            rows.append(rws)
            pre.append(_rwkv_prepare(r_ref[rws, :], k_ref[rws, :], v_ref[rws, :], lw_ref[rws, :],
                                     a_ref[rws, :], kkw_ref[...], kaw_ref[...], valid, tri_incl,
                                     ones_blk))
        chains = lambda name: [p[name][:, sl] for p in pre for sl in sls]
        rq, ol, gmat, hmat = _rwkv_local(
            chains('kap_t'), chains('r_t'), chains('b_t'), chains('k_t'), chains('b_h'),
            chains('k_h'), chains('v'), tri_incl, tri_strict, eye, n_doubling)
        s = [s_ref[h] for h in heads]
        for gi in range(group):
            p = pre[gi]
            at = lambda xs, h: xs[gi * n_heads + h]
            o = jnp.concatenate([_bdot_nt(at(rq, h), s[h]) + at(ol, h) for h in heads], axis=1)
            s = [s[h] * p['e_tot'][:, sls[h]] - _bdot(s[h], at(gmat, h)) + at(hmat, h)
                 for h in heads]
            mu = _segsum(o, ones_blk) * (1.0 / HEAD_A)
            dev = o - mu
            var = _segsum(dev * dev, ones_blk) * (1.0 / HEAD_A)
            on = dev * lax.rsqrt(var + GN_EPS) * lnw_ref[...] + lnb_ref[...]
            bonus = _segsum(p['r'] * p['k2'] * rkw_ref[...], ones_blk) * p['v']
            y_ref[rows[gi], :] = (on + bonus) * g_ref[rows[gi], :]
        for h in heads:
            s_ref[h] = s[h]
        return carry

    lax.fori_loop(0, n_chunks // group, one_group, 0)

    @pl.when(pl.program_id(1) == pl.num_programs(1) - 1)
    def _():
        sout_ref[...] = s_ref[...]


def _rwkv_chunked(r, k, v, lw, a, g, k_k, k_a, r_k, lnx_w, lnx_b, s0, tb, chunk, n_valid):
    b, t, d_a = r.shape
    n_heads = d_a // HEAD_A
    tile = pl.BlockSpec((None, tb, d_a), lambda i, j: (i, j, 0))
    row = pl.BlockSpec((1, d_a), lambda i, j: (0, 0))
    st = pl.BlockSpec((None, n_heads, HEAD_A, HEAD_A), lambda i, j: (i, 0, 0, 0))
    return pl.pallas_call(
        functools.partial(_rwkv_chunk_body, chunk=chunk, n_chunks=tb // chunk,
                          n_heads=n_heads, n_valid=n_valid, group=min(4, tb // chunk)),
        grid=(b, t // tb),
        in_specs=[tile] * 6 + [row] * 5 + [st],
        out_specs=(tile, st),
        out_shape=(jax.ShapeDtypeStruct((b, t, d_a), F32),
                   jax.ShapeDtypeStruct((b, n_heads, HEAD_A, HEAD_A), F32)),
        scratch_shapes=[pltpu.VMEM((n_heads, HEAD_A, HEAD_A), F32)],
        compiler_params=_cparams(("arbitrary", "arbitrary")),
        name="rwkv_chunked",
    )(r, k, v, lw, a, g, k_k.reshape(1, d_a), k_a.reshape(1, d_a), r_k.reshape(1, d_a),
      lnx_w.reshape(1, d_a), lnx_b.reshape(1, d_a), s0)


def _s5_disc_body(lr_ref, li_ref, ldt_ref, bre_ref, bim_ref, ar_ref, ai_ref, bbre_ref, bbim_ref):
    lr, li = lr_ref[...], li_ref[...]
    dt = jnp.exp(ldt_ref[...])
    mag = jnp.exp(lr * dt)
    ar = mag * jnp.cos(li * dt)
    ai = mag * jnp.sin(li * dt)
    den = lr * lr + li * li
    fr = ((ar - 1.0) * lr + ai * li) / den
    fi = (ai * lr - (ar - 1.0) * li) / den
    ar_ref[...] = ar
    ai_ref[...] = ai
    b_re, b_im = bre_ref[...], bim_ref[...]
    bbre_ref[...] = fr[:, None, :] * b_re - fi[:, None, :] * b_im
    bbim_ref[...] = fr[:, None, :] * b_im + fi[:, None, :] * b_re


def _s5_discretise(lam_re, lam_im, log_dt, b_re, b_im):
    g, p = lam_re.shape
    n = b_re.shape[2]
    bt = lambda x: jnp.swapaxes(x, 1, 2)
    return pl.pallas_call(
        _s5_disc_body,
        out_shape=(jax.ShapeDtypeStruct((g, p), F32), jax.ShapeDtypeStruct((g, p), F32),
                   jax.ShapeDtypeStruct((g, n, p), F32), jax.ShapeDtypeStruct((g, n, p), F32)),
        name="s5_discretise",
    )(lam_re, lam_im, log_dt.reshape(g, 1), bt(b_re), bt(b_im))


def _gelu_tanh(x):
    c = math.sqrt(2.0 / math.pi)
    return 0.5 * x * (1.0 + jnp.tanh(c * (x + 0.044715 * (x * x * x))))


def _s5_body(u_ref, h0r_ref, h0i_ref, ar_ref, ai_ref, bdre_ref, bdim_ref, cdre_ref, cdim_ref,
             d_ref, wglu_ref, bglu_ref, y_ref, hre_ref, him_ref,
             xr_ref, xi_ref, apr_ref, api_ref, mr_ref, mi_ref, cr_ref, ci_ref, *, last_row):
    j = pl.program_id(1)
    rows, width = xr_ref.shape
    half_u = u_ref.shape[1] // 2
    half_x = width // 2

    ar, ai = ar_ref[...], ai_ref[...]
    a2r, a2i = ar * ar - ai * ai, 2.0 * ar * ai
    a4r, a4i = a2r * a2r - a2i * a2i, 2.0 * a2r * a2i

    @pl.when(j == 0)
    def _():
        cr_ref[...] = h0r_ref[...]
        ci_ref[...] = h0i_ref[...]
        pr, pi = ar, ai
        for s in range(SUBLANES):
            apr_ref[s:s + 1, :] = pr
            api_ref[s:s + 1, :] = pi
            pr, pi = pr * ar - pi * ai, pr * ai + pi * ar
        sub8 = lax.broadcasted_iota(jnp.int32, (SUBLANES, width), 0)
        for rd, (d, pr, pi) in enumerate(((1, ar, ai), (2, a2r, a2i), (4, a4r, a4i))):
            mr_ref[rd] = jnp.where(sub8 >= d, pr, 0.0)
            mi_ref[rd] = jnp.where(sub8 >= d, pi, 0.0)

    ub = u_ref[...].astype(BF16)
    for hf in range(2):
        us = ub[:, hf * half_u:(hf + 1) * half_u]
        cols = slice(hf * half_x, (hf + 1) * half_x)
        xr_ref[:, cols] = jnp.dot(us, bdre_ref[hf], preferred_element_type=F32)
        xi_ref[:, cols] = jnp.dot(us, bdim_ref[hf], preferred_element_type=F32)

    lane_chunk = 4 * LANES
    grouped = lambda x: x.reshape(rows // SUBLANES, SUBLANES, lane_chunk)
    for c0 in range(0, width, lane_chunk):
        cols = slice(c0, c0 + lane_chunk)
        xr, xi = xr_ref[:, cols], xi_ref[:, cols]
        for rd, d in enumerate((1, 2, 4)):
            pr, pi = mr_ref[rd][:, cols][None], mi_ref[rd][:, cols][None]
            sr = grouped(pltpu.roll(xr, d, axis=0))
            si = grouped(pltpu.roll(xi, d, axis=0))
            xr, xi = ((grouped(xr) + (pr * sr - pi * si)).reshape(rows, lane_chunk),
                      (grouped(xi) + (pr * si + pi * sr)).reshape(rows, lane_chunk))
        xr_ref[:, cols] = xr
        xi_ref[:, cols] = xi

    def group(n, carry):
        cr, ci = carry
        r0 = pl.multiple_of(n * SUBLANES, SUBLANES)
        rs = pl.ds(r0, SUBLANES)
        apr, api = apr_ref[...], api_ref[...]
        hr = xr_ref[rs, :] + apr * cr - api * ci
        hi = xi_ref[rs, :] + apr * ci + api * cr
        xr_ref[rs, :] = hr
        xi_ref[rs, :] = hi
        return hr[SUBLANES - 1:SUBLANES, :], hi[SUBLANES - 1:SUBLANES, :]

    cr, ci = lax.fori_loop(0, rows // SUBLANES, group, (cr_ref[...], ci_ref[...]))
    cr_ref[...] = cr
    ci_ref[...] = ci

    @pl.when(j == pl.num_programs(1) - 1)
    def _():
        hre_ref[...] = xr_ref[last_row:last_row + 1, :]
        him_ref[...] = xi_ref[last_row:last_row + 1, :]

    u = u_ref[...]
    for hf in range(2):
        cols = slice(hf * half_x, (hf + 1) * half_x)
        ucols = slice(hf * half_u, (hf + 1) * half_u)
        y = (jnp.dot(xr_ref[:, cols].astype(BF16), cdre_ref[hf], preferred_element_type=F32)
             - jnp.dot(xi_ref[:, cols].astype(BF16), cdim_ref[hf], preferred_element_type=F32)
             + d_ref[:, ucols] * u[:, ucols])
        y_ref[:, ucols] = _gelu_tanh(y)
    y = y_ref[...]
    z = jnp.dot(y.astype(BF16), wglu_ref[...], preferred_element_type=F32) + bglu_ref[...]
    y_ref[...] = y * _sigmoid(z)


def _block_diag_halves(w_gab, transpose):
    g = w_gab.shape[0]
    hg = g // 2
    eye = jnp.eye(hg, dtype=w_gab.dtype)
    halves = []
    for hf in range(2):
        w = w_gab[hf * hg:(hf + 1) * hg]
        if transpose:
            w = jnp.swapaxes(w, 1, 2)
        a, b = w.shape[1], w.shape[2]
        halves.append(jnp.einsum('gab,gh->gahb', w, eye).reshape(hg * a, hg * b))
    return jnp.stack(halves)


def _s5_mix(u, h0_re, h0_im, ar, ai, bbt_re, bbt_im, c_re, c_im, d, w_glu, b_glu, rows, n_valid):
    b, t, d_b = u.shape
    g, p = ar.shape
    width = g * p
    bd_re = _block_diag_halves(bbt_re, False).astype(BF16)
    bd_im = _block_diag_halves(bbt_im, False).astype(BF16)
    cd_re = _block_diag_halves(c_re, True).astype(BF16)
    cd_im = _block_diag_halves(c_im, True).astype(BF16)
    n_blocks = t // rows
    last_row = (n_valid - 1) - (n_blocks - 1) * rows
    full = lambda x: pl.BlockSpec(x.shape, lambda i, j: (0,) * x.ndim)
    st = pl.BlockSpec((None, 1, width), lambda i, j: (i, 0, 0))
    args = (u, h0_re.reshape(b, 1, width), h0_im.reshape(b, 1, width),
            ar.reshape(1, width), ai.reshape(1, width), bd_re, bd_im, cd_re, cd_im,
            d.reshape(1, d_b), w_glu.astype(BF16), b_glu.reshape(1, d_b))
    return pl.pallas_call(
        functools.partial(_s5_body, last_row=last_row),
        grid=(b, n_blocks),
        in_specs=[pl.BlockSpec((None, rows, d_b), lambda i, j: (i, j, 0)), st, st]
                 + [full(x) for x in args[3:]],
        out_specs=(pl.BlockSpec((None, rows, d_b), lambda i, j: (i, j, 0)), st, st),
        out_shape=(jax.ShapeDtypeStruct((b, t, d_b), F32),
                   jax.ShapeDtypeStruct((b, 1, width), F32),
                   jax.ShapeDtypeStruct((b, 1, width), F32)),
        scratch_shapes=[pltpu.VMEM((rows, width), F32), pltpu.VMEM((rows, width), F32),
                        pltpu.VMEM((SUBLANES, width), F32), pltpu.VMEM((SUBLANES, width), F32),
                        pltpu.VMEM((3, SUBLANES, width), F32), pltpu.VMEM((3, SUBLANES, width), F32),
                        pltpu.VMEM((1, width), F32), pltpu.VMEM((1, width), F32)],
        compiler_params=_cparams(("arbitrary", "arbitrary")),
        name="s5_mix",
    )(*args)


def _lambda_full(lq1, lk1, lq2, lk2):
    s1 = jnp.sum(lq1 * lk1, axis=-1, keepdims=True)
    s2 = jnp.sum(lq2 * lk2, axis=-1, keepdims=True)
    return jnp.exp(s1) - jnp.exp(s2) + LAMBDA_INIT


def _sub_ln(o, subln):
    ms = jnp.mean(o * o, axis=-1, keepdims=True)
    return o * lax.rsqrt(ms + RMS_EPS) * subln * (1.0 - LAMBDA_INIT)


def _attn_prompt_body(q_ref, k_ref, v_ref, lq1_ref, lk1_ref, lq2_ref, lk2_ref, subln_ref,
                      o_ref, kb_ref, vt_ref, m_ref, l_ref, acc_ref, sa_ref, sb_ref, *, tile,
                      n_tiles):
    qi = pl.program_id(2)
    halves = range(2)

    @pl.when(qi == 0)
    def _():
        for c in range(n_tiles):
            rows = slice(c * tile, (c + 1) * tile)
            kb_ref[rows, :] = k_ref[rows, :].astype(BF16)
            vt_ref[c] = v_ref[rows, :].T.astype(BF16)

    q = q_ref[...] * (SCALE_C * math.log2(math.e))
    lane = lax.broadcasted_iota(jnp.int32, q.shape, 1)
    qs = [jnp.where((lane >= HALF_C) == bool(c), q, 0.0).astype(BF16) for c in halves]
    m_ref[...] = jnp.full_like(m_ref, NEG_INF)
    l_ref[...] = jnp.zeros_like(l_ref)
    acc_ref[...] = jnp.zeros_like(acc_ref)

    def scores(kj, slot):
        k0 = kj * tile if isinstance(kj, int) else pl.multiple_of(kj * tile, tile)
        kb = kb_ref[pl.ds(k0, tile), :]
        for c in halves:
            slot[c] = lax.dot_general(kb, qs[c], (((1,), (1,)), ((), ())),
                                      preferred_element_type=F32)

    def kv_step(kj, slot, diagonal):
        st = [slot[c] for c in halves]
        if diagonal:
            krow = lax.broadcasted_iota(jnp.int32, st[0].shape, 0)
            qcol = lax.broadcasted_iota(jnp.int32, st[0].shape, 1)
            st = [jnp.where(krow <= qcol, x, NEG_INF) for x in st]
        m_prev = [m_ref[c] for c in halves]
        m_new = [jnp.maximum(m_prev[c], jnp.max(st[c], axis=0, keepdims=True)) for c in halves]
        alpha = [jnp.exp2(m_prev[c] - m_new[c]) for c in halves]
        p = [jnp.exp2(st[c] - m_new[c]) for c in halves]
        vt = vt_ref[kj]
        pv = [jnp.dot(vt, p[c].astype(BF16), preferred_element_type=F32) for c in halves]
        for c in halves:
            l_ref[c] = alpha[c] * l_ref[c] + jnp.sum(p[c], axis=0, keepdims=True)
            acc_ref[c] = alpha[c] * acc_ref[c] + pv[c]
            m_ref[c] = m_new[c]

    s_even, s_odd = sa_ref, sb_ref
    scores(0, s_even)

    def two_steps(i, carry):
        scores(2 * i + 1, s_odd)
        kv_step(2 * i, s_even, False)
        scores(2 * i + 2, s_even)
        kv_step(2 * i + 1, s_odd, False)
        return carry

    lax.fori_loop(0, qi // 2, two_steps, 0)

    @pl.when(qi % 2 == 0)
    def _():
        kv_step(qi, s_even, True)

    @pl.when(qi % 2 == 1)
    def _():
        scores(qi, s_odd)
        kv_step(qi - 1, s_even, False)
        kv_step(qi, s_odd, True)

    lam = _lambda_full(lq1_ref[...], lk1_ref[...], lq2_ref[...], lk2_ref[...])
    ot = acc_ref[0] / l_ref[0] - lam * (acc_ref[1] / l_ref[1])
    ms = jnp.mean(ot * ot, axis=0, keepdims=True)
    ot = ot * lax.rsqrt(ms + RMS_EPS) * subln_ref[...] * (1.0 - LAMBDA_INIT)
    o_ref[...] = ot.T


def _attn_prompt(q, k, v, lq1, lk1, lq2, lk2, subln, tile):
    b, t, d = q.shape
    n_heads = d // HEAD_C
    n_tiles = t // tile
    small = lambda n: pl.BlockSpec((1, n), lambda i, h, j: (0, 0))
    return pl.pallas_call(
        functools.partial(_attn_prompt_body, tile=tile, n_tiles=n_tiles),
        grid=(b, n_heads, n_tiles),
        in_specs=[pl.BlockSpec((None, tile, HEAD_C), lambda i, h, j: (i, j, h)),
                  pl.BlockSpec((None, t, HEAD_C), lambda i, h, j: (i, 0, h)),
                  pl.BlockSpec((None, t, HEAD_C), lambda i, h, j: (i, 0, h)),
                  small(HALF_C), small(HALF_C), small(HALF_C), small(HALF_C),
                  pl.BlockSpec((HEAD_C, 1), lambda i, h, j: (0, 0))],
        out_specs=pl.BlockSpec((None, tile, HEAD_C), lambda i, h, j: (i, j, h)),
        out_shape=jax.ShapeDtypeStruct((b, t, d), F32),
        scratch_shapes=[pltpu.VMEM((t, HEAD_C), BF16), pltpu.VMEM((n_tiles, HEAD_C, tile), BF16),
                        pltpu.VMEM((2, 1, tile), F32), pltpu.VMEM((2, 1, tile), F32),
                        pltpu.VMEM((2, HEAD_C, tile), F32),
                        pltpu.VMEM((2, tile, tile), F32), pltpu.VMEM((2, tile, tile), F32)],
        compiler_params=_cparams(("arbitrary", "arbitrary", "arbitrary")),
        name="attn_prompt",
    )(q, k, v, lq1.reshape(1, -1), lk1.reshape(1, -1), lq2.reshape(1, -1), lk2.reshape(1, -1),
      subln.reshape(-1, 1))


def _attn_sample_body(pt_ref, qrep_ref, knew_ref, vnew_ref, *rest, pages_per_step, n_heads,
                      t_new, page):
    kp_refs = rest[:pages_per_step]
    vp_refs = rest[pages_per_step:2 * pages_per_step]
    lq1_ref, lk1_ref, lq2_ref, lk2_ref, subln_ref = rest[2 * pages_per_step:2 * pages_per_step + 5]
    o_ref = rest[2 * pages_per_step + 5]
    qw_ref, m_ref, l_ref, acc_ref = rest[2 * pages_per_step + 6:]
    j = pl.program_id(1)
    pairs = range(n_heads // 2)
    q_rows = qrep_ref.shape[1]

    @pl.when(j == 0)
    def _():
        qr = qrep_ref[...]
        row = lax.broadcasted_iota(jnp.int32, qr.shape, 1)
        lane = lax.broadcasted_iota(jnp.int32, qr.shape, 2)
        qm = jnp.where(lane // HALF_C == row // t_new, qr * SCALE_C, 0.0)
        zero = jnp.zeros((q_rows, HEAD_C), F32)
        for pr in pairs:
            qw_ref[pr] = jnp.concatenate(
                [jnp.concatenate([qm[2 * pr], zero], axis=1),
                 jnp.concatenate([zero, qm[2 * pr + 1]], axis=1)], axis=0).astype(BF16)
        m_ref[...] = jnp.full_like(m_ref, NEG_INF)
        l_ref[...] = jnp.zeros_like(l_ref)
        acc_ref[...] = jnp.zeros_like(acc_ref)

    def pair_rows(ref, pr):
        return jnp.concatenate([ref[pl.ds(2 * pr + i, page, stride=n_heads), :] for i in range(2)],
                               axis=1).astype(BF16)

    def update(s, values):
        m_prev = [m_ref[pr] for pr in pairs]
        m_new = [jnp.maximum(m_prev[pr], jnp.max(s[pr], axis=-1, keepdims=True)) for pr in pairs]
        alpha = [jnp.exp(m_prev[pr] - m_new[pr]) for pr in pairs]
        p = [jnp.exp(s[pr] - m_new[pr]) for pr in pairs]
        pv = []
        for pr in pairs:
            w = p[pr].shape[1] // len(values[pr])
            acc = None
            for i, vb in enumerate(values[pr]):
                d = jnp.dot(p[pr][:, i * w:(i + 1) * w].astype(BF16), vb,
                            preferred_element_type=F32)
                acc = d if acc is None else acc + d
            pv.append(acc)
        for pr in pairs:
            l_ref[pr] = alpha[pr] * l_ref[pr] + jnp.sum(p[pr], axis=-1, keepdims=True)
            acc_ref[pr] = alpha[pr] * acc_ref[pr] + pv[pr]
            m_ref[pr] = m_new[pr]

    s = [jnp.concatenate(
            [lax.dot_general(qw_ref[pr], pair_rows(kp, pr), (((1,), (1,)), ((), ())),
                             preferred_element_type=F32) for kp in kp_refs], axis=1)
         for pr in pairs]
    update(s, [[pair_rows(vp, pr) for vp in vp_refs] for pr in pairs])

    @pl.when(j == pl.num_programs(1) - 1)
    def _():
        both = lambda ref, pr: jnp.concatenate([ref[2 * pr], ref[2 * pr + 1]], axis=1).astype(BF16)
        sn = []
        for pr in pairs:
            x = lax.dot_general(qw_ref[pr], both(knew_ref, pr), (((1,), (1,)), ((), ())),
                                preferred_element_type=F32)
            trow = lax.broadcasted_iota(jnp.int32, x.shape, 0) % t_new
            tcol = lax.broadcasted_iota(jnp.int32, x.shape, 1)
            sn.append(jnp.where(tcol <= trow, x, NEG_INF))
        update(sn, [[both(vnew_ref, pr)] for pr in pairs])
        lam = _lambda_full(lq1_ref[...], lk1_ref[...], lq2_ref[...], lk2_ref[...])
        for pr in pairs:
            on = acc_ref[pr] / l_ref[pr]
            for i in range(2):
                h = 2 * pr + i
                r0 = i * q_rows
                oh = on[r0:r0 + 2 * t_new, i * HEAD_C:(i + 1) * HEAD_C]
                o = oh[0:t_new] - lam * oh[t_new:2 * t_new]
                o_ref[:, h * HEAD_C:(h + 1) * HEAD_C] = _sub_ln(o, subln_ref[...])


def _attn_sample(q, k_new, v_new, cache_k, cache_v, page_table, lq1, lk1, lq2, lk2, subln,
                 pages_per_step):
    db, t_new, d = q.shape
    n_heads = d // HEAD_C
    n_pool, page = cache_k.shape[0], cache_k.shape[1]
    n_pages = page_table.shape[1]
    ck = cache_k.reshape(n_pool, page * n_heads, HEAD_C)
    cv = cache_v.reshape(n_pool, page * n_heads, HEAD_C)
    by_head = lambda x: jnp.swapaxes(x.reshape(db, t_new, n_heads, HEAD_C), 1, 2)
    q_rows = NEW_PAD
    qrep = jnp.tile(by_head(q), (1, 1, q_rows // t_new, 1))
    pad = ((0, 0), (0, 0), (0, NEW_PAD - t_new), (0, 0))
    knew = jnp.pad(by_head(k_new), pad)
    vnew = jnp.pad(by_head(v_new), pad)
    pt = page_table.reshape(-1)

    def page_spec(i):
        return pl.BlockSpec((None, page * n_heads, HEAD_C),
                            lambda b, j, pt_ref: (pt_ref[b * n_pages + j * pages_per_step + i], 0, 0))

    small = lambda n: pl.BlockSpec((1, n), lambda b, j, pt_ref: (0, 0))
    per_b = lambda r: pl.BlockSpec((None, n_heads, r, HEAD_C), lambda b, j, pt_ref: (b, 0, 0, 0))
    grid_spec = pltpu.PrefetchScalarGridSpec(
        num_scalar_prefetch=1,
        grid=(db, n_pages // pages_per_step),
        in_specs=[per_b(q_rows), per_b(NEW_PAD), per_b(NEW_PAD)]
                 + [page_spec(i) for i in range(pages_per_step)] * 2
                 + [small(HALF_C)] * 4 + [small(HEAD_C)],
        out_specs=pl.BlockSpec((None, t_new, d), lambda b, j, pt_ref: (b, 0, 0)),
        scratch_shapes=[pltpu.VMEM((n_heads // 2, 2 * q_rows, 2 * HEAD_C), BF16),
                        pltpu.VMEM((n_heads // 2, 2 * q_rows, 1), F32),
                        pltpu.VMEM((n_heads // 2, 2 * q_rows, 1), F32),
                        pltpu.VMEM((n_heads // 2, 2 * q_rows, 2 * HEAD_C), F32)])
    return pl.pallas_call(
        functools.partial(_attn_sample_body, pages_per_step=pages_per_step, n_heads=n_heads,
                          t_new=t_new, page=page),
        grid_spec=grid_spec,
        out_shape=jax.ShapeDtypeStruct((db, t_new, d), F32),
        compiler_params=_cparams(("arbitrary", "arbitrary")),
        name="attn_sample",
    )(pt, qrep, knew, vnew, *([ck] * pages_per_step), *([cv] * pages_per_step),
      lq1.reshape(1, -1), lk1.reshape(1, -1), lq2.reshape(1, -1), lk2.reshape(1, -1),
      subln.reshape(1, -1))


def _pad_time(x, t_pad):
    return jnp.pad(x, ((0, 0), (0, t_pad - x.shape[1]), (0, 0)))


def _trunk(x, mods, shift0, wkv0, sre0, sim0, attend, W, per_row_mod):
    b, t, d = x.shape
    m = b * t
    rows_mod = m if per_row_mod else t
    tm = min(512, rows_mod)
    tm_mlp = min(1024, rows_mod)
    d_a = W['rwkv_w0'].shape[0]
    d_ap = W['rwkv_mu'].shape[0]

    def mod_arg(v):
        if per_row_mod:
            return jnp.repeat(v, t, axis=0)
        return v.reshape(b, 1, d)

    x2 = x.reshape(m, d)

    sh_m, sc_m, g_m, sh_f, sc_f, g_f = (mod_arg(v) for v in mods[0])
    za2, u2 = _normmod_mm(x2, W['norm_mix'][0], sc_m, sh_m, W['w_in_ab_bf16'],
                          ((0, d_ap), (d_ap, W['w_in_ab_bf16'].shape[1])), tm, t)
    za = za2.reshape(b, t, d_ap)
    u = u2.reshape(b, t, -1)
    shift1 = za[:, -1]

    chunk = RWKV_CHUNK
    t_pad = -(-t // chunk) * chunk
    if t_pad != t:
        za_p, u_p = _pad_time(za, t_pad), _pad_time(u, t_pad)
    else:
        za_p, u_p = za, u
    tb = min(512, t_pad)
    r, k, v, lw, a, g = _rwkv_prep(za_p, shift0, W['rwkv_mu'], W['rwkv_w0'], W['rwkv_w_up'],
                                   W['rwkv_a0'], W['rwkv_a_up'], W['rwkv_g_up'], tb)
    y_a, wkv1 = _rwkv_chunked(r, k, v, lw, a, g, W['rwkv_k_k'], W['rwkv_k_a'], W['rwkv_r_k'],
                              W['rwkv_lnx_w'], W['rwkv_lnx_b'], wkv0, tb, chunk,
                              chunk if t_pad == t else t)
    s5_rows = min(256, t_pad)
    y_b, sre1, sim1 = _s5_mix(u_p, sre0, sim0, W['s5_ar'], W['s5_ai'], W['s5_bbt_re'],
                              W['s5_bbt_im'], W['s5_c_re'], W['s5_c_im'], W['s5_d'],
                              W['s5_w_glu'], W['s5_b_glu'], s5_rows, t)
    y_a2 = y_a[:, :t].reshape(m, d_a)
    y_b2 = y_b[:, :t].reshape(m, -1)
    wo = W['w_out_bf16'][0]
    x2 = _proj_res(x2, g_m, (y_a2, y_b2), (wo[:d_a], wo[d_a:]), tm, t)
    x2 = _mlp(x2, W['norm_mlp'][0], sc_f, sh_f, g_f, W['w_up_bf16'][0], W['w_down_bf16'][0],
              W['norm_f'], tm_mlp, 1024, t, False)

    sh_m, sc_m, g_m, sh_f, sc_f, g_f = (mod_arg(v) for v in mods[1])
    q2, k2, v2 = _normmod_mm(x2, W['norm_mix'][1], sc_m, sh_m, W['diff_w_qkv_bf16'],
                             ((0, d), (d, 2 * d), (2 * d, 3 * d)), tm, t)
    q3, k3, v3 = (z.reshape(b, t, d) for z in (q2, k2, v2))
    o = attend(q3, k3, v3)
    x2 = _proj_res(x2, g_m, (o.reshape(m, d),), (W['w_out_bf16'][1],), tm, t)
    y2 = _mlp(x2, W['norm_mlp'][1], sc_f, sh_f, g_f, W['w_up_bf16'][1], W['w_down_bf16'][1],
              W['norm_f'], tm_mlp, 1024, t, True)

    n_heads_c = d // HEAD_C
    g_b, p_b = W['s5_ar'].shape
    return (y2.reshape(b, t, d), shift1, wkv1, sre1.reshape(b, g_b, p_b), sim1.reshape(b, g_b, p_b),
            k3.reshape(b, t, n_heads_c, HEAD_C), v3.reshape(b, t, n_heads_c, HEAD_C))


def kernel(x_prompt, x_sample, state_shift, state_wkv, state_ssm_re, state_ssm_im, cache_k, cache_v, page_table, c_prompt, c_sample, norm_mix, norm_mlp, norm_f, w_ada, b_ada, w_out, w_up, w_down, w_in_ab, rwkv_mu, rwkv_w0, rwkv_w_up, rwkv_a0, rwkv_a_up, rwkv_g_up, rwkv_k_k, rwkv_k_a, rwkv_r_k, rwkv_lnx_w, rwkv_lnx_b, s5_lam_re, s5_lam_im, s5_log_dt, s5_b_re, s5_b_im, s5_c_re, s5_c_im, s5_d, s5_w_glu, s5_b_glu, diff_w_qkv, diff_lq1, diff_lk1, diff_lq2, diff_lk2, diff_subln):
    bp, tp, d = x_prompt.shape
    db, ts, _ = x_sample.shape
    depth = w_ada.shape[0]
    n_heads_a = rwkv_r_k.shape[0]
    g_b, p_b = s5_lam_re.shape

    ar, ai, bbt_re, bbt_im = _s5_discretise(s5_lam_re, s5_lam_im, s5_log_dt, s5_b_re, s5_b_im)
    W = dict(norm_mix=norm_mix, norm_mlp=norm_mlp, norm_f=norm_f,
             w_out_bf16=w_out.astype(BF16), w_up_bf16=w_up.astype(BF16),
             w_down_bf16=w_down.astype(BF16), w_in_ab_bf16=w_in_ab.astype(BF16),
             diff_w_qkv_bf16=diff_w_qkv.astype(BF16),
             rwkv_mu=rwkv_mu, rwkv_w0=rwkv_w0, rwkv_w_up=rwkv_w_up, rwkv_a0=rwkv_a0,
             rwkv_a_up=rwkv_a_up, rwkv_g_up=rwkv_g_up, rwkv_k_k=rwkv_k_k, rwkv_k_a=rwkv_k_a,
             rwkv_r_k=rwkv_r_k.reshape(-1), rwkv_lnx_w=rwkv_lnx_w, rwkv_lnx_b=rwkv_lnx_b,
             s5_ar=ar, s5_ai=ai, s5_bbt_re=bbt_re, s5_bbt_im=bbt_im,
             s5_c_re=s5_c_re, s5_c_im=s5_c_im, s5_d=s5_d.reshape(-1), s5_w_glu=s5_w_glu,
             s5_b_glu=s5_b_glu)

    n_c = bp + db
    rows_c = -(-n_c // SUBLANES) * SUBLANES
    c_all = jnp.pad(jnp.concatenate([c_prompt, c_sample], axis=0), ((0, rows_c - n_c), (0, 0)))
    mod = _ada_mod(c_all, w_ada, b_ada)
    mods_p = [tuple(mod[l, :bp, i * d:(i + 1) * d] for i in range(6)) for l in range(depth)]
    mods_s = [tuple(mod[l, bp:n_c, i * d:(i + 1) * d] for i in range(6)) for l in range(depth)]

    lam_args = (diff_lq1, diff_lk1, diff_lq2, diff_lk2, diff_subln)
    attend_p = lambda q, k, v: _attn_prompt(q, k, v, *lam_args, tile=min(512, tp))
    y_prompt, p_shift, p_wkv, p_re, p_im, p_k, p_v = _trunk(
        x_prompt, mods_p,
        jnp.zeros((bp, rwkv_mu.shape[0]), F32),
        jnp.zeros((bp, n_heads_a, HEAD_A, HEAD_A), F32),
        jnp.zeros((bp, g_b, p_b), F32), jnp.zeros((bp, g_b, p_b), F32),
        attend_p, W, per_row_mod=False)

    attend_s = lambda q, k, v: _attn_sample(q, k, v, cache_k, cache_v, page_table, *lam_args,
                                            pages_per_step=8)
    y_sample, s_shift, s_wkv, s_re, s_im, s_k, s_v = _trunk(
        x_sample, mods_s, state_shift, state_wkv, state_ssm_re, state_ssm_im,
        attend_s, W, per_row_mod=True)

    return (y_prompt, y_sample, p_shift, p_wkv, p_re, p_im, p_k, p_v,
            s_shift, s_wkv, s_re, s_im, s_k, s_v)
```

```python
import functools
import math

import jax
import jax.numpy as jnp
from jax import lax
from jax.experimental import pallas as pl
from jax.experimental.pallas import tpu as pltpu

F32 = jnp.float32
BF16 = jnp.bfloat16
HIGHEST = lax.Precision.HIGHEST

HEAD_A = 64
LORA_W, LORA_A, LORA_G = 64, 64, 128
S5_GROUP = 16
S5_STATE = 64
HEAD_C = 128
HALF_C = HEAD_C // 2
SCALE_C = HALF_C ** -0.5
RMS_EPS = 1e-6
GN_EPS = 64e-5
DECAY_SCALE = math.exp(-0.5)
LAMBDA_INIT = 0.8 - 0.6 * math.exp(-0.3 * 1)
NEG_INF = -1e30

SUBLANES = 8
LANES = 128
MXU_DIM = 256
VMEM_LIMIT = 56 * 1024 * 1024

RWKV_CHUNK = 64
RWKV_CHUNK_SHORT = 16
NEW_PAD = 16


def _cparams(sem):
    return pltpu.CompilerParams(dimension_semantics=sem, vmem_limit_bytes=VMEM_LIMIT)


def _bdot(a, b):
    return jnp.dot(a.astype(BF16), b.astype(BF16), preferred_element_type=F32)


def _bdot_nt(a, b):
    return lax.dot_general(a.astype(BF16), b.astype(BF16), (((1,), (1,)), ((), ())),
                           preferred_element_type=F32)


def _bdot_tn(a, b):
    return lax.dot_general(a.astype(BF16), b.astype(BF16), (((0,), (0,)), ((), ())),
                           preferred_element_type=F32)


def _sigmoid(x):
    return 1.0 / (1.0 + jnp.exp(-x))


def _ada_body(c_ref, w_ref, b_ref, o_ref):
    c = c_ref[...]
    cond = c * _sigmoid(c)
    o_ref[...] = jnp.dot(cond, w_ref[...], precision=HIGHEST,
                         preferred_element_type=F32) + b_ref[...]


def _ada_mod(c, w_ada, b_ada):
    depth, d, n = w_ada.shape
    rows = c.shape[0]
    tn = 1536
    return pl.pallas_call(
        _ada_body,
        grid=(depth, n // tn),
        in_specs=[pl.BlockSpec((rows, d), lambda l, j: (0, 0)),
                  pl.BlockSpec((None, d, tn), lambda l, j: (l, 0, j)),
                  pl.BlockSpec((None, 1, tn), lambda l, j: (l, 0, j))],
        out_specs=pl.BlockSpec((None, rows, tn), lambda l, j: (l, 0, j)),
        out_shape=jax.ShapeDtypeStruct((depth, rows, n), F32),
        compiler_params=_cparams(("arbitrary", "arbitrary")),
        name="ada_mod",
    )(c, w_ada, b_ada.reshape(depth, 1, n))


def _mod_spec(mod, tm, rows_per_batch, d):
    if mod.ndim == 3:
        tiles = rows_per_batch // tm
        return pl.BlockSpec((None, 1, d), lambda i, *_: (i // tiles, 0, 0))
    return pl.BlockSpec((tm, d), lambda i, *_: (i, 0))


def _norm_mod(x, g, sc, sh):
    ms = jnp.mean(x * x, axis=-1, keepdims=True)
    h = (x * lax.rsqrt(ms + RMS_EPS)) * g
    return h * (1.0 + sc) + sh


def _normmod_mm_body(x_ref, g_ref, sc_ref, sh_ref, w_ref, *o_refs, splits, col_chunk):
    hb = _norm_mod(x_ref[...], g_ref[...], sc_ref[...], sh_ref[...]).astype(BF16)
    for o_ref, (c0, c1) in zip(o_refs, splits):
        for s in range(c0, c1, col_chunk):
            e = min(s + col_chunk, c1)
            o_ref[:, s - c0:e - c0] = jnp.dot(hb, w_ref[:, s:e], preferred_element_type=F32)


def _normmod_mm(x2, g, sc, sh, w_bf16, splits, tm, rows_per_batch):
    m, d = x2.shape
    n = w_bf16.shape[1]
    outs = tuple(jax.ShapeDtypeStruct((m, c1 - c0), F32) for c0, c1 in splits)
    return pl.pallas_call(
        functools.partial(_normmod_mm_body, splits=splits, col_chunk=512),
        grid=(m // tm,),
        in_specs=[pl.BlockSpec((tm, d), lambda i: (i, 0)),
                  pl.BlockSpec((1, d), lambda i: (0, 0)),
                  _mod_spec(sc, tm, rows_per_batch, d),
                  _mod_spec(sh, tm, rows_per_batch, d),
                  pl.BlockSpec((d, n), lambda i: (0, 0))],
        out_specs=tuple(pl.BlockSpec((tm, c1 - c0), lambda i: (i, 0)) for c0, c1 in splits),
        out_shape=outs,
        compiler_params=_cparams(("arbitrary",)),
        name="normmod_mm",
    )(x2, g.reshape(1, d), sc, sh, w_bf16)


def _mix_mlp_body(*refs, n_in, final_norm):
    x_ref, gm_ref = refs[0], refs[1]
    y_refs = refs[2:2 + n_in]
    w_refs = refs[2 + n_in:2 + 2 * n_in]
    (g_ref, sc_ref, sh_ref, gate_ref, wu_ref, wd_ref, gf_ref, o_ref,
     hb_ref, acc_ref, x1_ref) = refs[2 + 2 * n_in:]
    f = pl.program_id(1)

    @pl.when(f == 0)
    def _():
        mix = None
        for y_ref, w_ref in zip(y_refs, w_refs):
            p = jnp.dot(y_ref[...].astype(BF16), w_ref[...], preferred_element_type=F32)
            mix = p if mix is None else mix + p
        x1 = x_ref[...] + gm_ref[...] * mix
        x1_ref[...] = x1
        hb_ref[...] = _norm_mod(x1, g_ref[...], sc_ref[...], sh_ref[...]).astype(BF16)
        acc_ref[...] = jnp.zeros_like(acc_ref)

    up = jnp.dot(hb_ref[...], wu_ref[...], preferred_element_type=F32)
    act = jnp.square(jnp.maximum(up, 0.0)).astype(BF16)
    acc_ref[...] += jnp.dot(act, wd_ref[...], preferred_element_type=F32)

    @pl.when(f == pl.num_programs(1) - 1)
    def _():
        xn = x1_ref[...] + gate_ref[...] * acc_ref[...]
        if final_norm:
            ms = jnp.mean(xn * xn, axis=-1, keepdims=True)
            xn = (xn * lax.rsqrt(ms + RMS_EPS)) * gf_ref[...]
        o_ref[...] = xn


def _mix_mlp(x2, gate_m, ys, ws_bf16, g, sc, sh, gate, wu_bf16, wd_bf16, gfinal, tm, tf,
             rows_per_batch, final_norm):
    m, d = x2.shape
    ff = wu_bf16.shape[1]
    mod = lambda a: _mod_spec(a, tm, rows_per_batch, d)
    row = pl.BlockSpec((1, d), lambda i, f: (0, 0))
    in_specs = [pl.BlockSpec((tm, d), lambda i, f: (i, 0)), mod(gate_m)]
    in_specs += [pl.BlockSpec((tm, y.shape[1]), lambda i, f: (i, 0)) for y in ys]
    in_specs += [pl.BlockSpec(w.shape, lambda i, f: (0, 0)) for w in ws_bf16]
    in_specs += [row, mod(sc), mod(sh), mod(gate),
                 pl.BlockSpec((d, tf), lambda i, f: (0, f)),
                 pl.BlockSpec((tf, d), lambda i, f: (f, 0)), row]
    return pl.pallas_call(
        functools.partial(_mix_mlp_body, n_in=len(ys), final_norm=final_norm),
        grid=(m // tm, ff // tf),
        in_specs=in_specs,
        out_specs=pl.BlockSpec((tm, d), lambda i, f: (i, 0)),
        out_shape=jax.ShapeDtypeStruct((m, d), F32),
        scratch_shapes=[pltpu.VMEM((tm, d), BF16), pltpu.VMEM((tm, d), F32),
                        pltpu.VMEM((tm, d), F32)],
        compiler_params=_cparams(("arbitrary", "arbitrary")),
        name="mix_mlp",
    )(x2, gate_m, *ys, *ws_bf16, g.reshape(1, d), sc, sh, gate, wu_bf16, wd_bf16,
      gfinal.reshape(1, d))


def _rwkv_prep_body(za_ref, shift_ref, mu_ref, w0_ref, wup_ref, a0_ref, aup_ref, gup_ref,
                    r_ref, k_ref, v_ref, lw_ref, a_ref, g_ref, carry_ref, *, d_a):
    @pl.when(pl.program_id(1) == 0)
    def _():
        carry_ref[...] = shift_ref[...]

    za = za_ref[...]
    tm = za.shape[0]
    row = lax.broadcasted_iota(jnp.int32, za.shape, 0)
    prev = jnp.where(row == 0, carry_ref[...], pltpu.roll(za, 1, axis=0))
    carry_ref[...] = za[tm - 1:tm, :]
    zs = za + mu_ref[...] * (prev - za)
    o_w = 3 * d_a
    o_a = o_w + LORA_W
    o_g = o_a + LORA_A
    r_ref[...] = zs[:, 0:d_a]
    k_ref[...] = zs[:, d_a:2 * d_a]
    v_ref[...] = zs[:, 2 * d_a:3 * d_a]
    dw = zs[:, o_w:o_a]
    da = zs[:, o_a:o_g]
    dg = zs[:, o_g:o_g + LORA_G]
    lw_ref[...] = -DECAY_SCALE * _sigmoid(w0_ref[...] + _bdot(jnp.tanh(dw), wup_ref[...]))
    a_ref[...] = _sigmoid(a0_ref[...] + _bdot(da, aup_ref[...]))
    g_ref[...] = _bdot(_sigmoid(dg), gup_ref[...])


def _rwkv_prep(za, shift_prev, mu, w0, w_up, a0, a_up, g_up, tm):
    b, t, dp = za.shape
    d_a = w0.shape[0]
    out = jax.ShapeDtypeStruct((b, t, d_a), F32)
    row = lambda n: pl.BlockSpec((1, n), lambda i, j: (0, 0))
    full = lambda w: pl.BlockSpec(w.shape, lambda i, j: (0, 0))
    tile = pl.BlockSpec((None, tm, d_a), lambda i, j: (i, j, 0))
    return pl.pallas_call(
        functools.partial(_rwkv_prep_body, d_a=d_a),
        grid=(b, t // tm),
        in_specs=[pl.BlockSpec((None, tm, dp), lambda i, j: (i, j, 0)),
                  pl.BlockSpec((None, 1, dp), lambda i, j: (i, 0, 0)),
                  row(dp), row(d_a), full(w_up), row(d_a), full(a_up), full(g_up)],
        out_specs=(tile,) * 6,
        out_shape=(out,) * 6,
        scratch_shapes=[pltpu.VMEM((1, dp), F32)],
        compiler_params=_cparams(("arbitrary", "arbitrary")),
        name="rwkv_prep",
    )(za, shift_prev.reshape(b, 1, dp), mu.reshape(1, dp), w0.reshape(1, d_a), w_up,
      a0.reshape(1, d_a), a_up, g_up)


def _segsum(x, ones_blk):
    hi = x.astype(BF16)
    lo = (x - hi.astype(F32)).astype(BF16)
    w = ones_blk.shape[0]
    parts = [jnp.dot(hi[:, c0:c0 + w], ones_blk, preferred_element_type=F32)
             + jnp.dot(lo[:, c0:c0 + w], ones_blk, preferred_element_type=F32)
             for c0 in range(0, x.shape[1], w)]
    return jnp.concatenate(parts, axis=1)


def _rwkv_prepare(r, k, v, lw, a, kkw, kaw, valid, tri_incl, ones_blk):
    c = r.shape[0]
    kk = k * kkw
    kk = kk / jnp.maximum(jnp.sqrt(_segsum(kk * kk, ones_blk)), 1e-12)
    k2 = k * (1.0 + (a - 1.0) * kaw)
    b = kk * a
    if valid is not None:
        zero = lambda x: jnp.where(valid, x, 0.0)
        lw, kk, k2, b, v, r = (zero(x) for x in (lw, kk, k2, b, v, r))
    p1 = lw.astype(BF16)
    d1 = lw - p1.astype(F32)
    p2 = d1.astype(BF16)
    p3 = (d1 - p2.astype(F32)).astype(BF16)
    tri = tri_incl.astype(BF16)
    cum = sum(jnp.dot(tri, p, preferred_element_type=F32) for p in (p1, p2, p3))
    tot = cum[c - 1:c, :]
    e_in = jnp.exp(cum)
    e_out = jnp.exp(-cum)
    e_end = jnp.exp(tot - cum)
    return dict(r=r, k2=k2, v=v, kap_t=kk * jnp.exp(cum - lw), r_t=r * e_in, b_t=b * e_out,
                k_t=k2 * e_out, b_h=b * e_end, k_h=k2 * e_end, e_tot=jnp.exp(tot))


def _rwkv_local(kap_t, r_t, b_t, k_t, b_h, k_h, v, tri_incl, tri_strict, eye, n_doubling):
    c = kap_t[0].shape[0]
    each = lambda f, *ls: [f(*xs) for xs in zip(*ls)]

    rhs = each(lambda x, y: jnp.concatenate([x, y], axis=0), b_t, k_t)
    gk = each(_bdot_nt, kap_t, rhs)
    gr = each(_bdot_nt, r_t, rhs)
    n1 = each(lambda x: jnp.where(tri_strict, x[:, :c], 0.0), gk)
    n2 = each(lambda x: jnp.where(tri_strict, x[:, c:], 0.0), gk)
    m1 = each(lambda x: jnp.where(tri_incl > 0, x[:, :c], 0.0), gr)
    m2 = each(lambda x: jnp.where(tri_incl > 0, x[:, c:], 0.0), gr)

    x = each(lambda n: -n, n1)
    tinv = each(lambda x_: eye + x_, x)
    for _ in range(n_doubling):
        x = each(_bdot, x, x)
        tinv = each(lambda t_, x_: t_ + _bdot(t_, x_), tinv, x)

    nv = each(lambda n, m_, v_: _bdot(jnp.concatenate([n, m_], axis=0), v_), n2, m2, v)
    ty = each(lambda t_, kp, nv_: _bdot(t_, jnp.concatenate([kp, nv_[:c]], axis=1)),
              tinv, kap_t, nv)
    m1ty = each(_bdot, m1, ty)
    rq = each(lambda r_, m_: r_ - m_[:, :HEAD_A], r_t, m1ty)
    ol = each(lambda nv_, m_: nv_[c:] - m_[:, HEAD_A:], nv, m1ty)
    tb = each(_bdot_tn, ty, b_h)
    vk = each(_bdot_tn, v, k_h)
    gmat = each(lambda tb_: tb_[:HEAD_A], tb)
    hmat = each(lambda vk_, tb_: vk_ - tb_[HEAD_A:], vk, tb)
    return rq, ol, gmat, hmat


def _rwkv_chunk_body(r_ref, k_ref, v_ref, lw_ref, a_ref, g_ref, kkw_ref, kaw_ref, rkw_ref,
                     lnw_ref, lnb_ref, s0_ref, y_ref, sout_ref, s_ref, *, chunk, n_chunks,
                     n_heads, n_valid, group):
    @pl.when(pl.program_id(1) == 0)
    def _():
        s_ref[...] = s0_ref[...]

    c = chunk
    d_a = n_heads * HEAD_A
    ri = lax.broadcasted_iota(jnp.int32, (c, c), 0)
    ci = lax.broadcasted_iota(jnp.int32, (c, c), 1)
    tri_incl = (ri >= ci).astype(F32)
    tri_strict = ri > ci
    eye = (ri == ci).astype(F32)
    bi = lax.broadcasted_iota(jnp.int32, (MXU_DIM, MXU_DIM), 0) // HEAD_A
    bj = lax.broadcasted_iota(jnp.int32, (MXU_DIM, MXU_DIM), 1) // HEAD_A
    ones_blk = (bi == bj).astype(BF16)
    valid = None
    if n_valid < c:
        valid = lax.broadcasted_iota(jnp.int32, (c, d_a), 0) < n_valid
    n_doubling = max(int(math.log2(c)) - 1, 0)
    heads = range(n_heads)
    sls = [slice(h * HEAD_A, (h + 1) * HEAD_A) for h in heads]

    def one_group(ig, carry):
        rows, pre = [], []
        for gi in range(group):
            rws = pl.ds(pl.multiple_of((ig * group + gi) * c, c), c)
            rows.append(rws)
            pre.append(_rwkv_prepare(r_ref[rws, :], k_ref[rws, :], v_ref[rws, :], lw_ref[rws, :],
                                     a_ref[rws, :], kkw_ref[...], kaw_ref[...], valid, tri_incl,
                                     ones_blk))
        chains = lambda name: [p[name][:, sl] for p in pre for sl in sls]
        rq, ol, gmat, hmat = _rwkv_local(
            chains('kap_t'), chains('r_t'), chains('b_t'), chains('k_t'), chains('b_h'),
            chains('k_h'), chains('v'), tri_incl, tri_strict, eye, n_doubling)
        s = [s_ref[h] for h in heads]
        for gi in range(group):
            p = pre[gi]
            at = lambda xs, h: xs[gi * n_heads + h]
            o = jnp.concatenate([_bdot_nt(at(rq, h), s[h]) + at(ol, h) for h in heads], axis=1)
            s = [s[h] * p['e_tot'][:, sls[h]] - _bdot(s[h], at(gmat, h)) + at(hmat, h)
                 for h in heads]
            mu = _segsum(o, ones_blk) * (1.0 / HEAD_A)
            dev = o - mu
            var = _segsum(dev * dev, ones_blk) * (1.0 / HEAD_A)
            on = dev * lax.rsqrt(var + GN_EPS) * lnw_ref[...] + lnb_ref[...]
            bonus = _segsum(p['r'] * p['k2'] * rkw_ref[...], ones_blk) * p['v']
            y_ref[rows[gi], :] = (on + bonus) * g_ref[rows[gi], :]
        for h in heads:
            s_ref[h] = s[h]
        return carry

    lax.fori_loop(0, n_chunks // group, one_group, 0)

    @pl.when(pl.program_id(1) == pl.num_programs(1) - 1)
    def _():
        sout_ref[...] = s_ref[...]


def _rwkv_chunked(r, k, v, lw, a, g, k_k, k_a, r_k, lnx_w, lnx_b, s0, tb, chunk, n_valid):
    b, t, d_a = r.shape
    n_heads = d_a // HEAD_A
    tile = pl.BlockSpec((None, tb, d_a), lambda i, j: (i, j, 0))
    row = pl.BlockSpec((1, d_a), lambda i, j: (0, 0))
    st = pl.BlockSpec((None, n_heads, HEAD_A, HEAD_A), lambda i, j: (i, 0, 0, 0))
    return pl.pallas_call(
        functools.partial(_rwkv_chunk_body, chunk=chunk, n_chunks=tb // chunk,
                          n_heads=n_heads, n_valid=n_valid, group=min(4, tb // chunk)),
        grid=(b, t // tb),
        in_specs=[tile] * 6 + [row] * 5 + [st],
        out_specs=(tile, st),
        out_shape=(jax.ShapeDtypeStruct((b, t, d_a), F32),
                   jax.ShapeDtypeStruct((b, n_heads, HEAD_A, HEAD_A), F32)),
        scratch_shapes=[pltpu.VMEM((n_heads, HEAD_A, HEAD_A), F32)],
        compiler_params=_cparams(("arbitrary", "arbitrary")),
        name="rwkv_chunked",
    )(r, k, v, lw, a, g, k_k.reshape(1, d_a), k_a.reshape(1, d_a), r_k.reshape(1, d_a),
      lnx_w.reshape(1, d_a), lnx_b.reshape(1, d_a), s0)


def _s5_disc_body(lr_ref, li_ref, ldt_ref, bre_ref, bim_ref, ar_ref, ai_ref, bbre_ref, bbim_ref):
    lr, li = lr_ref[...], li_ref[...]
    dt = jnp.exp(ldt_ref[...])
    mag = jnp.exp(lr * dt)
    ar = mag * jnp.cos(li * dt)
    ai = mag * jnp.sin(li * dt)
    den = lr * lr + li * li
    fr = ((ar - 1.0) * lr + ai * li) / den
    fi = (ai * lr - (ar - 1.0) * li) / den
    ar_ref[...] = ar
    ai_ref[...] = ai
    b_re, b_im = bre_ref[...], bim_ref[...]
    bbre_ref[...] = fr[:, None, :] * b_re - fi[:, None, :] * b_im
    bbim_ref[...] = fr[:, None, :] * b_im + fi[:, None, :] * b_re


def _s5_discretise(lam_re, lam_im, log_dt, b_re, b_im):
    g, p = lam_re.shape
    n = b_re.shape[2]
    bt = lambda x: jnp.swapaxes(x, 1, 2)
    return pl.pallas_call(
        _s5_disc_body,
        out_shape=(jax.ShapeDtypeStruct((g, p), F32), jax.ShapeDtypeStruct((g, p), F32),
                   jax.ShapeDtypeStruct((g, n, p), F32), jax.ShapeDtypeStruct((g, n, p), F32)),
        name="s5_discretise",
    )(lam_re, lam_im, log_dt.reshape(g, 1), bt(b_re), bt(b_im))


def _gelu_tanh(x):
    c = math.sqrt(2.0 / math.pi)
    return 0.5 * x * (1.0 + jnp.tanh(c * (x + 0.044715 * (x * x * x))))


def _s5_body(u_ref, h0r_ref, h0i_ref, ar_ref, ai_ref, bdre_ref, bdim_ref, cdre_ref, cdim_ref,
             d_ref, wglu_ref, bglu_ref, y_ref, hre_ref, him_ref,
             xr_ref, xi_ref, apr_ref, api_ref, mr_ref, mi_ref, cr_ref, ci_ref, *, last_row):
    j = pl.program_id(1)
    rows, width = xr_ref.shape
    half_u = u_ref.shape[1] // 2
    half_x = width // 2

    ar, ai = ar_ref[...], ai_ref[...]
    a2r, a2i = ar * ar - ai * ai, 2.0 * ar * ai
    a4r, a4i = a2r * a2r - a2i * a2i, 2.0 * a2r * a2i

    @pl.when(j == 0)
    def _():
        cr_ref[...] = h0r_ref[...]
        ci_ref[...] = h0i_ref[...]
        pr, pi = ar, ai
        for s in range(SUBLANES):
            apr_ref[s:s + 1, :] = pr
            api_ref[s:s + 1, :] = pi
            pr, pi = pr * ar - pi * ai, pr * ai + pi * ar
        sub8 = lax.broadcasted_iota(jnp.int32, (SUBLANES, width), 0)
        for rd, (d, pr, pi) in enumerate(((1, ar, ai), (2, a2r, a2i), (4, a4r, a4i))):
            mr_ref[rd] = jnp.where(sub8 >= d, pr, 0.0)
            mi_ref[rd] = jnp.where(sub8 >= d, pi, 0.0)

    ub = u_ref[...].astype(BF16)
    for hf in range(2):
        us = ub[:, hf * half_u:(hf + 1) * half_u]
        cols = slice(hf * half_x, (hf + 1) * half_x)
        xr_ref[:, cols] = jnp.dot(us, bdre_ref[hf], preferred_element_type=F32)
        xi_ref[:, cols] = jnp.dot(us, bdim_ref[hf], preferred_element_type=F32)

    lane_chunk = 4 * LANES
    grouped = lambda x: x.reshape(rows // SUBLANES, SUBLANES, lane_chunk)
    for c0 in range(0, width, lane_chunk):
        cols = slice(c0, c0 + lane_chunk)
        xr, xi = xr_ref[:, cols], xi_ref[:, cols]
        for rd, d in enumerate((1, 2, 4)):
            pr, pi = mr_ref[rd][:, cols][None], mi_ref[rd][:, cols][None]
            sr = grouped(pltpu.roll(xr, d, axis=0))
            si = grouped(pltpu.roll(xi, d, axis=0))
            xr, xi = ((grouped(xr) + (pr * sr - pi * si)).reshape(rows, lane_chunk),
                      (grouped(xi) + (pr * si + pi * sr)).reshape(rows, lane_chunk))
        xr_ref[:, cols] = xr
        xi_ref[:, cols] = xi

    def group(n, carry):
        cr, ci = carry
        r0 = pl.multiple_of(n * SUBLANES, SUBLANES)
        rs = pl.ds(r0, SUBLANES)
        apr, api = apr_ref[...], api_ref[...]
        hr = xr_ref[rs, :] + apr * cr - api * ci
        hi = xi_ref[rs, :] + apr * ci + api * cr
        xr_ref[rs, :] = hr
        xi_ref[rs, :] = hi
        return hr[SUBLANES - 1:SUBLANES, :], hi[SUBLANES - 1:SUBLANES, :]

    cr, ci = lax.fori_loop(0, rows // SUBLANES, group, (cr_ref[...], ci_ref[...]))
    cr_ref[...] = cr
    ci_ref[...] = ci

    @pl.when(j == pl.num_programs(1) - 1)
    def _():
        hre_ref[...] = xr_ref[last_row:last_row + 1, :]
        him_ref[...] = xi_ref[last_row:last_row + 1, :]

    u = u_ref[...]
    for hf in range(2):
        cols = slice(hf * half_x, (hf + 1) * half_x)
        ucols = slice(hf * half_u, (hf + 1) * half_u)
        y = (jnp.dot(xr_ref[:, cols].astype(BF16), cdre_ref[hf], preferred_element_type=F32)
             - jnp.dot(xi_ref[:, cols].astype(BF16), cdim_ref[hf], preferred_element_type=F32)
             + d_ref[:, ucols] * u[:, ucols])
        y_ref[:, ucols] = _gelu_tanh(y)
    y = y_ref[...]
    z = jnp.dot(y.astype(BF16), wglu_ref[...], preferred_element_type=F32) + bglu_ref[...]
    y_ref[...] = y * _sigmoid(z)


def _block_diag_halves(w_gab, transpose):
    g = w_gab.shape[0]
    hg = g // 2
    eye = jnp.eye(hg, dtype=w_gab.dtype)
    halves = []
    for hf in range(2):
        w = w_gab[hf * hg:(hf + 1) * hg]
        if transpose:
            w = jnp.swapaxes(w, 1, 2)
        a, b = w.shape[1], w.shape[2]
        halves.append(jnp.einsum('gab,gh->gahb', w, eye).reshape(hg * a, hg * b))
    return jnp.stack(halves)


def _s5_mix(u, h0_re, h0_im, ar, ai, bbt_re, bbt_im, c_re, c_im, d, w_glu, b_glu, rows, n_valid):
    b, t, d_b = u.shape
    g, p = ar.shape
    width = g * p
    bd_re = _block_diag_halves(bbt_re, False).astype(BF16)
    bd_im = _block_diag_halves(bbt_im, False).astype(BF16)
    cd_re = _block_diag_halves(c_re, True).astype(BF16)
    cd_im = _block_diag_halves(c_im, True).astype(BF16)
    n_blocks = t // rows
    last_row = (n_valid - 1) - (n_blocks - 1) * rows
    full = lambda x: pl.BlockSpec(x.shape, lambda i, j: (0,) * x.ndim)
    st = pl.BlockSpec((None, 1, width), lambda i, j: (i, 0, 0))
    args = (u, h0_re.reshape(b, 1, width), h0_im.reshape(b, 1, width),
            ar.reshape(1, width), ai.reshape(1, width), bd_re, bd_im, cd_re, cd_im,
            d.reshape(1, d_b), w_glu.astype(BF16), b_glu.reshape(1, d_b))
    return pl.pallas_call(
        functools.partial(_s5_body, last_row=last_row),
        grid=(b, n_blocks),
        in_specs=[pl.BlockSpec((None, rows, d_b), lambda i, j: (i, j, 0)), st, st]
                 + [full(x) for x in args[3:]],
        out_specs=(pl.BlockSpec((None, rows, d_b), lambda i, j: (i, j, 0)), st, st),
        out_shape=(jax.ShapeDtypeStruct((b, t, d_b), F32),
                   jax.ShapeDtypeStruct((b, 1, width), F32),
                   jax.ShapeDtypeStruct((b, 1, width), F32)),
        scratch_shapes=[pltpu.VMEM((rows, width), F32), pltpu.VMEM((rows, width), F32),
                        pltpu.VMEM((SUBLANES, width), F32), pltpu.VMEM((SUBLANES, width), F32),
                        pltpu.VMEM((3, SUBLANES, width), F32), pltpu.VMEM((3, SUBLANES, width), F32),
                        pltpu.VMEM((1, width), F32), pltpu.VMEM((1, width), F32)],
        compiler_params=_cparams(("arbitrary", "arbitrary")),
        name="s5_mix",
    )(*args)


def _lambda_full(lq1, lk1, lq2, lk2):
    s1 = jnp.sum(lq1 * lk1, axis=-1, keepdims=True)
    s2 = jnp.sum(lq2 * lk2, axis=-1, keepdims=True)
    return jnp.exp(s1) - jnp.exp(s2) + LAMBDA_INIT


def _sub_ln(o, subln):
    ms = jnp.mean(o * o, axis=-1, keepdims=True)
    return o * lax.rsqrt(ms + RMS_EPS) * subln * (1.0 - LAMBDA_INIT)


def _attn_prompt_body(q_ref, k_ref, v_ref, lq1_ref, lk1_ref, lq2_ref, lk2_ref, subln_ref,
                      o_ref, kb_ref, vt_ref, m_ref, l_ref, acc_ref, sa_ref, sb_ref, *, tile,
                      n_tiles):
    halves = range(2)
    for c in range(n_tiles):
        rows = slice(c * tile, (c + 1) * tile)
        kb_ref[rows, :] = k_ref[rows, :].astype(BF16)
        vt_ref[c] = v_ref[rows, :].T.astype(BF16)
    lam = _lambda_full(lq1_ref[...], lk1_ref[...], lq2_ref[...], lk2_ref[...])
    s_even, s_odd = sa_ref, sb_ref

    def q_tile(qi, carry):
        q_rows = pl.ds(pl.multiple_of(qi * tile, tile), tile)
        q = q_ref[q_rows, :] * (SCALE_C * math.log2(math.e))
        lane = lax.broadcasted_iota(jnp.int32, q.shape, 1)
        qs = [jnp.where((lane >= HALF_C) == bool(c), q, 0.0).astype(BF16) for c in halves]
        m_ref[...] = jnp.full_like(m_ref, NEG_INF)
        l_ref[...] = jnp.zeros_like(l_ref)
        acc_ref[...] = jnp.zeros_like(acc_ref)

        def scores(kj, slot):
            k0 = kj * tile if isinstance(kj, int) else pl.multiple_of(kj * tile, tile)
            kb = kb_ref[pl.ds(k0, tile), :]
            for c in halves:
                slot[c] = lax.dot_general(kb, qs[c], (((1,), (1,)), ((), ())),
                                          preferred_element_type=F32)

        def kv_step(kj, slot, diagonal):
            st = [slot[c] for c in halves]
            if diagonal:
                krow = lax.broadcasted_iota(jnp.int32, st[0].shape, 0)
                qcol = lax.broadcasted_iota(jnp.int32, st[0].shape, 1)
                st = [jnp.where(krow <= qcol, x, NEG_INF) for x in st]
            m_prev = [m_ref[c] for c in halves]
            m_new = [jnp.maximum(m_prev[c], jnp.max(st[c], axis=0, keepdims=True))
                     for c in halves]
            alpha = [jnp.exp2(m_prev[c] - m_new[c]) for c in halves]
            p = [jnp.exp2(st[c] - m_new[c]) for c in halves]
            vt = vt_ref[kj]
            pv = [jnp.dot(vt, p[c].astype(BF16), preferred_element_type=F32) for c in halves]
            for c in halves:
                l_ref[c] = alpha[c] * l_ref[c] + jnp.sum(p[c], axis=0, keepdims=True)
                acc_ref[c] = alpha[c] * acc_ref[c] + pv[c]
                m_ref[c] = m_new[c]

        scores(0, s_even)

        def two_steps(i, c2):
            scores(2 * i + 1, s_odd)
            kv_step(2 * i, s_even, False)
            scores(2 * i + 2, s_even)
            kv_step(2 * i + 1, s_odd, False)
            return c2

        lax.fori_loop(0, qi // 2, two_steps, 0)

        @pl.when(qi % 2 == 0)
        def _():
            kv_step(qi, s_even, True)

        @pl.when(qi % 2 == 1)
        def _():
            scores(qi, s_odd)
            kv_step(qi - 1, s_even, False)
            kv_step(qi, s_odd, True)

        ot = acc_ref[0] / l_ref[0] - lam * (acc_ref[1] / l_ref[1])
        ms = jnp.mean(ot * ot, axis=0, keepdims=True)
        ot = ot * lax.rsqrt(ms + RMS_EPS) * subln_ref[...] * (1.0 - LAMBDA_INIT)
        o_ref[q_rows, :] = ot.T
        return carry

    lax.fori_loop(0, n_tiles, q_tile, 0)


def _attn_prompt(q, k, v, lq1, lk1, lq2, lk2, subln, tile):
    b, t, d = q.shape
    n_heads = d // HEAD_C
    n_tiles = t // tile
    small = lambda n: pl.BlockSpec((1, n), lambda i, h: (0, 0))
    per_head = pl.BlockSpec((None, t, HEAD_C), lambda i, h: (i, 0, h))
    return pl.pallas_call(
        functools.partial(_attn_prompt_body, tile=tile, n_tiles=n_tiles),
        grid=(b, n_heads),
        in_specs=[per_head, per_head, per_head,
                  small(HALF_C), small(HALF_C), small(HALF_C), small(HALF_C),
                  pl.BlockSpec((HEAD_C, 1), lambda i, h: (0, 0))],
        out_specs=per_head,
        out_shape=jax.ShapeDtypeStruct((b, t, d), F32),
        scratch_shapes=[pltpu.VMEM((t, HEAD_C), BF16), pltpu.VMEM((n_tiles, HEAD_C, tile), BF16),
                        pltpu.VMEM((2, 1, tile), F32), pltpu.VMEM((2, 1, tile), F32),
                        pltpu.VMEM((2, HEAD_C, tile), F32),
                        pltpu.VMEM((2, tile, tile), F32), pltpu.VMEM((2, tile, tile), F32)],
        compiler_params=_cparams(("arbitrary", "arbitrary")),
        name="attn_prompt",
    )(q, k, v, lq1.reshape(1, -1), lk1.reshape(1, -1), lq2.reshape(1, -1), lk2.reshape(1, -1),
      subln.reshape(-1, 1))


def _attn_sample_body(pt_ref, qrep_ref, knew_ref, vnew_ref, *rest, pages_per_step, n_heads,
                      t_new, page):
    kp_refs = rest[:pages_per_step]
    vp_refs = rest[pages_per_step:2 * pages_per_step]
    lq1_ref, lk1_ref, lq2_ref, lk2_ref, subln_ref = rest[2 * pages_per_step:2 * pages_per_step + 5]
    o_ref = rest[2 * pages_per_step + 5]
    qw_ref, m_ref, l_ref, acc_ref = rest[2 * pages_per_step + 6:]
    j = pl.program_id(1)
    pairs = range(n_heads // 2)
    q_rows = qrep_ref.shape[1]

    @pl.when(j == 0)
    def _():
        qr = qrep_ref[...]
        row = lax.broadcasted_iota(jnp.int32, qr.shape, 1)
        lane = lax.broadcasted_iota(jnp.int32, qr.shape, 2)
        qm = jnp.where(lane // HALF_C == row // t_new, qr * SCALE_C, 0.0)
        zero = jnp.zeros((q_rows, HEAD_C), F32)
        for pr in pairs:
            qw_ref[pr] = jnp.concatenate(
                [jnp.concatenate([qm[2 * pr], zero], axis=1),
                 jnp.concatenate([zero, qm[2 * pr + 1]], axis=1)], axis=0).astype(BF16)
        m_ref[...] = jnp.full_like(m_ref, NEG_INF)
        l_ref[...] = jnp.zeros_like(l_ref)
        acc_ref[...] = jnp.zeros_like(acc_ref)

    def pair_rows(ref, pr):
        return jnp.concatenate([ref[pl.ds(2 * pr + i, page, stride=n_heads), :] for i in range(2)],
                               axis=1).astype(BF16)

    def update(s, values):
        m_prev = [m_ref[pr] for pr in pairs]
        m_new = [jnp.maximum(m_prev[pr], jnp.max(s[pr], axis=-1, keepdims=True)) for pr in pairs]
        alpha = [jnp.exp(m_prev[pr] - m_new[pr]) for pr in pairs]
        p = [jnp.exp(s[pr] - m_new[pr]) for pr in pairs]
        pv = []
        for pr in pairs:
            w = p[pr].shape[1] // len(values[pr])
            acc = None
            for i, vb in enumerate(values[pr]):
                d = jnp.dot(p[pr][:, i * w:(i + 1) * w].astype(BF16), vb,
                            preferred_element_type=F32)
                acc = d if acc is None else acc + d
            pv.append(acc)
        for pr in pairs:
            l_ref[pr] = alpha[pr] * l_ref[pr] + jnp.sum(p[pr], axis=-1, keepdims=True)
            acc_ref[pr] = alpha[pr] * acc_ref[pr] + pv[pr]
            m_ref[pr] = m_new[pr]

    s = [jnp.concatenate(
            [lax.dot_general(qw_ref[pr], pair_rows(kp, pr), (((1,), (1,)), ((), ())),
                             preferred_element_type=F32) for kp in kp_refs], axis=1)
         for pr in pairs]
    update(s, [[pair_rows(vp, pr) for vp in vp_refs] for pr in pairs])

    @pl.when(j == pl.num_programs(1) - 1)
    def _():
        both = lambda ref, pr: jnp.concatenate([ref[2 * pr], ref[2 * pr + 1]], axis=1).astype(BF16)
        sn = []
        for pr in pairs:
            x = lax.dot_general(qw_ref[pr], both(knew_ref, pr), (((1,), (1,)), ((), ())),
                                preferred_element_type=F32)
            trow = lax.broadcasted_iota(jnp.int32, x.shape, 0) % t_new
            tcol = lax.broadcasted_iota(jnp.int32, x.shape, 1)
            sn.append(jnp.where(tcol <= trow, x, NEG_INF))
        update(sn, [[both(vnew_ref, pr)] for pr in pairs])
        lam = _lambda_full(lq1_ref[...], lk1_ref[...], lq2_ref[...], lk2_ref[...])
        for pr in pairs:
            on = acc_ref[pr] / l_ref[pr]
            for i in range(2):
                h = 2 * pr + i
                r0 = i * q_rows
                oh = on[r0:r0 + 2 * t_new, i * HEAD_C:(i + 1) * HEAD_C]
                o = oh[0:t_new] - lam * oh[t_new:2 * t_new]
                o_ref[:, h * HEAD_C:(h + 1) * HEAD_C] = _sub_ln(o, subln_ref[...])


def _attn_sample(q, k_new, v_new, cache_k, cache_v, page_table, lq1, lk1, lq2, lk2, subln,
                 pages_per_step):
    db, t_new, d = q.shape
    n_heads = d // HEAD_C
    n_pool, page = cache_k.shape[0], cache_k.shape[1]
    n_pages = page_table.shape[1]
    ck = cache_k.reshape(n_pool, page * n_heads, HEAD_C)
    cv = cache_v.reshape(n_pool, page * n_heads, HEAD_C)
    by_head = lambda x: jnp.swapaxes(x.reshape(db, t_new, n_heads, HEAD_C), 1, 2)
    q_rows = NEW_PAD
    qrep = jnp.tile(by_head(q), (1, 1, q_rows // t_new, 1))
    pad = ((0, 0), (0, 0), (0, NEW_PAD - t_new), (0, 0))
    knew = jnp.pad(by_head(k_new), pad)
    vnew = jnp.pad(by_head(v_new), pad)
    pt = page_table.reshape(-1)

    def page_spec(i):
        return pl.BlockSpec((None, page * n_heads, HEAD_C),
                            lambda b, j, pt_ref: (pt_ref[b * n_pages + j * pages_per_step + i], 0, 0))

    small = lambda n: pl.BlockSpec((1, n), lambda b, j, pt_ref: (0, 0))
    per_b = lambda r: pl.BlockSpec((None, n_heads, r, HEAD_C), lambda b, j, pt_ref: (b, 0, 0, 0))
    grid_spec = pltpu.PrefetchScalarGridSpec(
        num_scalar_prefetch=1,
        grid=(db, n_pages // pages_per_step),
        in_specs=[per_b(q_rows), per_b(NEW_PAD), per_b(NEW_PAD)]
                 + [page_spec(i) for i in range(pages_per_step)] * 2
                 + [small(HALF_C)] * 4 + [small(HEAD_C)],
        out_specs=pl.BlockSpec((None, t_new, d), lambda b, j, pt_ref: (b, 0, 0)),
        scratch_shapes=[pltpu.VMEM((n_heads // 2, 2 * q_rows, 2 * HEAD_C), BF16),
                        pltpu.VMEM((n_heads // 2, 2 * q_rows, 1), F32),
                        pltpu.VMEM((n_heads // 2, 2 * q_rows, 1), F32),
                        pltpu.VMEM((n_heads // 2, 2 * q_rows, 2 * HEAD_C), F32)])
    return pl.pallas_call(
        functools.partial(_attn_sample_body, pages_per_step=pages_per_step, n_heads=n_heads,
                          t_new=t_new, page=page),
        grid_spec=grid_spec,
        out_shape=jax.ShapeDtypeStruct((db, t_new, d), F32),
        compiler_params=_cparams(("arbitrary", "arbitrary")),
        name="attn_sample",
    )(pt, qrep, knew, vnew, *([ck] * pages_per_step), *([cv] * pages_per_step),
      lq1.reshape(1, -1), lk1.reshape(1, -1), lq2.reshape(1, -1), lk2.reshape(1, -1),
      subln.reshape(1, -1))


def _pad_time(x, t_pad):
    return jnp.pad(x, ((0, 0), (0, t_pad - x.shape[1]), (0, 0)))


def _trunk(x, mods, shift0, wkv0, sre0, sim0, attend, W, per_row_mod):
    b, t, d = x.shape
    m = b * t
    rows_mod = m if per_row_mod else t
    tm = min(512, rows_mod)
    tm_mlp = min(1024, rows_mod)
    d_a = W['rwkv_w0'].shape[0]
    d_ap = W['rwkv_mu'].shape[0]

    def mod_arg(v):
        if per_row_mod:
            return jnp.repeat(v, t, axis=0)
        return v.reshape(b, 1, d)

    x2 = x.reshape(m, d)

    sh_m, sc_m, g_m, sh_f, sc_f, g_f = (mod_arg(v) for v in mods[0])
    za2, u2 = _normmod_mm(x2, W['norm_mix'][0], sc_m, sh_m, W['w_in_ab_bf16'],
                          ((0, d_ap), (d_ap, W['w_in_ab_bf16'].shape[1])), tm, t)
    za = za2.reshape(b, t, d_ap)
    u = u2.reshape(b, t, -1)
    shift1 = za[:, -1]

    chunk = RWKV_CHUNK if t >= RWKV_CHUNK else RWKV_CHUNK_SHORT
    t_pad = -(-t // chunk) * chunk
    if t_pad != t:
        za_p, u_p = _pad_time(za, t_pad), _pad_time(u, t_pad)
    else:
        za_p, u_p = za, u
    tb = min(512, t_pad)
    r, k, v, lw, a, g = _rwkv_prep(za_p, shift0, W['rwkv_mu'], W['rwkv_w0'], W['rwkv_w_up'],
                                   W['rwkv_a0'], W['rwkv_a_up'], W['rwkv_g_up'], tb)
    y_a, wkv1 = _rwkv_chunked(r, k, v, lw, a, g, W['rwkv_k_k'], W['rwkv_k_a'], W['rwkv_r_k'],
                              W['rwkv_lnx_w'], W['rwkv_lnx_b'], wkv0, tb, chunk,
                              chunk if t_pad == t else t)
    s5_rows = min(256, t_pad)
    y_b, sre1, sim1 = _s5_mix(u_p, sre0, sim0, W['s5_ar'], W['s5_ai'], W['s5_bbt_re'],
                              W['s5_bbt_im'], W['s5_c_re'], W['s5_c_im'], W['s5_d'],
                              W['s5_w_glu'], W['s5_b_glu'], s5_rows, t)
    y_a2 = y_a[:, :t].reshape(m, d_a)
    y_b2 = y_b[:, :t].reshape(m, -1)
    wo = W['w_out_bf16'][0]
    x2 = _mix_mlp(x2, g_m, (y_a2, y_b2), (wo[:d_a], wo[d_a:]), W['norm_mlp'][0], sc_f, sh_f, g_f,
                  W['w_up_bf16'][0], W['w_down_bf16'][0], W['norm_f'], tm_mlp, 1024, t, False)

    sh_m, sc_m, g_m, sh_f, sc_f, g_f = (mod_arg(v) for v in mods[1])
    q2, k2, v2 = _normmod_mm(x2, W['norm_mix'][1], sc_m, sh_m, W['diff_w_qkv_bf16'],
                             ((0, d), (d, 2 * d), (2 * d, 3 * d)), tm, t)
    q3, k3, v3 = (z.reshape(b, t, d) for z in (q2, k2, v2))
    o = attend(q3, k3, v3)
    y2 = _mix_mlp(x2, g_m, (o.reshape(m, d),), (W['w_out_bf16'][1],), W['norm_mlp'][1], sc_f, sh_f,
                  g_f, W['w_up_bf16'][1], W['w_down_bf16'][1], W['norm_f'], tm_mlp, 1024, t, True)

    n_heads_c = d // HEAD_C
    g_b, p_b = W['s5_ar'].shape
    return (y2.reshape(b, t, d), shift1, wkv1, sre1.reshape(b, g_b, p_b), sim1.reshape(b, g_b, p_b),
            k3.reshape(b, t, n_heads_c, HEAD_C), v3.reshape(b, t, n_heads_c, HEAD_C))


def kernel(x_prompt, x_sample, state_shift, state_wkv, state_ssm_re, state_ssm_im, cache_k, cache_v, page_table, c_prompt, c_sample, norm_mix, norm_mlp, norm_f, w_ada, b_ada, w_out, w_up, w_down, w_in_ab, rwkv_mu, rwkv_w0, rwkv_w_up, rwkv_a0, rwkv_a_up, rwkv_g_up, rwkv_k_k, rwkv_k_a, rwkv_r_k, rwkv_lnx_w, rwkv_lnx_b, s5_lam_re, s5_lam_im, s5_log_dt, s5_b_re, s5_b_im, s5_c_re, s5_c_im, s5_d, s5_w_glu, s5_b_glu, diff_w_qkv, diff_lq1, diff_lk1, diff_lq2, diff_lk2, diff_subln):
    bp, tp, d = x_prompt.shape
    db, ts, _ = x_sample.shape
    depth = w_ada.shape[0]
    n_heads_a = rwkv_r_k.shape[0]
    g_b, p_b = s5_lam_re.shape

    ar, ai, bbt_re, bbt_im = _s5_discretise(s5_lam_re, s5_lam_im, s5_log_dt, s5_b_re, s5_b_im)
    W = dict(norm_mix=norm_mix, norm_mlp=norm_mlp, norm_f=norm_f,
             w_out_bf16=w_out.astype(BF16), w_up_bf16=w_up.astype(BF16),
             w_down_bf16=w_down.astype(BF16), w_in_ab_bf16=w_in_ab.astype(BF16),
             diff_w_qkv_bf16=diff_w_qkv.astype(BF16),
             rwkv_mu=rwkv_mu, rwkv_w0=rwkv_w0, rwkv_w_up=rwkv_w_up, rwkv_a0=rwkv_a0,
             rwkv_a_up=rwkv_a_up, rwkv_g_up=rwkv_g_up, rwkv_k_k=rwkv_k_k, rwkv_k_a=rwkv_k_a,
             rwkv_r_k=rwkv_r_k.reshape(-1), rwkv_lnx_w=rwkv_lnx_w, rwkv_lnx_b=rwkv_lnx_b,
             s5_ar=ar, s5_ai=ai, s5_bbt_re=bbt_re, s5_bbt_im=bbt_im,
             s5_c_re=s5_c_re, s5_c_im=s5_c_im, s5_d=s5_d.reshape(-1), s5_w_glu=s5_w_glu,
             s5_b_glu=s5_b_glu)

    n_c = bp + db
    rows_c = -(-n_c // SUBLANES) * SUBLANES
    c_all = jnp.pad(jnp.concatenate([c_prompt, c_sample], axis=0), ((0, rows_c - n_c), (0, 0)))
    mod = _ada_mod(c_all, w_ada, b_ada)
    mods_p = [tuple(mod[l, :bp, i * d:(i + 1) * d] for i in range(6)) for l in range(depth)]
    mods_s = [tuple(mod[l, bp:n_c, i * d:(i + 1) * d] for i in range(6)) for l in range(depth)]

    lam_args = (diff_lq1, diff_lk1, diff_lq2, diff_lk2, diff_subln)
    attend_p = lambda q, k, v: _attn_prompt(q, k, v, *lam_args, tile=min(512, tp))
    y_prompt, p_shift, p_wkv, p_re, p_im, p_k, p_v = _trunk(
        x_prompt, mods_p,
        jnp.zeros((bp, rwkv_mu.shape[0]), F32),
        jnp.zeros((bp, n_heads_a, HEAD_A, HEAD_A), F32),
        jnp.zeros((bp, g_b, p_b), F32), jnp.zeros((bp, g_b, p_b), F32),
        attend_p, W, per_row_mod=False)

    attend_s = lambda q, k, v: _attn_sample(q, k, v, cache_k, cache_v, page_table, *lam_args,
                                            pages_per_step=8)
    y_sample, s_shift, s_wkv, s_re, s_im, s_k, s_v = _trunk(
        x_sample, mods_s, state_shift, state_wkv, state_ssm_re, state_ssm_im,
        attend_s, W, per_row_mod=True)

    return (y_prompt, y_sample, p_shift, p_wkv, p_re, p_im, p_k, p_v,
            s_shift, s_wkv, s_re, s_im, s_k, s_v)
```

```python
import functools
import math

import jax
import jax.numpy as jnp
from jax import lax
from jax.experimental import pallas as pl
from jax.experimental.pallas import tpu as pltpu

F32 = jnp.float32
BF16 = jnp.bfloat16
HIGHEST = lax.Precision.HIGHEST

HEAD_A = 64
LORA_W, LORA_A, LORA_G = 64, 64, 128
S5_GROUP = 16
S5_STATE = 64
HEAD_C = 128
HALF_C = HEAD_C // 2
SCALE_C = HALF_C ** -0.5
RMS_EPS = 1e-6
GN_EPS = 64e-5
DECAY_SCALE = math.exp(-0.5)
LAMBDA_INIT = 0.8 - 0.6 * math.exp(-0.3 * 1)
NEG_INF = -1e30

SUBLANES = 8
LANES = 128
MXU_DIM = 256
VMEM_LIMIT = 56 * 1024 * 1024

RWKV_CHUNK = 64
RWKV_CHUNK_SHORT = 16
NEW_PAD = 16


def _cparams(sem):
    return pltpu.CompilerParams(dimension_semantics=sem, vmem_limit_bytes=VMEM_LIMIT)


def _bdot(a, b):
    return jnp.dot(a.astype(BF16), b.astype(BF16), preferred_element_type=F32)


def _bdot_nt(a, b):
    return lax.dot_general(a.astype(BF16), b.astype(BF16), (((1,), (1,)), ((), ())),
                           preferred_element_type=F32)


def _bdot_tn(a, b):
    return lax.dot_general(a.astype(BF16), b.astype(BF16), (((0,), (0,)), ((), ())),
                           preferred_element_type=F32)


def _sigmoid(x):
    return 1.0 / (1.0 + jnp.exp(-x))


def _ada_body(c_ref, w_ref, b_ref, o_ref):
    c = c_ref[...]
    cond = c * _sigmoid(c)
    o_ref[...] = jnp.dot(cond, w_ref[...], precision=HIGHEST,
                         preferred_element_type=F32) + b_ref[...]


def _ada_mod(c, w_ada, b_ada):
    depth, d, n = w_ada.shape
    rows = c.shape[0]
    tn = 1536
    return pl.pallas_call(
        _ada_body,
        grid=(depth, n // tn),
        in_specs=[pl.BlockSpec((rows, d), lambda l, j: (0, 0)),
                  pl.BlockSpec((None, d, tn), lambda l, j: (l, 0, j)),
                  pl.BlockSpec((None, 1, tn), lambda l, j: (l, 0, j))],
        out_specs=pl.BlockSpec((None, rows, tn), lambda l, j: (l, 0, j)),
        out_shape=jax.ShapeDtypeStruct((depth, rows, n), F32),
        compiler_params=_cparams(("arbitrary", "arbitrary")),
        name="ada_mod",
    )(c, w_ada, b_ada.reshape(depth, 1, n))


def _mod_spec(mod, tm, rows_per_batch, d):
    if mod.ndim == 3:
        tiles = rows_per_batch // tm
        return pl.BlockSpec((None, 1, d), lambda i, *_: (i // tiles, 0, 0))
    return pl.BlockSpec((tm, d), lambda i, *_: (i, 0))


def _norm_mod(x, g, sc, sh):
    ms = jnp.mean(x * x, axis=-1, keepdims=True)
    h = (x * lax.rsqrt(ms + RMS_EPS)) * g
    return h * (1.0 + sc) + sh


def _normmod_mm_body(x_ref, g_ref, sc_ref, sh_ref, w_ref, *o_refs, splits, col_chunk):
    hb = _norm_mod(x_ref[...], g_ref[...], sc_ref[...], sh_ref[...]).astype(BF16)
    for o_ref, (c0, c1) in zip(o_refs, splits):
        for s in range(c0, c1, col_chunk):
            e = min(s + col_chunk, c1)
            o_ref[:, s - c0:e - c0] = jnp.dot(hb, w_ref[:, s:e], preferred_element_type=F32)


def _normmod_mm(x2, g, sc, sh, w_bf16, splits, tm, rows_per_batch):
    m, d = x2.shape
    n = w_bf16.shape[1]
    outs = tuple(jax.ShapeDtypeStruct((m, c1 - c0), F32) for c0, c1 in splits)
    return pl.pallas_call(
        functools.partial(_normmod_mm_body, splits=splits, col_chunk=512),
        grid=(m // tm,),
        in_specs=[pl.BlockSpec((tm, d), lambda i: (i, 0)),
                  pl.BlockSpec((1, d), lambda i: (0, 0)),
                  _mod_spec(sc, tm, rows_per_batch, d),
                  _mod_spec(sh, tm, rows_per_batch, d),
                  pl.BlockSpec((d, n), lambda i: (0, 0))],
        out_specs=tuple(pl.BlockSpec((tm, c1 - c0), lambda i: (i, 0)) for c0, c1 in splits),
        out_shape=outs,
        compiler_params=_cparams(("arbitrary",)),
        name="normmod_mm",
    )(x2, g.reshape(1, d), sc, sh, w_bf16)


def _mix_mlp_body(*refs, n_in, final_norm):
    x_ref, gm_ref = refs[0], refs[1]
    y_refs = refs[2:2 + n_in]
    w_refs = refs[2 + n_in:2 + 2 * n_in]
    (g_ref, sc_ref, sh_ref, gate_ref, wu_ref, wd_ref, gf_ref, o_ref,
     hb_ref, acc_ref, x1_ref) = refs[2 + 2 * n_in:]
    f = pl.program_id(1)

    @pl.when(f == 0)
    def _():
        mix = None
        for y_ref, w_ref in zip(y_refs, w_refs):
            p = jnp.dot(y_ref[...].astype(BF16), w_ref[...], preferred_element_type=F32)
            mix = p if mix is None else mix + p
        x1 = x_ref[...] + gm_ref[...] * mix
        x1_ref[...] = x1
        hb_ref[...] = _norm_mod(x1, g_ref[...], sc_ref[...], sh_ref[...]).astype(BF16)
        acc_ref[...] = jnp.zeros_like(acc_ref)

    up = jnp.dot(hb_ref[...], wu_ref[...], preferred_element_type=F32)
    act = jnp.square(jnp.maximum(up, 0.0)).astype(BF16)
    acc_ref[...] += jnp.dot(act, wd_ref[...], preferred_element_type=F32)

    @pl.when(f == pl.num_programs(1) - 1)
    def _():
        xn = x1_ref[...] + gate_ref[...] * acc_ref[...]
        if final_norm:
            ms = jnp.mean(xn * xn, axis=-1, keepdims=True)
            xn = (xn * lax.rsqrt(ms + RMS_EPS)) * gf_ref[...]
        o_ref[...] = xn


def _mix_mlp(x2, gate_m, ys, ws_bf16, g, sc, sh, gate, wu_bf16, wd_bf16, gfinal, tm, tf,
             rows_per_batch, final_norm):
    m, d = x2.shape
    ff = wu_bf16.shape[1]
    mod = lambda a: _mod_spec(a, tm, rows_per_batch, d)
    row = pl.BlockSpec((1, d), lambda i, f: (0, 0))
    in_specs = [pl.BlockSpec((tm, d), lambda i, f: (i, 0)), mod(gate_m)]
    in_specs += [pl.BlockSpec((tm, y.shape[1]), lambda i, f: (i, 0)) for y in ys]
    in_specs += [pl.BlockSpec(w.shape, lambda i, f: (0, 0)) for w in ws_bf16]
    in_specs += [row, mod(sc), mod(sh), mod(gate),
                 pl.BlockSpec((d, tf), lambda i, f: (0, f)),
                 pl.BlockSpec((tf, d), lambda i, f: (f, 0)), row]
    return pl.pallas_call(
        functools.partial(_mix_mlp_body, n_in=len(ys), final_norm=final_norm),
        grid=(m // tm, ff // tf),
        in_specs=in_specs,
        out_specs=pl.BlockSpec((tm, d), lambda i, f: (i, 0)),
        out_shape=jax.ShapeDtypeStruct((m, d), F32),
        scratch_shapes=[pltpu.VMEM((tm, d), BF16), pltpu.VMEM((tm, d), F32),
                        pltpu.VMEM((tm, d), F32)],
        compiler_params=_cparams(("arbitrary", "arbitrary")),
        name="mix_mlp",
    )(x2, gate_m, *ys, *ws_bf16, g.reshape(1, d), sc, sh, gate, wu_bf16, wd_bf16,
      gfinal.reshape(1, d))


def _rwkv_prep_body(za_ref, shift_ref, mu_ref, w0_ref, wup_ref, a0_ref, aup_ref, gup_ref,
                    r_ref, k_ref, v_ref, lw_ref, a_ref, g_ref, carry_ref, *, d_a):
    @pl.when(pl.program_id(1) == 0)
    def _():
        carry_ref[...] = shift_ref[...]

    tm = za_ref.shape[0]

    def shifted(c0, c1):
        za = za_ref[:, c0:c1]
        row = lax.broadcasted_iota(jnp.int32, za.shape, 0)
        prev = jnp.where(row == 0, carry_ref[:, c0:c1], pltpu.roll(za, 1, axis=0))
        return za + mu_ref[:, c0:c1] * (prev - za)

    o_w = 3 * d_a
    for out_ref, base in ((r_ref, 0), (k_ref, d_a), (v_ref, 2 * d_a)):
        for c0 in range(0, d_a, LANES):
            out_ref[:, c0:c0 + LANES] = shifted(base + c0, base + c0 + LANES)
    lora = shifted(o_w, o_w + LORA_W + LORA_A)
    dw, da = lora[:, :LORA_W], lora[:, LORA_W:]
    dg = shifted(o_w + LORA_W + LORA_A, o_w + LORA_W + LORA_A + LORA_G)
    lw_ref[...] = -DECAY_SCALE * _sigmoid(w0_ref[...] + _bdot(jnp.tanh(dw), wup_ref[...]))
    a_ref[...] = _sigmoid(a0_ref[...] + _bdot(da, aup_ref[...]))
    g_ref[...] = _bdot(_sigmoid(dg), gup_ref[...])
    carry_ref[...] = za_ref[tm - 1:tm, :]


def _rwkv_prep(za, shift_prev, mu, w0, w_up, a0, a_up, g_up, tm):
    b, t, dp = za.shape
    d_a = w0.shape[0]
    out = jax.ShapeDtypeStruct((b, t, d_a), F32)
    row = lambda n: pl.BlockSpec((1, n), lambda i, j: (0, 0))
    full = lambda w: pl.BlockSpec(w.shape, lambda i, j: (0, 0))
    tile = pl.BlockSpec((None, tm, d_a), lambda i, j: (i, j, 0))
    return pl.pallas_call(
        functools.partial(_rwkv_prep_body, d_a=d_a),
        grid=(b, t // tm),
        in_specs=[pl.BlockSpec((None, tm, dp), lambda i, j: (i, j, 0)),
                  pl.BlockSpec((None, 1, dp), lambda i, j: (i, 0, 0)),
                  row(dp), row(d_a), full(w_up), row(d_a), full(a_up), full(g_up)],
        out_specs=(tile,) * 6,
        out_shape=(out,) * 6,
        scratch_shapes=[pltpu.VMEM((1, dp), F32)],
        compiler_params=_cparams(("arbitrary", "arbitrary")),
        name="rwkv_prep",
    )(za, shift_prev.reshape(b, 1, dp), mu.reshape(1, dp), w0.reshape(1, d_a), w_up,
      a0.reshape(1, d_a), a_up, g_up)


def _segsum(x, ones_blk):
    hi = x.astype(BF16)
    lo = (x - hi.astype(F32)).astype(BF16)
    w = ones_blk.shape[0]
    parts = [jnp.dot(hi[:, c0:c0 + w], ones_blk, preferred_element_type=F32)
             + jnp.dot(lo[:, c0:c0 + w], ones_blk, preferred_element_type=F32)
             for c0 in range(0, x.shape[1], w)]
    return jnp.concatenate(parts, axis=1)


def _rwkv_prepare(r, k, v, lw, a, kkw, kaw, valid, tri_incl, ones_blk):
    c = r.shape[0]
    kk = k * kkw
    kk = kk / jnp.maximum(jnp.sqrt(_segsum(kk * kk, ones_blk)), 1e-12)
    k2 = k * (1.0 + (a - 1.0) * kaw)
    b = kk * a
    if valid is not None:
        zero = lambda x: jnp.where(valid, x, 0.0)
        lw, kk, k2, b, v, r = (zero(x) for x in (lw, kk, k2, b, v, r))
    p1 = lw.astype(BF16)
    d1 = lw - p1.astype(F32)
    p2 = d1.astype(BF16)
    p3 = (d1 - p2.astype(F32)).astype(BF16)
    tri = tri_incl.astype(BF16)
    cum = sum(jnp.dot(tri, p, preferred_element_type=F32) for p in (p1, p2, p3))
    tot = cum[c - 1:c, :]
    e_in = jnp.exp(cum)
    e_out = jnp.exp(-cum)
    e_end = jnp.exp(tot - cum)
    return dict(r=r, k2=k2, v=v, kap_t=kk * jnp.exp(cum - lw), r_t=r * e_in, b_t=b * e_out,
                k_t=k2 * e_out, b_h=b * e_end, k_h=k2 * e_end, e_tot=jnp.exp(tot))


def _rwkv_local(kap_t, r_t, b_t, k_t, b_h, k_h, v, tri_incl, tri_strict, eye, n_doubling):
    c = kap_t[0].shape[0]
    each = lambda f, *ls: [f(*xs) for xs in zip(*ls)]

    rhs = each(lambda x, y: jnp.concatenate([x, y], axis=0), b_t, k_t)
    gk = each(_bdot_nt, kap_t, rhs)
    gr = each(_bdot_nt, r_t, rhs)
    n1 = each(lambda x: jnp.where(tri_strict, x[:, :c], 0.0), gk)
    n2 = each(lambda x: jnp.where(tri_strict, x[:, c:], 0.0), gk)
    m1 = each(lambda x: jnp.where(tri_incl > 0, x[:, :c], 0.0), gr)
    m2 = each(lambda x: jnp.where(tri_incl > 0, x[:, c:], 0.0), gr)

    x = each(lambda n: -n, n1)
    tinv = each(lambda x_: eye + x_, x)
    for _ in range(n_doubling):
        x = each(_bdot, x, x)
        tinv = each(lambda t_, x_: t_ + _bdot(t_, x_), tinv, x)

    nv = each(lambda n, m_, v_: _bdot(jnp.concatenate([n, m_], axis=0), v_), n2, m2, v)
    ty = each(lambda t_, kp, nv_: _bdot(t_, jnp.concatenate([kp, nv_[:c]], axis=1)),
              tinv, kap_t, nv)
    m1ty = each(_bdot, m1, ty)
    rq = each(lambda r_, m_: r_ - m_[:, :HEAD_A], r_t, m1ty)
    ol = each(lambda nv_, m_: nv_[c:] - m_[:, HEAD_A:], nv, m1ty)
    tb = each(_bdot_tn, ty, b_h)
    vk = each(_bdot_tn, v, k_h)
    gmat = each(lambda tb_: tb_[:HEAD_A], tb)
    hmat = each(lambda vk_, tb_: vk_ - tb_[HEAD_A:], vk, tb)
    return rq, ol, gmat, hmat


def _rwkv_chunk_body(r_ref, k_ref, v_ref, lw_ref, a_ref, g_ref, kkw_ref, kaw_ref, rkw_ref,
                     lnw_ref, lnb_ref, s0_ref, y_ref, sout_ref, s_ref, *, chunk, n_chunks,
                     n_heads, n_valid, group):
    @pl.when(pl.program_id(1) == 0)
    def _():
        s_ref[...] = s0_ref[...]

    c = chunk
    d_a = n_heads * HEAD_A
    ri = lax.broadcasted_iota(jnp.int32, (c, c), 0)
    ci = lax.broadcasted_iota(jnp.int32, (c, c), 1)
    tri_incl = (ri >= ci).astype(F32)
    tri_strict = ri > ci
    eye = (ri == ci).astype(F32)
    bi = lax.broadcasted_iota(jnp.int32, (MXU_DIM, MXU_DIM), 0) // HEAD_A
    bj = lax.broadcasted_iota(jnp.int32, (MXU_DIM, MXU_DIM), 1) // HEAD_A
    ones_blk = (bi == bj).astype(BF16)
    valid = None
    if n_valid < c:
        valid = lax.broadcasted_iota(jnp.int32, (c, d_a), 0) < n_valid
    n_doubling = max(int(math.log2(c)) - 1, 0)
    heads = range(n_heads)
    sls = [slice(h * HEAD_A, (h + 1) * HEAD_A) for h in heads]

    def one_group(ig, carry):
        rows, pre = [], []
        for gi in range(group):
            rws = pl.ds(pl.multiple_of((ig * group + gi) * c, c), c)
            rows.append(rws)
            pre.append(_rwkv_prepare(r_ref[rws, :], k_ref[rws, :], v_ref[rws, :], lw_ref[rws, :],
                                     a_ref[rws, :], kkw_ref[...], kaw_ref[...], valid, tri_incl,
                                     ones_blk))
        chains = lambda name: [p[name][:, sl] for p in pre for sl in sls]
        rq, ol, gmat, hmat = _rwkv_local(
            chains('kap_t'), chains('r_t'), chains('b_t'), chains('k_t'), chains('b_h'),
            chains('k_h'), chains('v'), tri_incl, tri_strict, eye, n_doubling)
        s = [s_ref[h] for h in heads]
        for gi in range(group):
            p = pre[gi]
            at = lambda xs, h: xs[gi * n_heads + h]
            o = jnp.concatenate([_bdot_nt(at(rq, h), s[h]) + at(ol, h) for h in heads], axis=1)
            s = [s[h] * p['e_tot'][:, sls[h]] - _bdot(s[h], at(gmat, h)) + at(hmat, h)
                 for h in heads]
            mu = _segsum(o, ones_blk) * (1.0 / HEAD_A)
            dev = o - mu
            var = _segsum(dev * dev, ones_blk) * (1.0 / HEAD_A)
            on = dev * lax.rsqrt(var + GN_EPS) * lnw_ref[...] + lnb_ref[...]
            bonus = _segsum(p['r'] * p['k2'] * rkw_ref[...], ones_blk) * p['v']
            y_ref[rows[gi], :] = (on + bonus) * g_ref[rows[gi], :]
        for h in heads:
            s_ref[h] = s[h]
        return carry

    lax.fori_loop(0, n_chunks // group, one_group, 0)

    @pl.when(pl.program_id(1) == pl.num_programs(1) - 1)
    def _():
        sout_ref[...] = s_ref[...]


def _rwkv_chunked(r, k, v, lw, a, g, k_k, k_a, r_k, lnx_w, lnx_b, s0, tb, chunk, n_valid):
    b, t, d_a = r.shape
    n_heads = d_a // HEAD_A
    tile = pl.BlockSpec((None, tb, d_a), lambda i, j: (i, j, 0))
    row = pl.BlockSpec((1, d_a), lambda i, j: (0, 0))
    st = pl.BlockSpec((None, n_heads, HEAD_A, HEAD_A), lambda i, j: (i, 0, 0, 0))
    return pl.pallas_call(
        functools.partial(_rwkv_chunk_body, chunk=chunk, n_chunks=tb // chunk,
                          n_heads=n_heads, n_valid=n_valid, group=min(4, tb // chunk)),
        grid=(b, t // tb),
        in_specs=[tile] * 6 + [row] * 5 + [st],
        out_specs=(tile, st),
        out_shape=(jax.ShapeDtypeStruct((b, t, d_a), F32),
                   jax.ShapeDtypeStruct((b, n_heads, HEAD_A, HEAD_A), F32)),
        scratch_shapes=[pltpu.VMEM((n_heads, HEAD_A, HEAD_A), F32)],
        compiler_params=_cparams(("arbitrary", "arbitrary")),
        name="rwkv_chunked",
    )(r, k, v, lw, a, g, k_k.reshape(1, d_a), k_a.reshape(1, d_a), r_k.reshape(1, d_a),
      lnx_w.reshape(1, d_a), lnx_b.reshape(1, d_a), s0)


def _s5_disc_body(lr_ref, li_ref, ldt_ref, bre_ref, bim_ref, ar_ref, ai_ref, bbre_ref, bbim_ref):
    lr, li = lr_ref[...], li_ref[...]
    dt = jnp.exp(ldt_ref[...])
    mag = jnp.exp(lr * dt)
    ar = mag * jnp.cos(li * dt)
    ai = mag * jnp.sin(li * dt)
    den = lr * lr + li * li
    fr = ((ar - 1.0) * lr + ai * li) / den
    fi = (ai * lr - (ar - 1.0) * li) / den
    ar_ref[...] = ar
    ai_ref[...] = ai
    b_re, b_im = bre_ref[...], bim_ref[...]
    bbre_ref[...] = fr[:, None, :] * b_re - fi[:, None, :] * b_im
    bbim_ref[...] = fr[:, None, :] * b_im + fi[:, None, :] * b_re


def _s5_discretise(lam_re, lam_im, log_dt, b_re, b_im):
    g, p = lam_re.shape
    n = b_re.shape[2]
    bt = lambda x: jnp.swapaxes(x, 1, 2)
    return pl.pallas_call(
        _s5_disc_body,
        out_shape=(jax.ShapeDtypeStruct((g, p), F32), jax.ShapeDtypeStruct((g, p), F32),
                   jax.ShapeDtypeStruct((g, n, p), F32), jax.ShapeDtypeStruct((g, n, p), F32)),
        name="s5_discretise",
    )(lam_re, lam_im, log_dt.reshape(g, 1), bt(b_re), bt(b_im))


def _gelu_tanh(x):
    c = math.sqrt(2.0 / math.pi)
    return 0.5 * x * (1.0 + jnp.tanh(c * (x + 0.044715 * (x * x * x))))


def _s5_body(u_ref, h0r_ref, h0i_ref, ar_ref, ai_ref, bdre_ref, bdim_ref, cdre_ref, cdim_ref,
             d_ref, wglu_ref, bglu_ref, y_ref, hre_ref, him_ref,
             xr_ref, xi_ref, apr_ref, api_ref, mr_ref, mi_ref, cr_ref, ci_ref, *, last_row):
    j = pl.program_id(1)
    rows, width = xr_ref.shape
    half_u = u_ref.shape[1] // 2
    half_x = width // 2

    ar, ai = ar_ref[...], ai_ref[...]
    a2r, a2i = ar * ar - ai * ai, 2.0 * ar * ai
    a4r, a4i = a2r * a2r - a2i * a2i, 2.0 * a2r * a2i

    @pl.when(j == 0)
    def _():
        cr_ref[...] = h0r_ref[...]
        ci_ref[...] = h0i_ref[...]
        pr, pi = ar, ai
        for s in range(SUBLANES):
            apr_ref[s:s + 1, :] = pr
            api_ref[s:s + 1, :] = pi
            pr, pi = pr * ar - pi * ai, pr * ai + pi * ar
        sub8 = lax.broadcasted_iota(jnp.int32, (SUBLANES, width), 0)
        for rd, (d, pr, pi) in enumerate(((1, ar, ai), (2, a2r, a2i), (4, a4r, a4i))):
            mr_ref[rd] = jnp.where(sub8 >= d, pr, 0.0)
            mi_ref[rd] = jnp.where(sub8 >= d, pi, 0.0)

    ub = u_ref[...].astype(BF16)
    for hf in range(2):
        us = ub[:, hf * half_u:(hf + 1) * half_u]
        cols = slice(hf * half_x, (hf + 1) * half_x)
        xr_ref[:, cols] = jnp.dot(us, bdre_ref[hf], preferred_element_type=F32)
        xi_ref[:, cols] = jnp.dot(us, bdim_ref[hf], preferred_element_type=F32)

    lane_chunk = 4 * LANES
    grouped = lambda x: x.reshape(rows // SUBLANES, SUBLANES, lane_chunk)
    for c0 in range(0, width, lane_chunk):
        cols = slice(c0, c0 + lane_chunk)
        xr, xi = xr_ref[:, cols], xi_ref[:, cols]
        for rd, d in enumerate((1, 2, 4)):
            pr, pi = mr_ref[rd][:, cols][None], mi_ref[rd][:, cols][None]
            sr = grouped(pltpu.roll(xr, d, axis=0))
            si = grouped(pltpu.roll(xi, d, axis=0))
            xr, xi = ((grouped(xr) + (pr * sr - pi * si)).reshape(rows, lane_chunk),
                      (grouped(xi) + (pr * si + pi * sr)).reshape(rows, lane_chunk))
        xr_ref[:, cols] = xr
        xi_ref[:, cols] = xi

    def group(n, carry):
        cr, ci = carry
        r0 = pl.multiple_of(n * SUBLANES, SUBLANES)
        rs = pl.ds(r0, SUBLANES)
        apr, api = apr_ref[...], api_ref[...]
        hr = xr_ref[rs, :] + apr * cr - api * ci
        hi = xi_ref[rs, :] + apr * ci + api * cr
        xr_ref[rs, :] = hr
        xi_ref[rs, :] = hi
        return hr[SUBLANES - 1:SUBLANES, :], hi[SUBLANES - 1:SUBLANES, :]

    cr, ci = lax.fori_loop(0, rows // SUBLANES, group, (cr_ref[...], ci_ref[...]))
    cr_ref[...] = cr
    ci_ref[...] = ci

    @pl.when(j == pl.num_programs(1) - 1)
    def _():
        hre_ref[...] = xr_ref[last_row:last_row + 1, :]
        him_ref[...] = xi_ref[last_row:last_row + 1, :]

    u = u_ref[...]
    for hf in range(2):
        cols = slice(hf * half_x, (hf + 1) * half_x)
        ucols = slice(hf * half_u, (hf + 1) * half_u)
        y = (jnp.dot(xr_ref[:, cols].astype(BF16), cdre_ref[hf], preferred_element_type=F32)
             - jnp.dot(xi_ref[:, cols].astype(BF16), cdim_ref[hf], preferred_element_type=F32)
             + d_ref[:, ucols] * u[:, ucols])
        y_ref[:, ucols] = _gelu_tanh(y)
    y = y_ref[...]
    z = jnp.dot(y.astype(BF16), wglu_ref[...], preferred_element_type=F32) + bglu_ref[...]
    y_ref[...] = y * _sigmoid(z)


def _block_diag_halves(w_gab, transpose):
    g = w_gab.shape[0]
    hg = g // 2
    eye = jnp.eye(hg, dtype=w_gab.dtype)
    halves = []
    for hf in range(2):
        w = w_gab[hf * hg:(hf + 1) * hg]
        if transpose:
            w = jnp.swapaxes(w, 1, 2)
        a, b = w.shape[1], w.shape[2]
        halves.append(jnp.einsum('gab,gh->gahb', w, eye).reshape(hg * a, hg * b))
    return jnp.stack(halves)


def _s5_mix(u, h0_re, h0_im, ar, ai, bbt_re, bbt_im, c_re, c_im, d, w_glu, b_glu, rows, n_valid):
    b, t, d_b = u.shape
    g, p = ar.shape
    width = g * p
    bd_re = _block_diag_halves(bbt_re, False).astype(BF16)
    bd_im = _block_diag_halves(bbt_im, False).astype(BF16)
    cd_re = _block_diag_halves(c_re, True).astype(BF16)
    cd_im = _block_diag_halves(c_im, True).astype(BF16)
    n_blocks = t // rows
    last_row = (n_valid - 1) - (n_blocks - 1) * rows
    full = lambda x: pl.BlockSpec(x.shape, lambda i, j: (0,) * x.ndim)
    st = pl.BlockSpec((None, 1, width), lambda i, j: (i, 0, 0))
    args = (u, h0_re.reshape(b, 1, width), h0_im.reshape(b, 1, width),
            ar.reshape(1, width), ai.reshape(1, width), bd_re, bd_im, cd_re, cd_im,
            d.reshape(1, d_b), w_glu.astype(BF16), b_glu.reshape(1, d_b))
    return pl.pallas_call(
        functools.partial(_s5_body, last_row=last_row),
        grid=(b, n_blocks),
        in_specs=[pl.BlockSpec((None, rows, d_b), lambda i, j: (i, j, 0)), st, st]
                 + [full(x) for x in args[3:]],
        out_specs=(pl.BlockSpec((None, rows, d_b), lambda i, j: (i, j, 0)), st, st),
        out_shape=(jax.ShapeDtypeStruct((b, t, d_b), F32),
                   jax.ShapeDtypeStruct((b, 1, width), F32),
                   jax.ShapeDtypeStruct((b, 1, width), F32)),
        scratch_shapes=[pltpu.VMEM((rows, width), F32), pltpu.VMEM((rows, width), F32),
                        pltpu.VMEM((SUBLANES, width), F32), pltpu.VMEM((SUBLANES, width), F32),
                        pltpu.VMEM((3, SUBLANES, width), F32), pltpu.VMEM((3, SUBLANES, width), F32),
                        pltpu.VMEM((1, width), F32), pltpu.VMEM((1, width), F32)],
        compiler_params=_cparams(("arbitrary", "arbitrary")),
        name="s5_mix",
    )(*args)


def _lambda_full(lq1, lk1, lq2, lk2):
    s1 = jnp.sum(lq1 * lk1, axis=-1, keepdims=True)
    s2 = jnp.sum(lq2 * lk2, axis=-1, keepdims=True)
    return jnp.exp(s1) - jnp.exp(s2) + LAMBDA_INIT


def _sub_ln(o, subln):
    ms = jnp.mean(o * o, axis=-1, keepdims=True)
    return o * lax.rsqrt(ms + RMS_EPS) * subln * (1.0 - LAMBDA_INIT)


def _attn_prompt_body(q_ref, k_ref, v_ref, lq1_ref, lk1_ref, lq2_ref, lk2_ref, subln_ref,
                      o_ref, kb_ref, vt_ref, m_ref, l_ref, acc_ref, sa_ref, sb_ref, *, tile,
                      n_tiles):
    halves = range(2)
    for c in range(n_tiles):
        rows = slice(c * tile, (c + 1) * tile)
        kb_ref[rows, :] = k_ref[rows, :].astype(BF16)
        vt_ref[c] = v_ref[rows, :].T.astype(BF16)
    lam = _lambda_full(lq1_ref[...], lk1_ref[...], lq2_ref[...], lk2_ref[...])
    s_even, s_odd = sa_ref, sb_ref

    wide = 2 * tile
    all_q = slice(0, wide)
    late_q = slice(tile, wide)

    def q_tile(qj, carry):
        q_rows = pl.ds(pl.multiple_of(qj * wide, wide), wide)
        q = q_ref[q_rows, :] * (SCALE_C * math.log2(math.e))
        lane = lax.broadcasted_iota(jnp.int32, q.shape, 1)
        qs = [jnp.where((lane >= HALF_C) == bool(c), q, 0.0).astype(BF16) for c in halves]
        m_ref[...] = jnp.full_like(m_ref, NEG_INF)
        l_ref[...] = jnp.zeros_like(l_ref)
        acc_ref[...] = jnp.zeros_like(acc_ref)

        def scores(kj, slot, cols):
            k0 = kj * tile if isinstance(kj, int) else pl.multiple_of(kj * tile, tile)
            kb = kb_ref[pl.ds(k0, tile), :]
            for c in halves:
                slot[c, :, cols] = lax.dot_general(kb, qs[c][cols], (((1,), (1,)), ((), ())),
                                                   preferred_element_type=F32)

        def kv_step(kj, slot, diagonal, cols):
            st = [slot[c, :, cols] for c in halves]
            if diagonal:
                krow = lax.broadcasted_iota(jnp.int32, st[0].shape, 0)
                qcol = lax.broadcasted_iota(jnp.int32, st[0].shape, 1)
                st = [jnp.where(krow <= qcol, x, NEG_INF) for x in st]
            m_prev = [m_ref[c, :, cols] for c in halves]
            m_new = [jnp.maximum(m_prev[c], jnp.max(st[c], axis=0, keepdims=True))
                     for c in halves]
            alpha = [jnp.exp2(m_prev[c] - m_new[c]) for c in halves]
            p = [jnp.exp2(st[c] - m_new[c]) for c in halves]
            vt = vt_ref[kj]
            pv = [jnp.dot(vt, p[c].astype(BF16), preferred_element_type=F32) for c in halves]
            for c in halves:
                l_ref[c, :, cols] = alpha[c] * l_ref[c, :, cols] + jnp.sum(p[c], axis=0,
                                                                           keepdims=True)
                acc_ref[c, :, cols] = alpha[c] * acc_ref[c, :, cols] + pv[c]
                m_ref[c, :, cols] = m_new[c]

        scores(0, s_even, all_q)

        def two_steps(i, c2):
            scores(2 * i + 1, s_odd, all_q)
            kv_step(2 * i, s_even, False, all_q)
            scores(2 * i + 2, s_even, all_q)
            kv_step(2 * i + 1, s_odd, False, all_q)
            return c2

        lax.fori_loop(0, qj, two_steps, 0)
        scores(2 * qj + 1, s_odd, late_q)
        kv_step(2 * qj, s_even, True, all_q)
        kv_step(2 * qj + 1, s_odd, True, late_q)

        ot = acc_ref[0] / l_ref[0] - lam * (acc_ref[1] / l_ref[1])
        ms = jnp.mean(ot * ot, axis=0, keepdims=True)
        ot = ot * lax.rsqrt(ms + RMS_EPS) * subln_ref[...] * (1.0 - LAMBDA_INIT)
        o_ref[q_rows, :] = ot.T
        return carry

    lax.fori_loop(0, n_tiles // 2, q_tile, 0)


def _attn_prompt(q, k, v, lq1, lk1, lq2, lk2, subln, tile):
    b, t, d = q.shape
    n_heads = d // HEAD_C
    n_tiles = t // tile
    assert t % (2 * tile) == 0, (t, tile)
    small = lambda n: pl.BlockSpec((1, n), lambda i, h: (0, 0))
    per_head = pl.BlockSpec((None, t, HEAD_C), lambda i, h: (i, 0, h))
    return pl.pallas_call(
        functools.partial(_attn_prompt_body, tile=tile, n_tiles=n_tiles),
        grid=(b, n_heads),
        in_specs=[per_head, per_head, per_head,
                  small(HALF_C), small(HALF_C), small(HALF_C), small(HALF_C),
                  pl.BlockSpec((HEAD_C, 1), lambda i, h: (0, 0))],
        out_specs=per_head,
        out_shape=jax.ShapeDtypeStruct((b, t, d), F32),
        scratch_shapes=[pltpu.VMEM((t, HEAD_C), BF16), pltpu.VMEM((n_tiles, HEAD_C, tile), BF16),
                        pltpu.VMEM((2, 1, 2 * tile), F32), pltpu.VMEM((2, 1, 2 * tile), F32),
                        pltpu.VMEM((2, HEAD_C, 2 * tile), F32),
                        pltpu.VMEM((2, tile, 2 * tile), F32),
                        pltpu.VMEM((2, tile, 2 * tile), F32)],
        compiler_params=_cparams(("arbitrary", "arbitrary")),
        name="attn_prompt",
    )(q, k, v, lq1.reshape(1, -1), lk1.reshape(1, -1), lq2.reshape(1, -1), lk2.reshape(1, -1),
      subln.reshape(-1, 1))


def _attn_sample_body(pt_ref, qrep_ref, knew_ref, vnew_ref, *rest, pages_per_step, n_heads,
                      t_new, page):
    kp_refs = rest[:pages_per_step]
    vp_refs = rest[pages_per_step:2 * pages_per_step]
    lq1_ref, lk1_ref, lq2_ref, lk2_ref, subln_ref = rest[2 * pages_per_step:2 * pages_per_step + 5]
    o_ref = rest[2 * pages_per_step + 5]
    qw_ref, m_ref, l_ref, acc_ref = rest[2 * pages_per_step + 6:]
    j = pl.program_id(1)
    pairs = range(n_heads // 2)
    q_rows = qrep_ref.shape[1]

    @pl.when(j == 0)
    def _():
        qr = qrep_ref[...]
        row = lax.broadcasted_iota(jnp.int32, qr.shape, 1)
        lane = lax.broadcasted_iota(jnp.int32, qr.shape, 2)
        qm = jnp.where(lane // HALF_C == row // t_new, qr * SCALE_C, 0.0)
        zero = jnp.zeros((q_rows, HEAD_C), F32)
        for pr in pairs:
            qw_ref[pr] = jnp.concatenate(
                [jnp.concatenate([qm[2 * pr], zero], axis=1),
                 jnp.concatenate([zero, qm[2 * pr + 1]], axis=1)], axis=0).astype(BF16)
        m_ref[...] = jnp.full_like(m_ref, NEG_INF)
        l_ref[...] = jnp.zeros_like(l_ref)
        acc_ref[...] = jnp.zeros_like(acc_ref)

    def pair_rows(ref, pr):
        return jnp.concatenate([ref[pl.ds(2 * pr + i, page, stride=n_heads), :] for i in range(2)],
                               axis=1).astype(BF16)

    def update(s, values):
        m_prev = [m_ref[pr] for pr in pairs]
        m_new = [jnp.maximum(m_prev[pr], jnp.max(s[pr], axis=-1, keepdims=True)) for pr in pairs]
        alpha = [jnp.exp(m_prev[pr] - m_new[pr]) for pr in pairs]
        p = [jnp.exp(s[pr] - m_new[pr]) for pr in pairs]
        pv = []
        for pr in pairs:
            w = p[pr].shape[1] // len(values[pr])
            acc = None
            for i, vb in enumerate(values[pr]):
                d = jnp.dot(p[pr][:, i * w:(i + 1) * w].astype(BF16), vb,
                            preferred_element_type=F32)
                acc = d if acc is None else acc + d
            pv.append(acc)
        for pr in pairs:
            l_ref[pr] = alpha[pr] * l_ref[pr] + jnp.sum(p[pr], axis=-1, keepdims=True)
            acc_ref[pr] = alpha[pr] * acc_ref[pr] + pv[pr]
            m_ref[pr] = m_new[pr]

    s = [jnp.concatenate(
            [lax.dot_general(qw_ref[pr], pair_rows(kp, pr), (((1,), (1,)), ((), ())),
                             preferred_element_type=F32) for kp in kp_refs], axis=1)
         for pr in pairs]
    update(s, [[pair_rows(vp, pr) for vp in vp_refs] for pr in pairs])

    @pl.when(j == pl.num_programs(1) - 1)
    def _():
        both = lambda ref, pr: jnp.concatenate([ref[2 * pr], ref[2 * pr + 1]], axis=1).astype(BF16)
        sn = []
        for pr in pairs:
            x = lax.dot_general(qw_ref[pr], both(knew_ref, pr), (((1,), (1,)), ((), ())),
                                preferred_element_type=F32)
            trow = lax.broadcasted_iota(jnp.int32, x.shape, 0) % t_new
            tcol = lax.broadcasted_iota(jnp.int32, x.shape, 1)
            sn.append(jnp.where(tcol <= trow, x, NEG_INF))
        update(sn, [[both(vnew_ref, pr)] for pr in pairs])
        lam = _lambda_full(lq1_ref[...], lk1_ref[...], lq2_ref[...], lk2_ref[...])
        for pr in pairs:
            on = acc_ref[pr] / l_ref[pr]
            for i in range(2):
                h = 2 * pr + i
                r0 = i * q_rows
                oh = on[r0:r0 + 2 * t_new, i * HEAD_C:(i + 1) * HEAD_C]
                o = oh[0:t_new] - lam * oh[t_new:2 * t_new]
                o_ref[:, h * HEAD_C:(h + 1) * HEAD_C] = _sub_ln(o, subln_ref[...])


def _attn_sample(q, k_new, v_new, cache_k, cache_v, page_table, lq1, lk1, lq2, lk2, subln,
                 pages_per_step):
    db, t_new, d = q.shape
    n_heads = d // HEAD_C
    n_pool, page = cache_k.shape[0], cache_k.shape[1]
    n_pages = page_table.shape[1]
    ck = cache_k.reshape(n_pool, page * n_heads, HEAD_C)
    cv = cache_v.reshape(n_pool, page * n_heads, HEAD_C)
    by_head = lambda x: jnp.swapaxes(x.reshape(db, t_new, n_heads, HEAD_C), 1, 2)
    q_rows = NEW_PAD
    qrep = jnp.tile(by_head(q), (1, 1, q_rows // t_new, 1))
    pad = ((0, 0), (0, 0), (0, NEW_PAD - t_new), (0, 0))
    knew = jnp.pad(by_head(k_new), pad)
    vnew = jnp.pad(by_head(v_new), pad)
    pt = page_table.reshape(-1)

    def page_spec(i):
        return pl.BlockSpec((None, page * n_heads, HEAD_C),
                            lambda b, j, pt_ref: (pt_ref[b * n_pages + j * pages_per_step + i], 0, 0))

    small = lambda n: pl.BlockSpec((1, n), lambda b, j, pt_ref: (0, 0))
    per_b = lambda r: pl.BlockSpec((None, n_heads, r, HEAD_C), lambda b, j, pt_ref: (b, 0, 0, 0))
    grid_spec = pltpu.PrefetchScalarGridSpec(
        num_scalar_prefetch=1,
        grid=(db, n_pages // pages_per_step),
        in_specs=[per_b(q_rows), per_b(NEW_PAD), per_b(NEW_PAD)]
                 + [page_spec(i) for i in range(pages_per_step)] * 2
                 + [small(HALF_C)] * 4 + [small(HEAD_C)],
        out_specs=pl.BlockSpec((None, t_new, d), lambda b, j, pt_ref: (b, 0, 0)),
        scratch_shapes=[pltpu.VMEM((n_heads // 2, 2 * q_rows, 2 * HEAD_C), BF16),
                        pltpu.VMEM((n_heads // 2, 2 * q_rows, 1), F32),
                        pltpu.VMEM((n_heads // 2, 2 * q_rows, 1), F32),
                        pltpu.VMEM((n_heads // 2, 2 * q_rows, 2 * HEAD_C), F32)])
    return pl.pallas_call(
        functools.partial(_attn_sample_body, pages_per_step=pages_per_step, n_heads=n_heads,
                          t_new=t_new, page=page),
        grid_spec=grid_spec,
        out_shape=jax.ShapeDtypeStruct((db, t_new, d), F32),
        compiler_params=_cparams(("arbitrary", "arbitrary")),
        name="attn_sample",
    )(pt, qrep, knew, vnew, *([ck] * pages_per_step), *([cv] * pages_per_step),
      lq1.reshape(1, -1), lk1.reshape(1, -1), lq2.reshape(1, -1), lk2.reshape(1, -1),
      subln.reshape(1, -1))


def _pad_time(x, t_pad):
    return jnp.pad(x, ((0, 0), (0, t_pad - x.shape[1]), (0, 0)))


def _trunk(x, mods, shift0, wkv0, sre0, sim0, attend, W, per_row_mod):
    b, t, d = x.shape
    m = b * t
    rows_mod = m if per_row_mod else t
    tm = min(512, rows_mod)
    tm_mlp = min(1024, rows_mod)
    d_a = W['rwkv_w0'].shape[0]
    d_ap = W['rwkv_mu'].shape[0]

    def mod_arg(v):
        if per_row_mod:
            return jnp.repeat(v, t, axis=0)
        return v.reshape(b, 1, d)

    x2 = x.reshape(m, d)

    sh_m, sc_m, g_m, sh_f, sc_f, g_f = (mod_arg(v) for v in mods[0])
    za2, u2 = _normmod_mm(x2, W['norm_mix'][0], sc_m, sh_m, W['w_in_ab_bf16'],
                          ((0, d_ap), (d_ap, W['w_in_ab_bf16'].shape[1])), tm, t)
    za = za2.reshape(b, t, d_ap)
    u = u2.reshape(b, t, -1)
    shift1 = za[:, -1]

    chunk = RWKV_CHUNK if t >= RWKV_CHUNK else RWKV_CHUNK_SHORT
    t_pad = -(-t // chunk) * chunk
    if t_pad != t:
        za_p, u_p = _pad_time(za, t_pad), _pad_time(u, t_pad)
    else:
        za_p, u_p = za, u
    tb = min(512, t_pad)
    r, k, v, lw, a, g = _rwkv_prep(za_p, shift0, W['rwkv_mu'], W['rwkv_w0'], W['rwkv_w_up'],
                                   W['rwkv_a0'], W['rwkv_a_up'], W['rwkv_g_up'], tb)
    y_a, wkv1 = _rwkv_chunked(r, k, v, lw, a, g, W['rwkv_k_k'], W['rwkv_k_a'], W['rwkv_r_k'],
                              W['rwkv_lnx_w'], W['rwkv_lnx_b'], wkv0, tb, chunk,
                              chunk if t_pad == t else t)
    s5_rows = min(256, t_pad)
    y_b, sre1, sim1 = _s5_mix(u_p, sre0, sim0, W['s5_ar'], W['s5_ai'], W['s5_bbt_re'],
                              W['s5_bbt_im'], W['s5_c_re'], W['s5_c_im'], W['s5_d'],
                              W['s5_w_glu'], W['s5_b_glu'], s5_rows, t)
    y_a2 = y_a[:, :t].reshape(m, d_a)
    y_b2 = y_b[:, :t].reshape(m, -1)
    wo = W['w_out_bf16'][0]
    x2 = _mix_mlp(x2, g_m, (y_a2, y_b2), (wo[:d_a], wo[d_a:]), W['norm_mlp'][0], sc_f, sh_f, g_f,
                  W['w_up_bf16'][0], W['w_down_bf16'][0], W['norm_f'], tm_mlp, 1024, t, False)

    sh_m, sc_m, g_m, sh_f, sc_f, g_f = (mod_arg(v) for v in mods[1])
    q2, k2, v2 = _normmod_mm(x2, W['norm_mix'][1], sc_m, sh_m, W['diff_w_qkv_bf16'],
                             ((0, d), (d, 2 * d), (2 * d, 3 * d)), tm, t)
    q3, k3, v3 = (z.reshape(b, t, d) for z in (q2, k2, v2))
    o = attend(q3, k3, v3)
    y2 = _mix_mlp(x2, g_m, (o.reshape(m, d),), (W['w_out_bf16'][1],), W['norm_mlp'][1], sc_f, sh_f,
                  g_f, W['w_up_bf16'][1], W['w_down_bf16'][1], W['norm_f'], tm_mlp, 1024, t, True)

    n_heads_c = d // HEAD_C
    g_b, p_b = W['s5_ar'].shape
    return (y2.reshape(b, t, d), shift1, wkv1, sre1.reshape(b, g_b, p_b), sim1.reshape(b, g_b, p_b),
            k3.reshape(b, t, n_heads_c, HEAD_C), v3.reshape(b, t, n_heads_c, HEAD_C))


def kernel(x_prompt, x_sample, state_shift, state_wkv, state_ssm_re, state_ssm_im, cache_k, cache_v, page_table, c_prompt, c_sample, norm_mix, norm_mlp, norm_f, w_ada, b_ada, w_out, w_up, w_down, w_in_ab, rwkv_mu, rwkv_w0, rwkv_w_up, rwkv_a0, rwkv_a_up, rwkv_g_up, rwkv_k_k, rwkv_k_a, rwkv_r_k, rwkv_lnx_w, rwkv_lnx_b, s5_lam_re, s5_lam_im, s5_log_dt, s5_b_re, s5_b_im, s5_c_re, s5_c_im, s5_d, s5_w_glu, s5_b_glu, diff_w_qkv, diff_lq1, diff_lk1, diff_lq2, diff_lk2, diff_subln):
    bp, tp, d = x_prompt.shape
    db, ts, _ = x_sample.shape
    depth = w_ada.shape[0]
    n_heads_a = rwkv_r_k.shape[0]
    g_b, p_b = s5_lam_re.shape

    ar, ai, bbt_re, bbt_im = _s5_discretise(s5_lam_re, s5_lam_im, s5_log_dt, s5_b_re, s5_b_im)
    W = dict(norm_mix=norm_mix, norm_mlp=norm_mlp, norm_f=norm_f,
             w_out_bf16=w_out.astype(BF16), w_up_bf16=w_up.astype(BF16),
             w_down_bf16=w_down.astype(BF16), w_in_ab_bf16=w_in_ab.astype(BF16),
             diff_w_qkv_bf16=diff_w_qkv.astype(BF16),
             rwkv_mu=rwkv_mu, rwkv_w0=rwkv_w0, rwkv_w_up=rwkv_w_up, rwkv_a0=rwkv_a0,
             rwkv_a_up=rwkv_a_up, rwkv_g_up=rwkv_g_up, rwkv_k_k=rwkv_k_k, rwkv_k_a=rwkv_k_a,
             rwkv_r_k=rwkv_r_k.reshape(-1), rwkv_lnx_w=rwkv_lnx_w, rwkv_lnx_b=rwkv_lnx_b,
             s5_ar=ar, s5_ai=ai, s5_bbt_re=bbt_re, s5_bbt_im=bbt_im,
             s5_c_re=s5_c_re, s5_c_im=s5_c_im, s5_d=s5_d.reshape(-1), s5_w_glu=s5_w_glu,
             s5_b_glu=s5_b_glu)

    n_c = bp + db
    rows_c = -(-n_c // SUBLANES) * SUBLANES
    c_all = jnp.pad(jnp.concatenate([c_prompt, c_sample], axis=0), ((0, rows_c - n_c), (0, 0)))
    mod = _ada_mod(c_all, w_ada, b_ada)
    mods_p = [tuple(mod[l, :bp, i * d:(i + 1) * d] for i in range(6)) for l in range(depth)]
    mods_s = [tuple(mod[l, bp:n_c, i * d:(i + 1) * d] for i in range(6)) for l in range(depth)]

    lam_args = (diff_lq1, diff_lk1, diff_lq2, diff_lk2, diff_subln)
    attend_p = lambda q, k, v: _attn_prompt(q, k, v, *lam_args, tile=min(512, tp // 2))
    y_prompt, p_shift, p_wkv, p_re, p_im, p_k, p_v = _trunk(
        x_prompt, mods_p,
        jnp.zeros((bp, rwkv_mu.shape[0]), F32),
        jnp.zeros((bp, n_heads_a, HEAD_A, HEAD_A), F32),
        jnp.zeros((bp, g_b, p_b), F32), jnp.zeros((bp, g_b, p_b), F32),
        attend_p, W, per_row_mod=False)

    attend_s = lambda q, k, v: _attn_sample(q, k, v, cache_k, cache_v, page_table, *lam_args,
                                            pages_per_step=8)
    y_sample, s_shift, s_wkv, s_re, s_im, s_k, s_v = _trunk(
        x_sample, mods_s, state_shift, state_wkv, state_ssm_re, state_ssm_im,
        attend_s, W, per_row_mod=True)

    return (y_prompt, y_sample, p_shift, p_wkv, p_re, p_im, p_k, p_v,
            s_shift, s_wkv, s_re, s_im, s_k, s_v)
```

```python
import functools
import math

import jax
import jax.numpy as jnp
from jax import lax
from jax.experimental import pallas as pl
from jax.experimental.pallas import tpu as pltpu

F32 = jnp.float32
BF16 = jnp.bfloat16
HIGHEST = lax.Precision.HIGHEST

HEAD_A = 64
LORA_W, LORA_A, LORA_G = 64, 64, 128
S5_GROUP = 16
S5_STATE = 64
HEAD_C = 128
HALF_C = HEAD_C // 2
SCALE_C = HALF_C ** -0.5
RMS_EPS = 1e-6
GN_EPS = 64e-5
DECAY_SCALE = math.exp(-0.5)
LAMBDA_INIT = 0.8 - 0.6 * math.exp(-0.3 * 1)
NEG_INF = -1e30

SUBLANES = 8
LANES = 128
MXU_DIM = 256
VMEM_LIMIT = 56 * 1024 * 1024

RWKV_CHUNK = 64
RWKV_CHUNK_SHORT = 16
S5_FOLD = SUBLANES // 2
NEW_PAD = 16


def _cparams(sem):
    return pltpu.CompilerParams(dimension_semantics=sem, vmem_limit_bytes=VMEM_LIMIT)


def _bdot(a, b):
    return jnp.dot(a.astype(BF16), b.astype(BF16), preferred_element_type=F32)


def _bdot_nt(a, b):
    return lax.dot_general(a.astype(BF16), b.astype(BF16), (((1,), (1,)), ((), ())),
                           preferred_element_type=F32)


def _bdot_tn(a, b):
    return lax.dot_general(a.astype(BF16), b.astype(BF16), (((0,), (0,)), ((), ())),
                           preferred_element_type=F32)


def _sigmoid(x):
    return 1.0 / (1.0 + jnp.exp(-x))


def _ada_body(c_ref, w_ref, b_ref, o_ref):
    c = c_ref[...]
    cond = c * _sigmoid(c)
    o_ref[...] = jnp.dot(cond, w_ref[...], precision=HIGHEST,
                         preferred_element_type=F32) + b_ref[...]


def _ada_mod(c, w_ada, b_ada):
    depth, d, n = w_ada.shape
    rows = c.shape[0]
    tn = 1536
    return pl.pallas_call(
        _ada_body,
        grid=(depth, n // tn),
        in_specs=[pl.BlockSpec((rows, d), lambda l, j: (0, 0)),
                  pl.BlockSpec((None, d, tn), lambda l, j: (l, 0, j)),
                  pl.BlockSpec((None, 1, tn), lambda l, j: (l, 0, j))],
        out_specs=pl.BlockSpec((None, rows, tn), lambda l, j: (l, 0, j)),
        out_shape=jax.ShapeDtypeStruct((depth, rows, n), F32),
        compiler_params=_cparams(("arbitrary", "arbitrary")),
        name="ada_mod",
    )(c, w_ada, b_ada.reshape(depth, 1, n))


def _mod_spec(mod, tm, rows_per_batch, d):
    if mod.ndim == 3:
        tiles = rows_per_batch // tm
        return pl.BlockSpec((None, 1, d), lambda i, *_: (i // tiles, 0, 0))
    return pl.BlockSpec((tm, d), lambda i, *_: (i, 0))


def _norm_mod(x, g, sc, sh):
    ms = jnp.mean(x * x, axis=-1, keepdims=True)
    h = (x * lax.rsqrt(ms + RMS_EPS)) * g
    return h * (1.0 + sc) + sh


def _normmod_mm_body(x_ref, g_ref, sc_ref, sh_ref, w_ref, *o_refs, splits, col_chunk):
    hb = _norm_mod(x_ref[...], g_ref[...], sc_ref[...], sh_ref[...]).astype(BF16)
    for o_ref, (c0, c1) in zip(o_refs, splits):
        for s in range(c0, c1, col_chunk):
            e = min(s + col_chunk, c1)
            o_ref[:, s - c0:e - c0] = jnp.dot(hb, w_ref[:, s:e], preferred_element_type=F32)


def _normmod_mm(x2, g, sc, sh, w_bf16, splits, tm, rows_per_batch):
    m, d = x2.shape
    n = w_bf16.shape[1]
    outs = tuple(jax.ShapeDtypeStruct((m, c1 - c0), F32) for c0, c1 in splits)
    return pl.pallas_call(
        functools.partial(_normmod_mm_body, splits=splits, col_chunk=512),
        grid=(m // tm,),
        in_specs=[pl.BlockSpec((tm, d), lambda i: (i, 0)),
                  pl.BlockSpec((1, d), lambda i: (0, 0)),
                  _mod_spec(sc, tm, rows_per_batch, d),
                  _mod_spec(sh, tm, rows_per_batch, d),
                  pl.BlockSpec((d, n), lambda i: (0, 0))],
        out_specs=tuple(pl.BlockSpec((tm, c1 - c0), lambda i: (i, 0)) for c0, c1 in splits),
        out_shape=outs,
        compiler_params=_cparams(("arbitrary",)),
        name="normmod_mm",
    )(x2, g.reshape(1, d), sc, sh, w_bf16)


def _mix_mlp_body(*refs, n_in, final_norm):
    x_ref, gm_ref = refs[0], refs[1]
    y_refs = refs[2:2 + n_in]
    w_refs = refs[2 + n_in:2 + 2 * n_in]
    (g_ref, sc_ref, sh_ref, gate_ref, wu_ref, wd_ref, gf_ref, o_ref,
     hb_ref, acc_ref, x1_ref) = refs[2 + 2 * n_in:]
    f = pl.program_id(1)

    @pl.when(f == 0)
    def _():
        mix = None
        for y_ref, w_ref in zip(y_refs, w_refs):
            p = jnp.dot(y_ref[...].astype(BF16), w_ref[...], preferred_element_type=F32)
            mix = p if mix is None else mix + p
        x1 = x_ref[...] + gm_ref[...] * mix
        x1_ref[...] = x1
        hb_ref[...] = _norm_mod(x1, g_ref[...], sc_ref[...], sh_ref[...]).astype(BF16)
        acc_ref[...] = jnp.zeros_like(acc_ref)

    up = jnp.dot(hb_ref[...], wu_ref[...], preferred_element_type=F32)
    act = jnp.square(jnp.maximum(up, 0.0)).astype(BF16)
    acc_ref[...] += jnp.dot(act, wd_ref[...], preferred_element_type=F32)

    @pl.when(f == pl.num_programs(1) - 1)
    def _():
        xn = x1_ref[...] + gate_ref[...] * acc_ref[...]
        if final_norm:
            ms = jnp.mean(xn * xn, axis=-1, keepdims=True)
            xn = (xn * lax.rsqrt(ms + RMS_EPS)) * gf_ref[...]
        o_ref[...] = xn


def _mix_mlp(x2, gate_m, ys, ws_bf16, g, sc, sh, gate, wu_bf16, wd_bf16, gfinal, tm, tf,
             rows_per_batch, final_norm):
    m, d = x2.shape
    ff = wu_bf16.shape[1]
    mod = lambda a: _mod_spec(a, tm, rows_per_batch, d)
    row = pl.BlockSpec((1, d), lambda i, f: (0, 0))
    in_specs = [pl.BlockSpec((tm, d), lambda i, f: (i, 0)), mod(gate_m)]
    in_specs += [pl.BlockSpec((tm, y.shape[1]), lambda i, f: (i, 0)) for y in ys]
    in_specs += [pl.BlockSpec(w.shape, lambda i, f: (0, 0)) for w in ws_bf16]
    in_specs += [row, mod(sc), mod(sh), mod(gate),
                 pl.BlockSpec((d, tf), lambda i, f: (0, f)),
                 pl.BlockSpec((tf, d), lambda i, f: (f, 0)), row]
    return pl.pallas_call(
        functools.partial(_mix_mlp_body, n_in=len(ys), final_norm=final_norm),
        grid=(m // tm, ff // tf),
        in_specs=in_specs,
        out_specs=pl.BlockSpec((tm, d), lambda i, f: (i, 0)),
        out_shape=jax.ShapeDtypeStruct((m, d), F32),
        scratch_shapes=[pltpu.VMEM((tm, d), BF16), pltpu.VMEM((tm, d), F32),
                        pltpu.VMEM((tm, d), F32)],
        compiler_params=_cparams(("arbitrary", "arbitrary")),
        name="mix_mlp",
    )(x2, gate_m, *ys, *ws_bf16, g.reshape(1, d), sc, sh, gate, wu_bf16, wd_bf16,
      gfinal.reshape(1, d))


def _rwkv_project(za, prev_row, mu, w0, w_up, a0, a_up, g_up, d_a):
    row = lax.broadcasted_iota(jnp.int32, za.shape, 0)
    prev = jnp.where(row == 0, prev_row, pltpu.roll(za, 1, axis=0))
    zs = za + mu * (prev - za)
    o_w = 3 * d_a
    o_a = o_w + LORA_W
    o_g = o_a + LORA_A
    lw = -DECAY_SCALE * _sigmoid(w0 + _bdot(jnp.tanh(zs[:, o_w:o_a]), w_up))
    a = _sigmoid(a0 + _bdot(zs[:, o_a:o_g], a_up))
    g = _bdot(_sigmoid(zs[:, o_g:o_g + LORA_G]), g_up)
    return zs[:, 0:d_a], zs[:, d_a:2 * d_a], zs[:, 2 * d_a:3 * d_a], lw, a, g


def _segsum(x, ones_blk):
    hi = x.astype(BF16)
    lo = (x - hi.astype(F32)).astype(BF16)
    w = ones_blk.shape[0]
    parts = [jnp.dot(hi[:, c0:c0 + w], ones_blk, preferred_element_type=F32)
             + jnp.dot(lo[:, c0:c0 + w], ones_blk, preferred_element_type=F32)
             for c0 in range(0, x.shape[1], w)]
    return jnp.concatenate(parts, axis=1)


def _rwkv_prepare(r, k, v, lw, a, kkw, kaw, valid, tri_incl, ones_blk):
    c = r.shape[0]
    kk = k * kkw
    kk = kk / jnp.maximum(jnp.sqrt(_segsum(kk * kk, ones_blk)), 1e-12)
    k2 = k * (1.0 + (a - 1.0) * kaw)
    b = kk * a
    if valid is not None:
        zero = lambda x: jnp.where(valid, x, 0.0)
        lw, kk, k2, b, v, r = (zero(x) for x in (lw, kk, k2, b, v, r))
    p1 = lw.astype(BF16)
    d1 = lw - p1.astype(F32)
    p2 = d1.astype(BF16)
    p3 = (d1 - p2.astype(F32)).astype(BF16)
    tri = tri_incl.astype(BF16)
    cum = sum(jnp.dot(tri, p, preferred_element_type=F32) for p in (p1, p2, p3))
    tot = cum[c - 1:c, :]
    e_in = jnp.exp(cum)
    e_out = jnp.exp(-cum)
    e_end = jnp.exp(tot - cum)
    return dict(r=r, k2=k2, v=v, kap_t=kk * jnp.exp(cum - lw), r_t=r * e_in, b_t=b * e_out,
                k_t=k2 * e_out, b_h=b * e_end, k_h=k2 * e_end, e_tot=jnp.exp(tot))


def _rwkv_local(kap_t, r_t, b_t, k_t, b_h, k_h, v, tri_incl, tri_strict, eye, n_doubling):
    c = kap_t[0].shape[0]
    each = lambda f, *ls: [f(*xs) for xs in zip(*ls)]

    rhs = each(lambda x, y: jnp.concatenate([x, y], axis=0), b_t, k_t)
    gk = each(_bdot_nt, kap_t, rhs)
    gr = each(_bdot_nt, r_t, rhs)
    n1 = each(lambda x: jnp.where(tri_strict, x[:, :c], 0.0), gk)
    n2 = each(lambda x: jnp.where(tri_strict, x[:, c:], 0.0), gk)
    m1 = each(lambda x: jnp.where(tri_incl > 0, x[:, :c], 0.0), gr)
    m2 = each(lambda x: jnp.where(tri_incl > 0, x[:, c:], 0.0), gr)

    x = each(lambda n: -n, n1)
    tinv = each(lambda x_: eye + x_, x)
    for _ in range(n_doubling):
        x = each(_bdot, x, x)
        tinv = each(lambda t_, x_: t_ + _bdot(t_, x_), tinv, x)

    nv = each(lambda n, m_, v_: _bdot(jnp.concatenate([n, m_], axis=0), v_), n2, m2, v)
    ty = each(lambda t_, kp, nv_: _bdot(t_, jnp.concatenate([kp, nv_[:c]], axis=1)),
              tinv, kap_t, nv)
    m1ty = each(_bdot, m1, ty)
    rq = each(lambda r_, m_: r_ - m_[:, :HEAD_A], r_t, m1ty)
    ol = each(lambda nv_, m_: nv_[c:] - m_[:, HEAD_A:], nv, m1ty)
    tb = each(_bdot_tn, ty, b_h)
    vk = each(_bdot_tn, v, k_h)
    gmat = each(lambda tb_: tb_[:HEAD_A], tb)
    hmat = each(lambda vk_, tb_: vk_ - tb_[HEAD_A:], vk, tb)
    return rq, ol, gmat, hmat


def _rwkv_chunk_body(za_ref, shift_ref, mu_ref, w0_ref, wup_ref, a0_ref, aup_ref, gup_ref,
                     kkw_ref, kaw_ref, rkw_ref, lnw_ref, lnb_ref, s0_ref, y_ref, sout_ref,
                     s_ref, last_ref, *, chunk, n_chunks, n_heads, n_valid, group):
    @pl.when(pl.program_id(1) == 0)
    def _():
        s_ref[...] = s0_ref[...]
        last_ref[...] = shift_ref[...]

    c = chunk
    d_a = n_heads * HEAD_A
    ri = lax.broadcasted_iota(jnp.int32, (c, c), 0)
    ci = lax.broadcasted_iota(jnp.int32, (c, c), 1)
    tri_incl = (ri >= ci).astype(F32)
    tri_strict = ri > ci
    eye = (ri == ci).astype(F32)
    bi = lax.broadcasted_iota(jnp.int32, (MXU_DIM, MXU_DIM), 0) // HEAD_A
    bj = lax.broadcasted_iota(jnp.int32, (MXU_DIM, MXU_DIM), 1) // HEAD_A
    ones_blk = (bi == bj).astype(BF16)
    valid = None
    if n_valid < c:
        valid = lax.broadcasted_iota(jnp.int32, (c, d_a), 0) < n_valid
    n_doubling = max(int(math.log2(c)) - 1, 0)
    heads = range(n_heads)
    sls = [slice(h * HEAD_A, (h + 1) * HEAD_A) for h in heads]

    def one_group(ig, carry):
        rows, pre = [], []
        prev_row = last_ref[...]
        for gi in range(group):
            rws = pl.ds(pl.multiple_of((ig * group + gi) * c, c), c)
            rows.append(rws)
            za = za_ref[rws, :]
            r, k, v, lw, a, g = _rwkv_project(za, prev_row, mu_ref[...], w0_ref[...], wup_ref[...],
                                              a0_ref[...], aup_ref[...], gup_ref[...], d_a)
            prev_row = za[c - 1:c, :]
            pre.append(dict(_rwkv_prepare(r, k, v, lw, a, kkw_ref[...], kaw_ref[...], valid,
                                          tri_incl, ones_blk), g=g))
        last_ref[...] = prev_row
        chains = lambda name: [p[name][:, sl] for p in pre for sl in sls]
        rq, ol, gmat, hmat = _rwkv_local(
            chains('kap_t'), chains('r_t'), chains('b_t'), chains('k_t'), chains('b_h'),
            chains('k_h'), chains('v'), tri_incl, tri_strict, eye, n_doubling)
        s = [s_ref[h] for h in heads]
        for gi in range(group):
            p = pre[gi]
            at = lambda xs, h: xs[gi * n_heads + h]
            o = jnp.concatenate([_bdot_nt(at(rq, h), s[h]) + at(ol, h) for h in heads], axis=1)
            s = [s[h] * p['e_tot'][:, sls[h]] - _bdot(s[h], at(gmat, h)) + at(hmat, h)
                 for h in heads]
            mu = _segsum(o, ones_blk) * (1.0 / HEAD_A)
            dev = o - mu
            var = _segsum(dev * dev, ones_blk) * (1.0 / HEAD_A)
            on = dev * lax.rsqrt(var + GN_EPS) * lnw_ref[...] + lnb_ref[...]
            bonus = _segsum(p['r'] * p['k2'] * rkw_ref[...], ones_blk) * p['v']
            y_ref[rows[gi], :] = (on + bonus) * p['g']
        for h in heads:
            s_ref[h] = s[h]
        return carry

    lax.fori_loop(0, n_chunks // group, one_group, 0)

    @pl.when(pl.program_id(1) == pl.num_programs(1) - 1)
    def _():
        sout_ref[...] = s_ref[...]


def _rwkv_chunked(za, shift_prev, mu, w0, w_up, a0, a_up, g_up, k_k, k_a, r_k, lnx_w, lnx_b, s0,
                  tb, chunk, n_valid):
    b, t, dp = za.shape
    d_a = w0.shape[0]
    n_heads = d_a // HEAD_A
    row = lambda n: pl.BlockSpec((1, n), lambda i, j: (0, 0))
    full = lambda w: pl.BlockSpec(w.shape, lambda i, j: (0, 0))
    st = pl.BlockSpec((None, n_heads, HEAD_A, HEAD_A), lambda i, j: (i, 0, 0, 0))
    return pl.pallas_call(
        functools.partial(_rwkv_chunk_body, chunk=chunk, n_chunks=tb // chunk,
                          n_heads=n_heads, n_valid=n_valid, group=min(4, tb // chunk)),
        grid=(b, t // tb),
        in_specs=[pl.BlockSpec((None, tb, dp), lambda i, j: (i, j, 0)),
                  pl.BlockSpec((None, 1, dp), lambda i, j: (i, 0, 0)),
                  row(dp), row(d_a), full(w_up), row(d_a), full(a_up), full(g_up)]
                 + [row(d_a)] * 5 + [st],
        out_specs=(pl.BlockSpec((None, tb, d_a), lambda i, j: (i, j, 0)), st),
        out_shape=(jax.ShapeDtypeStruct((b, t, d_a), F32),
                   jax.ShapeDtypeStruct((b, n_heads, HEAD_A, HEAD_A), F32)),
        scratch_shapes=[pltpu.VMEM((n_heads, HEAD_A, HEAD_A), F32), pltpu.VMEM((1, dp), F32)],
        compiler_params=_cparams(("arbitrary", "arbitrary")),
        name="rwkv_chunked",
    )(za, shift_prev.reshape(b, 1, dp), mu.reshape(1, dp), w0.reshape(1, d_a), w_up,
      a0.reshape(1, d_a), a_up, g_up, k_k.reshape(1, d_a), k_a.reshape(1, d_a),
      r_k.reshape(1, d_a), lnx_w.reshape(1, d_a), lnx_b.reshape(1, d_a), s0)


def _s5_disc_body(lr_ref, li_ref, ldt_ref, bre_ref, bim_ref, ar_ref, ai_ref, bbre_ref, bbim_ref):
    lr, li = lr_ref[...], li_ref[...]
    dt = jnp.exp(ldt_ref[...])
    mag = jnp.exp(lr * dt)
    ar = mag * jnp.cos(li * dt)
    ai = mag * jnp.sin(li * dt)
    den = lr * lr + li * li
    fr = ((ar - 1.0) * lr + ai * li) / den
    fi = (ai * lr - (ar - 1.0) * li) / den
    ar_ref[...] = ar
    ai_ref[...] = ai
    b_re, b_im = bre_ref[...], bim_ref[...]
    bb_re = fr[:, None, :] * b_re - fi[:, None, :] * b_im
    bb_im = fr[:, None, :] * b_im + fi[:, None, :] * b_re
    pr, pi = jnp.ones_like(ar), jnp.zeros_like(ai)
    for i in range(S5_FOLD):
        bbre_ref[i] = pr[:, None, :] * bb_re - pi[:, None, :] * bb_im
        bbim_ref[i] = pr[:, None, :] * bb_im + pi[:, None, :] * bb_re
        pr, pi = pr * ar - pi * ai, pr * ai + pi * ar


def _s5_discretise(lam_re, lam_im, log_dt, b_re, b_im):
    g, p = lam_re.shape
    n = b_re.shape[2]
    bt = lambda x: jnp.swapaxes(x, 1, 2)
    return pl.pallas_call(
        _s5_disc_body,
        out_shape=(jax.ShapeDtypeStruct((g, p), F32), jax.ShapeDtypeStruct((g, p), F32),
                   jax.ShapeDtypeStruct((S5_FOLD, g, n, p), F32),
                   jax.ShapeDtypeStruct((S5_FOLD, g, n, p), F32)),
        name="s5_discretise",
    )(lam_re, lam_im, log_dt.reshape(g, 1), bt(b_re), bt(b_im))


def _gelu_tanh(x):
    c = math.sqrt(2.0 / math.pi)
    return 0.5 * x * (1.0 + jnp.tanh(c * (x + 0.044715 * (x * x * x))))


def _s5_body(u_ref, h0r_ref, h0i_ref, ar_ref, ai_ref, bdre_ref, bdim_ref, cdre_ref, cdim_ref,
             d_ref, wglu_ref, bglu_ref, y_ref, hre_ref, him_ref,
             xr_ref, xi_ref, apr_ref, api_ref, mr_ref, mi_ref, cr_ref, ci_ref, *, last_row):
    j = pl.program_id(1)
    rows, width = xr_ref.shape
    half_u = u_ref.shape[1] // 2
    half_x = width // 2

    ar, ai = ar_ref[...], ai_ref[...]
    a2r, a2i = ar * ar - ai * ai, 2.0 * ar * ai
    a4r, a4i = a2r * a2r - a2i * a2i, 2.0 * a2r * a2i

    @pl.when(j == 0)
    def _():
        cr_ref[...] = h0r_ref[...]
        ci_ref[...] = h0i_ref[...]
        pr, pi = ar, ai
        for s in range(SUBLANES):
            apr_ref[s:s + 1, :] = pr
            api_ref[s:s + 1, :] = pi
            pr, pi = pr * ar - pi * ai, pr * ai + pi * ar
        sub8 = lax.broadcasted_iota(jnp.int32, (SUBLANES, width), 0)
        mr_ref[...] = jnp.where(sub8 >= S5_FOLD, a4r, 0.0)
        mi_ref[...] = jnp.where(sub8 >= S5_FOLD, a4i, 0.0)

    u = u_ref[...]
    n_parts = bdre_ref.shape[0]
    part_x = width // n_parts
    sub = lax.broadcasted_iota(jnp.int32, (rows, LANES), 0) % SUBLANES
    for pt in range(n_parts):
        up = u[:, pt * LANES:(pt + 1) * LANES]
        lags = [up] + [jnp.where(sub >= i, pltpu.roll(up, i, axis=0), 0.0)
                       for i in range(1, S5_FOLD)]
        lhs = jnp.concatenate(lags, axis=1).astype(BF16)
        cols = slice(pt * part_x, (pt + 1) * part_x)
        xr_ref[:, cols] = jnp.dot(lhs, bdre_ref[pt], preferred_element_type=F32)
        xi_ref[:, cols] = jnp.dot(lhs, bdim_ref[pt], preferred_element_type=F32)

    lane_chunk = 4 * LANES
    grouped = lambda x: x.reshape(rows // SUBLANES, SUBLANES, lane_chunk)
    for c0 in range(0, width, lane_chunk):
        cols = slice(c0, c0 + lane_chunk)
        xr, xi = xr_ref[:, cols], xi_ref[:, cols]
        pr, pi = mr_ref[:, cols][None], mi_ref[:, cols][None]
        sr = grouped(pltpu.roll(xr, S5_FOLD, axis=0))
        si = grouped(pltpu.roll(xi, S5_FOLD, axis=0))
        xr_ref[:, cols] = (grouped(xr) + (pr * sr - pi * si)).reshape(rows, lane_chunk)
        xi_ref[:, cols] = (grouped(xi) + (pr * si + pi * sr)).reshape(rows, lane_chunk)

    def group(n, carry):
        cr, ci = carry
        r0 = pl.multiple_of(n * SUBLANES, SUBLANES)
        rs = pl.ds(r0, SUBLANES)
        apr, api = apr_ref[...], api_ref[...]
        hr = xr_ref[rs, :] + apr * cr - api * ci
        hi = xi_ref[rs, :] + apr * ci + api * cr
        xr_ref[rs, :] = hr
        xi_ref[rs, :] = hi
        return hr[SUBLANES - 1:SUBLANES, :], hi[SUBLANES - 1:SUBLANES, :]

    cr, ci = lax.fori_loop(0, rows // SUBLANES, group, (cr_ref[...], ci_ref[...]))
    cr_ref[...] = cr
    ci_ref[...] = ci

    @pl.when(j == pl.num_programs(1) - 1)
    def _():
        hre_ref[...] = xr_ref[last_row:last_row + 1, :]
        him_ref[...] = xi_ref[last_row:last_row + 1, :]

    for hf in range(2):
        cols = slice(hf * half_x, (hf + 1) * half_x)
        ucols = slice(hf * half_u, (hf + 1) * half_u)
        y = (jnp.dot(xr_ref[:, cols].astype(BF16), cdre_ref[hf], preferred_element_type=F32)
             - jnp.dot(xi_ref[:, cols].astype(BF16), cdim_ref[hf], preferred_element_type=F32)
             + d_ref[:, ucols] * u[:, ucols])
        y_ref[:, ucols] = _gelu_tanh(y)
    y = y_ref[...]
    z = jnp.dot(y.astype(BF16), wglu_ref[...], preferred_element_type=F32) + bglu_ref[...]
    y_ref[...] = y * _sigmoid(z)


def _block_diag_parts(w_gab, transpose, n_parts=2):
    g = w_gab.shape[0]
    hg = g // n_parts
    eye = jnp.eye(hg, dtype=w_gab.dtype)
    parts = []
    for pt in range(n_parts):
        w = w_gab[pt * hg:(pt + 1) * hg]
        if transpose:
            w = jnp.swapaxes(w, 1, 2)
        a, b = w.shape[1], w.shape[2]
        parts.append(jnp.einsum('gab,gh->gahb', w, eye).reshape(hg * a, hg * b))
    return jnp.stack(parts)


def _s5_mix(u, h0_re, h0_im, ar, ai, bbt_re, bbt_im, c_re, c_im, d, w_glu, b_glu, rows, n_valid):
    b, t, d_b = u.shape
    g, p = ar.shape
    width = g * p
    n_parts = d_b // LANES
    lagged = lambda w: jnp.concatenate(
        [_block_diag_parts(w[i], False, n_parts) for i in range(S5_FOLD)], axis=1).astype(BF16)
    bd_re, bd_im = lagged(bbt_re), lagged(bbt_im)
    cd_re = _block_diag_parts(c_re, True).astype(BF16)
    cd_im = _block_diag_parts(c_im, True).astype(BF16)
    n_blocks = t // rows
    last_row = (n_valid - 1) - (n_blocks - 1) * rows
    full = lambda x: pl.BlockSpec(x.shape, lambda i, j: (0,) * x.ndim)
    st = pl.BlockSpec((None, 1, width), lambda i, j: (i, 0, 0))
    args = (u, h0_re.reshape(b, 1, width), h0_im.reshape(b, 1, width),
            ar.reshape(1, width), ai.reshape(1, width), bd_re, bd_im, cd_re, cd_im,
            d.reshape(1, d_b), w_glu.astype(BF16), b_glu.reshape(1, d_b))
    return pl.pallas_call(
        functools.partial(_s5_body, last_row=last_row),
        grid=(b, n_blocks),
        in_specs=[pl.BlockSpec((None, rows, d_b), lambda i, j: (i, j, 0)), st, st]
                 + [full(x) for x in args[3:]],
        out_specs=(pl.BlockSpec((None, rows, d_b), lambda i, j: (i, j, 0)), st, st),
        out_shape=(jax.ShapeDtypeStruct((b, t, d_b), F32),
                   jax.ShapeDtypeStruct((b, 1, width), F32),
                   jax.ShapeDtypeStruct((b, 1, width), F32)),
        scratch_shapes=[pltpu.VMEM((rows, width), F32), pltpu.VMEM((rows, width), F32),
                        pltpu.VMEM((SUBLANES, width), F32), pltpu.VMEM((SUBLANES, width), F32),
                        pltpu.VMEM((SUBLANES, width), F32), pltpu.VMEM((SUBLANES, width), F32),
                        pltpu.VMEM((1, width), F32), pltpu.VMEM((1, width), F32)],
        compiler_params=_cparams(("arbitrary", "arbitrary")),
        name="s5_mix",
    )(*args)


def _lambda_full(lq1, lk1, lq2, lk2):
    s1 = jnp.sum(lq1 * lk1, axis=-1, keepdims=True)
    s2 = jnp.sum(lq2 * lk2, axis=-1, keepdims=True)
    return jnp.exp(s1) - jnp.exp(s2) + LAMBDA_INIT


def _sub_ln(o, subln):
    ms = jnp.mean(o * o, axis=-1, keepdims=True)
    return o * lax.rsqrt(ms + RMS_EPS) * subln * (1.0 - LAMBDA_INIT)


def _attn_prompt_body(q_ref, k_ref, v_ref, lq1_ref, lk1_ref, lq2_ref, lk2_ref, subln_ref,
                      o_ref, kb_ref, vt_ref, m_ref, l_ref, acc_ref, sa_ref, sb_ref, *, tile,
                      n_tiles):
    halves = range(2)
    for c in range(n_tiles):
        rows = slice(c * tile, (c + 1) * tile)
        kb_ref[rows, :] = k_ref[rows, :].astype(BF16)
        vt_ref[c] = v_ref[rows, :].T.astype(BF16)
    lam = _lambda_full(lq1_ref[...], lk1_ref[...], lq2_ref[...], lk2_ref[...])
    s_even, s_odd = sa_ref, sb_ref

    wide = 2 * tile
    all_q = slice(0, wide)
    late_q = slice(tile, wide)

    def q_tile(qj, carry):
        q_rows = pl.ds(pl.multiple_of(qj * wide, wide), wide)
        q = q_ref[q_rows, :] * (SCALE_C * math.log2(math.e))
        lane = lax.broadcasted_iota(jnp.int32, q.shape, 1)
        qs = [jnp.where((lane >= HALF_C) == bool(c), q, 0.0).astype(BF16) for c in halves]
        m_ref[...] = jnp.full_like(m_ref, NEG_INF)
        l_ref[...] = jnp.zeros_like(l_ref)
        acc_ref[...] = jnp.zeros_like(acc_ref)

        def scores(kj, slot, cols):
            k0 = kj * tile if isinstance(kj, int) else pl.multiple_of(kj * tile, tile)
            kb = kb_ref[pl.ds(k0, tile), :]
            for c in halves:
                slot[c, :, cols] = lax.dot_general(kb, qs[c][cols], (((1,), (1,)), ((), ())),
                                                   preferred_element_type=F32)

        def kv_step(kj, slot, diagonal, cols):
            st = [slot[c, :, cols] for c in halves]
            if diagonal:
                krow = lax.broadcasted_iota(jnp.int32, st[0].shape, 0)
                qcol = lax.broadcasted_iota(jnp.int32, st[0].shape, 1)
                st = [jnp.where(krow <= qcol, x, NEG_INF) for x in st]
            m_prev = [m_ref[c, :, cols] for c in halves]
            m_new = [jnp.maximum(m_prev[c], jnp.max(st[c], axis=0, keepdims=True))
                     for c in halves]
            alpha = [jnp.exp2(m_prev[c] - m_new[c]) for c in halves]
            p = [jnp.exp2(st[c] - m_new[c]) for c in halves]
            vt = vt_ref[kj]
            pv = [jnp.dot(vt, p[c].astype(BF16), preferred_element_type=F32) for c in halves]
            for c in halves:
                l_ref[c, :, cols] = alpha[c] * l_ref[c, :, cols] + jnp.sum(p[c], axis=0,
                                                                           keepdims=True)
                acc_ref[c, :, cols] = alpha[c] * acc_ref[c, :, cols] + pv[c]
                m_ref[c, :, cols] = m_new[c]

        scores(0, s_even, all_q)

        def two_steps(i, c2):
            scores(2 * i + 1, s_odd, all_q)
            kv_step(2 * i, s_even, False, all_q)
            scores(2 * i + 2, s_even, all_q)
            kv_step(2 * i + 1, s_odd, False, all_q)
            return c2

        lax.fori_loop(0, qj, two_steps, 0)
        scores(2 * qj + 1, s_odd, late_q)
        kv_step(2 * qj, s_even, True, all_q)
        kv_step(2 * qj + 1, s_odd, True, late_q)

        ot = acc_ref[0] / l_ref[0] - lam * (acc_ref[1] / l_ref[1])
        ms = jnp.mean(ot * ot, axis=0, keepdims=True)
        ot = ot * lax.rsqrt(ms + RMS_EPS) * subln_ref[...] * (1.0 - LAMBDA_INIT)
        o_ref[q_rows, :] = ot.T
        return carry

    lax.fori_loop(0, n_tiles // 2, q_tile, 0)


def _attn_prompt(q, k, v, lq1, lk1, lq2, lk2, subln, tile):
    b, t, d = q.shape
    n_heads = d // HEAD_C
    n_tiles = t // tile
    assert t % (2 * tile) == 0, (t, tile)
    small = lambda n: pl.BlockSpec((1, n), lambda i, h: (0, 0))
    per_head = pl.BlockSpec((None, t, HEAD_C), lambda i, h: (i, 0, h))
    return pl.pallas_call(
        functools.partial(_attn_prompt_body, tile=tile, n_tiles=n_tiles),
        grid=(b, n_heads),
        in_specs=[per_head, per_head, per_head,
                  small(HALF_C), small(HALF_C), small(HALF_C), small(HALF_C),
                  pl.BlockSpec((HEAD_C, 1), lambda i, h: (0, 0))],
        out_specs=per_head,
        out_shape=jax.ShapeDtypeStruct((b, t, d), F32),
        scratch_shapes=[pltpu.VMEM((t, HEAD_C), BF16), pltpu.VMEM((n_tiles, HEAD_C, tile), BF16),
                        pltpu.VMEM((2, 1, 2 * tile), F32), pltpu.VMEM((2, 1, 2 * tile), F32),
                        pltpu.VMEM((2, HEAD_C, 2 * tile), F32),
                        pltpu.VMEM((2, tile, 2 * tile), F32),
                        pltpu.VMEM((2, tile, 2 * tile), F32)],
        compiler_params=_cparams(("arbitrary", "arbitrary")),
        name="attn_prompt",
    )(q, k, v, lq1.reshape(1, -1), lk1.reshape(1, -1), lq2.reshape(1, -1), lk2.reshape(1, -1),
      subln.reshape(-1, 1))


def _attn_sample_body(pt_ref, qrep_ref, knew_ref, vnew_ref, *rest, pages_per_step, n_heads,
                      t_new, page):
    kp_refs = rest[:pages_per_step]
    vp_refs = rest[pages_per_step:2 * pages_per_step]
    lq1_ref, lk1_ref, lq2_ref, lk2_ref, subln_ref = rest[2 * pages_per_step:2 * pages_per_step + 5]
    o_ref = rest[2 * pages_per_step + 5]
    qw_ref, m_ref, l_ref, acc_ref = rest[2 * pages_per_step + 6:]
    j = pl.program_id(1)
    pairs = range(n_heads // 2)
    q_rows = qrep_ref.shape[1]

    @pl.when(j == 0)
    def _():
        qr = qrep_ref[...]
        row = lax.broadcasted_iota(jnp.int32, qr.shape, 1)
        lane = lax.broadcasted_iota(jnp.int32, qr.shape, 2)
        qm = jnp.where(lane // HALF_C == row // t_new, qr * SCALE_C, 0.0)
        zero = jnp.zeros((q_rows, HEAD_C), F32)
        for pr in pairs:
            qw_ref[pr] = jnp.concatenate(
                [jnp.concatenate([qm[2 * pr], zero], axis=1),
                 jnp.concatenate([zero, qm[2 * pr + 1]], axis=1)], axis=0).astype(BF16)
        m_ref[...] = jnp.full_like(m_ref, NEG_INF)
        l_ref[...] = jnp.zeros_like(l_ref)
        acc_ref[...] = jnp.zeros_like(acc_ref)

    def pair_rows(ref, pr):
        return jnp.concatenate([ref[pl.ds(2 * pr + i, page, stride=n_heads), :] for i in range(2)],
                               axis=1).astype(BF16)

    def update(s, values):
        m_prev = [m_ref[pr] for pr in pairs]
        m_new = [jnp.maximum(m_prev[pr], jnp.max(s[pr], axis=-1, keepdims=True)) for pr in pairs]
        alpha = [jnp.exp(m_prev[pr] - m_new[pr]) for pr in pairs]
        p = [jnp.exp(s[pr] - m_new[pr]) for pr in pairs]
        pv = []
        for pr in pairs:
            w = p[pr].shape[1] // len(values[pr])
            acc = None
            for i, vb in enumerate(values[pr]):
                d = jnp.dot(p[pr][:, i * w:(i + 1) * w].astype(BF16), vb,
                            preferred_element_type=F32)
                acc = d if acc is None else acc + d
            pv.append(acc)
        for pr in pairs:
            l_ref[pr] = alpha[pr] * l_ref[pr] + jnp.sum(p[pr], axis=-1, keepdims=True)
            acc_ref[pr] = alpha[pr] * acc_ref[pr] + pv[pr]
            m_ref[pr] = m_new[pr]

    s = [jnp.concatenate(
            [lax.dot_general(qw_ref[pr], pair_rows(kp, pr), (((1,), (1,)), ((), ())),
                             preferred_element_type=F32) for kp in kp_refs], axis=1)
         for pr in pairs]
    update(s, [[pair_rows(vp, pr) for vp in vp_refs] for pr in pairs])

    @pl.when(j == pl.num_programs(1) - 1)
    def _():
        both = lambda ref, pr: jnp.concatenate([ref[2 * pr], ref[2 * pr + 1]], axis=1).astype(BF16)
        sn = []
        for pr in pairs:
            x = lax.dot_general(qw_ref[pr], both(knew_ref, pr), (((1,), (1,)), ((), ())),
                                preferred_element_type=F32)
            trow = lax.broadcasted_iota(jnp.int32, x.shape, 0) % t_new
            tcol = lax.broadcasted_iota(jnp.int32, x.shape, 1)
            sn.append(jnp.where(tcol <= trow, x, NEG_INF))
        update(sn, [[both(vnew_ref, pr)] for pr in pairs])
        lam = _lambda_full(lq1_ref[...], lk1_ref[...], lq2_ref[...], lk2_ref[...])
        for pr in pairs:
            on = acc_ref[pr] / l_ref[pr]
            for i in range(2):
                h = 2 * pr + i
                r0 = i * q_rows
                oh = on[r0:r0 + 2 * t_new, i * HEAD_C:(i + 1) * HEAD_C]
                o = oh[0:t_new] - lam * oh[t_new:2 * t_new]
                o_ref[:, h * HEAD_C:(h + 1) * HEAD_C] = _sub_ln(o, subln_ref[...])


def _attn_sample(q, k_new, v_new, cache_k, cache_v, page_table, lq1, lk1, lq2, lk2, subln,
                 pages_per_step):
    db, t_new, d = q.shape
    n_heads = d // HEAD_C
    n_pool, page = cache_k.shape[0], cache_k.shape[1]
    n_pages = page_table.shape[1]
    ck = cache_k.reshape(n_pool, page * n_heads, HEAD_C)
    cv = cache_v.reshape(n_pool, page * n_heads, HEAD_C)
    by_head = lambda x: jnp.swapaxes(x.reshape(db, t_new, n_heads, HEAD_C), 1, 2)
    q_rows = NEW_PAD
    qrep = jnp.tile(by_head(q), (1, 1, q_rows // t_new, 1))
    pad = ((0, 0), (0, 0), (0, NEW_PAD - t_new), (0, 0))
    knew = jnp.pad(by_head(k_new), pad)
    vnew = jnp.pad(by_head(v_new), pad)
    pt = page_table.reshape(-1)

    def page_spec(i):
        return pl.BlockSpec((None, page * n_heads, HEAD_C),
                            lambda b, j, pt_ref: (pt_ref[b * n_pages + j * pages_per_step + i], 0, 0))

    small = lambda n: pl.BlockSpec((1, n), lambda b, j, pt_ref: (0, 0))
    per_b = lambda r: pl.BlockSpec((None, n_heads, r, HEAD_C), lambda b, j, pt_ref: (b, 0, 0, 0))
    grid_spec = pltpu.PrefetchScalarGridSpec(
        num_scalar_prefetch=1,
        grid=(db, n_pages // pages_per_step),
        in_specs=[per_b(q_rows), per_b(NEW_PAD), per_b(NEW_PAD)]
                 + [page_spec(i) for i in range(pages_per_step)] * 2
                 + [small(HALF_C)] * 4 + [small(HEAD_C)],
        out_specs=pl.BlockSpec((None, t_new, d), lambda b, j, pt_ref: (b, 0, 0)),
        scratch_shapes=[pltpu.VMEM((n_heads // 2, 2 * q_rows, 2 * HEAD_C), BF16),
                        pltpu.VMEM((n_heads // 2, 2 * q_rows, 1), F32),
                        pltpu.VMEM((n_heads // 2, 2 * q_rows, 1), F32),
                        pltpu.VMEM((n_heads // 2, 2 * q_rows, 2 * HEAD_C), F32)])
    return pl.pallas_call(
        functools.partial(_attn_sample_body, pages_per_step=pages_per_step, n_heads=n_heads,
                          t_new=t_new, page=page),
        grid_spec=grid_spec,
        out_shape=jax.ShapeDtypeStruct((db, t_new, d), F32),
        compiler_params=_cparams(("arbitrary", "arbitrary")),
        name="attn_sample",
    )(pt, qrep, knew, vnew, *([ck] * pages_per_step), *([cv] * pages_per_step),
      lq1.reshape(1, -1), lk1.reshape(1, -1), lq2.reshape(1, -1), lk2.reshape(1, -1),
      subln.reshape(1, -1))


def _pad_time(x, t_pad):
    return jnp.pad(x, ((0, 0), (0, t_pad - x.shape[1]), (0, 0)))


def _trunk(x, mods, shift0, wkv0, sre0, sim0, attend, W, per_row_mod):
    b, t, d = x.shape
    m = b * t
    rows_mod = m if per_row_mod else t
    tm = min(512, rows_mod)
    tm_mlp = min(1024, rows_mod)
    d_a = W['rwkv_w0'].shape[0]
    d_ap = W['rwkv_mu'].shape[0]

    def mod_arg(v):
        if per_row_mod:
            return jnp.repeat(v, t, axis=0)
        return v.reshape(b, 1, d)

    x2 = x.reshape(m, d)

    sh_m, sc_m, g_m, sh_f, sc_f, g_f = (mod_arg(v) for v in mods[0])
    za2, u2 = _normmod_mm(x2, W['norm_mix'][0], sc_m, sh_m, W['w_in_ab_bf16'],
                          ((0, d_ap), (d_ap, W['w_in_ab_bf16'].shape[1])), tm, t)
    za = za2.reshape(b, t, d_ap)
    u = u2.reshape(b, t, -1)
    shift1 = za[:, -1]

    chunk = RWKV_CHUNK if t >= RWKV_CHUNK else RWKV_CHUNK_SHORT
    t_pad = -(-t // chunk) * chunk
    if t_pad != t:
        za_p, u_p = _pad_time(za, t_pad), _pad_time(u, t_pad)
    else:
        za_p, u_p = za, u
    tb = min(512, t_pad)
    y_a, wkv1 = _rwkv_chunked(za_p, shift0, W['rwkv_mu'], W['rwkv_w0'], W['rwkv_w_up'],
                              W['rwkv_a0'], W['rwkv_a_up'], W['rwkv_g_up'], W['rwkv_k_k'],
                              W['rwkv_k_a'], W['rwkv_r_k'], W['rwkv_lnx_w'], W['rwkv_lnx_b'],
                              wkv0, tb, chunk, chunk if t_pad == t else t)
    s5_rows = min(256, t_pad)
    y_b, sre1, sim1 = _s5_mix(u_p, sre0, sim0, W['s5_ar'], W['s5_ai'], W['s5_bbt_re'],
                              W['s5_bbt_im'], W['s5_c_re'], W['s5_c_im'], W['s5_d'],
                              W['s5_w_glu'], W['s5_b_glu'], s5_rows, t)
    y_a2 = y_a[:, :t].reshape(m, d_a)
    y_b2 = y_b[:, :t].reshape(m, -1)
    wo = W['w_out_bf16'][0]
    x2 = _mix_mlp(x2, g_m, (y_a2, y_b2), (wo[:d_a], wo[d_a:]), W['norm_mlp'][0], sc_f, sh_f, g_f,
                  W['w_up_bf16'][0], W['w_down_bf16'][0], W['norm_f'], tm_mlp, 1024, t, False)

    sh_m, sc_m, g_m, sh_f, sc_f, g_f = (mod_arg(v) for v in mods[1])
    q2, k2, v2 = _normmod_mm(x2, W['norm_mix'][1], sc_m, sh_m, W['diff_w_qkv_bf16'],
                             ((0, d), (d, 2 * d), (2 * d, 3 * d)), tm, t)
    q3, k3, v3 = (z.reshape(b, t, d) for z in (q2, k2, v2))
    o = attend(q3, k3, v3)
    y2 = _mix_mlp(x2, g_m, (o.reshape(m, d),), (W['w_out_bf16'][1],), W['norm_mlp'][1], sc_f, sh_f,
                  g_f, W['w_up_bf16'][1], W['w_down_bf16'][1], W['norm_f'], tm_mlp, 1024, t, True)

    n_heads_c = d // HEAD_C
    g_b, p_b = W['s5_ar'].shape
    return (y2.reshape(b, t, d), shift1, wkv1, sre1.reshape(b, g_b, p_b), sim1.reshape(b, g_b, p_b),
            k3.reshape(b, t, n_heads_c, HEAD_C), v3.reshape(b, t, n_heads_c, HEAD_C))


def kernel(x_prompt, x_sample, state_shift, state_wkv, state_ssm_re, state_ssm_im, cache_k, cache_v, page_table, c_prompt, c_sample, norm_mix, norm_mlp, norm_f, w_ada, b_ada, w_out, w_up, w_down, w_in_ab, rwkv_mu, rwkv_w0, rwkv_w_up, rwkv_a0, rwkv_a_up, rwkv_g_up, rwkv_k_k, rwkv_k_a, rwkv_r_k, rwkv_lnx_w, rwkv_lnx_b, s5_lam_re, s5_lam_im, s5_log_dt, s5_b_re, s5_b_im, s5_c_re, s5_c_im, s5_d, s5_w_glu, s5_b_glu, diff_w_qkv, diff_lq1, diff_lk1, diff_lq2, diff_lk2, diff_subln):
    bp, tp, d = x_prompt.shape
    db, ts, _ = x_sample.shape
    depth = w_ada.shape[0]
    n_heads_a = rwkv_r_k.shape[0]
    g_b, p_b = s5_lam_re.shape

    ar, ai, bbt_re, bbt_im = _s5_discretise(s5_lam_re, s5_lam_im, s5_log_dt, s5_b_re, s5_b_im)
    W = dict(norm_mix=norm_mix, norm_mlp=norm_mlp, norm_f=norm_f,
             w_out_bf16=w_out.astype(BF16), w_up_bf16=w_up.astype(BF16),
             w_down_bf16=w_down.astype(BF16), w_in_ab_bf16=w_in_ab.astype(BF16),
             diff_w_qkv_bf16=diff_w_qkv.astype(BF16),
             rwkv_mu=rwkv_mu, rwkv_w0=rwkv_w0, rwkv_w_up=rwkv_w_up, rwkv_a0=rwkv_a0,
             rwkv_a_up=rwkv_a_up, rwkv_g_up=rwkv_g_up, rwkv_k_k=rwkv_k_k, rwkv_k_a=rwkv_k_a,
             rwkv_r_k=rwkv_r_k.reshape(-1), rwkv_lnx_w=rwkv_lnx_w, rwkv_lnx_b=rwkv_lnx_b,
             s5_ar=ar, s5_ai=ai, s5_bbt_re=bbt_re, s5_bbt_im=bbt_im,
             s5_c_re=s5_c_re, s5_c_im=s5_c_im, s5_d=s5_d.reshape(-1), s5_w_glu=s5_w_glu,
             s5_b_glu=s5_b_glu)

    n_c = bp + db
    rows_c = -(-n_c // SUBLANES) * SUBLANES
    c_all = jnp.pad(jnp.concatenate([c_prompt, c_sample], axis=0), ((0, rows_c - n_c), (0, 0)))
    mod = _ada_mod(c_all, w_ada, b_ada)
    mods_p = [tuple(mod[l, :bp, i * d:(i + 1) * d] for i in range(6)) for l in range(depth)]
    mods_s = [tuple(mod[l, bp:n_c, i * d:(i + 1) * d] for i in range(6)) for l in range(depth)]

    lam_args = (diff_lq1, diff_lk1, diff_lq2, diff_lk2, diff_subln)
    attend_p = lambda q, k, v: _attn_prompt(q, k, v, *lam_args, tile=min(512, tp // 2))
    y_prompt, p_shift, p_wkv, p_re, p_im, p_k, p_v = _trunk(
        x_prompt, mods_p,
        jnp.zeros((bp, rwkv_mu.shape[0]), F32),
        jnp.zeros((bp, n_heads_a, HEAD_A, HEAD_A), F32),
        jnp.zeros((bp, g_b, p_b), F32), jnp.zeros((bp, g_b, p_b), F32),
        attend_p, W, per_row_mod=False)

    attend_s = lambda q, k, v: _attn_sample(q, k, v, cache_k, cache_v, page_table, *lam_args,
                                            pages_per_step=8)
    y_sample, s_shift, s_wkv, s_re, s_im, s_k, s_v = _trunk(
        x_sample, mods_s, state_shift, state_wkv, state_ssm_re, state_ssm_im,
        attend_s, W, per_row_mod=True)

    return (y_prompt, y_sample, p_shift, p_wkv, p_re, p_im, p_k, p_v,
            s_shift, s_wkv, s_re, s_im, s_k, s_v)
```

```python
import functools
import math

import jax
import jax.numpy as jnp
from jax import lax
from jax.experimental import pallas as pl
from jax.experimental.pallas import tpu as pltpu

F32 = jnp.float32
BF16 = jnp.bfloat16
HIGHEST = lax.Precision.HIGHEST

HEAD_A = 64
LORA_W, LORA_A, LORA_G = 64, 64, 128
S5_GROUP = 16
S5_STATE = 64
HEAD_C = 128
HALF_C = HEAD_C // 2
SCALE_C = HALF_C ** -0.5
RMS_EPS = 1e-6
GN_EPS = 64e-5
DECAY_SCALE = math.exp(-0.5)
LAMBDA_INIT = 0.8 - 0.6 * math.exp(-0.3 * 1)
NEG_INF = -1e30

SUBLANES = 8
LANES = 128
MXU_DIM = 256
VMEM_LIMIT = 56 * 1024 * 1024

RWKV_CHUNK = 64
RWKV_CHUNK_SHORT = 16
RWKV_PARALLEL = 4
S5_FOLD = SUBLANES // 2
NEW_PAD = 16


def _cparams(sem):
    return pltpu.CompilerParams(dimension_semantics=sem, vmem_limit_bytes=VMEM_LIMIT)


def _bdot(a, b):
    return jnp.dot(a.astype(BF16), b.astype(BF16), preferred_element_type=F32)


def _bdot_nt(a, b):
    return lax.dot_general(a.astype(BF16), b.astype(BF16), (((1,), (1,)), ((), ())),
                           preferred_element_type=F32)


def _bdot_tn(a, b):
    return lax.dot_general(a.astype(BF16), b.astype(BF16), (((0,), (0,)), ((), ())),
                           preferred_element_type=F32)


def _sigmoid(x):
    return 1.0 / (1.0 + jnp.exp(-x))


def _ada_body(c_ref, w_ref, b_ref, o_ref):
    c = c_ref[...]
    cond = c * _sigmoid(c)
    o_ref[...] = jnp.dot(cond, w_ref[...], precision=HIGHEST,
                         preferred_element_type=F32) + b_ref[...]


def _ada_mod(c, w_ada, b_ada):
    depth, d, n = w_ada.shape
    rows = c.shape[0]
    tn = 1536
    return pl.pallas_call(
        _ada_body,
        grid=(depth, n // tn),
        in_specs=[pl.BlockSpec((rows, d), lambda l, j: (0, 0)),
                  pl.BlockSpec((None, d, tn), lambda l, j: (l, 0, j)),
                  pl.BlockSpec((None, 1, tn), lambda l, j: (l, 0, j))],
        out_specs=pl.BlockSpec((None, rows, tn), lambda l, j: (l, 0, j)),
        out_shape=jax.ShapeDtypeStruct((depth, rows, n), F32),
        compiler_params=_cparams(("arbitrary", "arbitrary")),
        name="ada_mod",
    )(c, w_ada, b_ada.reshape(depth, 1, n))


def _mod_spec(mod, tm, rows_per_batch, d):
    if mod.ndim == 3:
        tiles = rows_per_batch // tm
        return pl.BlockSpec((None, 1, d), lambda i, *_: (i // tiles, 0, 0))
    return pl.BlockSpec((tm, d), lambda i, *_: (i, 0))


def _norm_mod(x, g, sc, sh):
    ms = jnp.mean(x * x, axis=-1, keepdims=True)
    h = (x * lax.rsqrt(ms + RMS_EPS)) * g
    return h * (1.0 + sc) + sh


def _normmod_mm_body(x_ref, g_ref, sc_ref, sh_ref, w_ref, *o_refs, splits, col_chunk):
    hb = _norm_mod(x_ref[...], g_ref[...], sc_ref[...], sh_ref[...]).astype(BF16)
    for o_ref, (c0, c1) in zip(o_refs, splits):
        for s in range(c0, c1, col_chunk):
            e = min(s + col_chunk, c1)
            o_ref[:, s - c0:e - c0] = jnp.dot(hb, w_ref[:, s:e], preferred_element_type=F32)


def _normmod_mm(x2, g, sc, sh, w_bf16, splits, tm, rows_per_batch):
    m, d = x2.shape
    n = w_bf16.shape[1]
    outs = tuple(jax.ShapeDtypeStruct((m, c1 - c0), F32) for c0, c1 in splits)
    return pl.pallas_call(
        functools.partial(_normmod_mm_body, splits=splits, col_chunk=512),
        grid=(m // tm,),
        in_specs=[pl.BlockSpec((tm, d), lambda i: (i, 0)),
                  pl.BlockSpec((1, d), lambda i: (0, 0)),
                  _mod_spec(sc, tm, rows_per_batch, d),
                  _mod_spec(sh, tm, rows_per_batch, d),
                  pl.BlockSpec((d, n), lambda i: (0, 0))],
        out_specs=tuple(pl.BlockSpec((tm, c1 - c0), lambda i: (i, 0)) for c0, c1 in splits),
        out_shape=outs,
        compiler_params=_cparams(("arbitrary",)),
        name="normmod_mm",
    )(x2, g.reshape(1, d), sc, sh, w_bf16)


def _mix_mlp_body(*refs, n_in, final_norm):
    x_ref, gm_ref = refs[0], refs[1]
    y_refs = refs[2:2 + n_in]
    w_refs = refs[2 + n_in:2 + 2 * n_in]
    (g_ref, sc_ref, sh_ref, gate_ref, wu_ref, wd_ref, gf_ref, o_ref,
     hb_ref, acc_ref, x1_ref) = refs[2 + 2 * n_in:]
    f = pl.program_id(1)

    @pl.when(f == 0)
    def _():
        mix = None
        for y_ref, w_ref in zip(y_refs, w_refs):
            p = jnp.dot(y_ref[...].astype(BF16), w_ref[...], preferred_element_type=F32)
            mix = p if mix is None else mix + p
        x1 = x_ref[...] + gm_ref[...] * mix
        x1_ref[...] = x1
        hb_ref[...] = _norm_mod(x1, g_ref[...], sc_ref[...], sh_ref[...]).astype(BF16)
        acc_ref[...] = jnp.zeros_like(acc_ref)

    up = jnp.dot(hb_ref[...], wu_ref[...], preferred_element_type=F32)
    act = jnp.square(jnp.maximum(up, 0.0)).astype(BF16)
    acc_ref[...] += jnp.dot(act, wd_ref[...], preferred_element_type=F32)

    @pl.when(f == pl.num_programs(1) - 1)
    def _():
        xn = x1_ref[...] + gate_ref[...] * acc_ref[...]
        if final_norm:
            ms = jnp.mean(xn * xn, axis=-1, keepdims=True)
            xn = (xn * lax.rsqrt(ms + RMS_EPS)) * gf_ref[...]
        o_ref[...] = xn


def _mix_mlp(x2, gate_m, ys, ws_bf16, g, sc, sh, gate, wu_bf16, wd_bf16, gfinal, tm, tf,
             rows_per_batch, final_norm):
    m, d = x2.shape
    ff = wu_bf16.shape[1]
    mod = lambda a: _mod_spec(a, tm, rows_per_batch, d)
    row = pl.BlockSpec((1, d), lambda i, f: (0, 0))
    in_specs = [pl.BlockSpec((tm, d), lambda i, f: (i, 0)), mod(gate_m)]
    in_specs += [pl.BlockSpec((tm, y.shape[1]), lambda i, f: (i, 0)) for y in ys]
    in_specs += [pl.BlockSpec(w.shape, lambda i, f: (0, 0)) for w in ws_bf16]
    in_specs += [row, mod(sc), mod(sh), mod(gate),
                 pl.BlockSpec((d, tf), lambda i, f: (0, f)),
                 pl.BlockSpec((tf, d), lambda i, f: (f, 0)), row]
    return pl.pallas_call(
        functools.partial(_mix_mlp_body, n_in=len(ys), final_norm=final_norm),
        grid=(m // tm, ff // tf),
        in_specs=in_specs,
        out_specs=pl.BlockSpec((tm, d), lambda i, f: (i, 0)),
        out_shape=jax.ShapeDtypeStruct((m, d), F32),
        scratch_shapes=[pltpu.VMEM((tm, d), BF16), pltpu.VMEM((tm, d), F32),
                        pltpu.VMEM((tm, d), F32)],
        compiler_params=_cparams(("arbitrary", "arbitrary")),
        name="mix_mlp",
    )(x2, gate_m, *ys, *ws_bf16, g.reshape(1, d), sc, sh, gate, wu_bf16, wd_bf16,
      gfinal.reshape(1, d))


def _rwkv_project(za, prev_row, mu, w0, w_up, a0, a_up, g_up, d_a):
    row = lax.broadcasted_iota(jnp.int32, za.shape, 0)
    prev = jnp.where(row == 0, prev_row, pltpu.roll(za, 1, axis=0))
    zs = za + mu * (prev - za)
    o_w = 3 * d_a
    o_a = o_w + LORA_W
    o_g = o_a + LORA_A
    lw = -DECAY_SCALE * _sigmoid(w0 + _bdot(jnp.tanh(zs[:, o_w:o_a]), w_up))
    a = _sigmoid(a0 + _bdot(zs[:, o_a:o_g], a_up))
    g = _bdot(_sigmoid(zs[:, o_g:o_g + LORA_G]), g_up)
    return zs[:, 0:d_a], zs[:, d_a:2 * d_a], zs[:, 2 * d_a:3 * d_a], lw, a, g


def _segsum(x, ones_blk):
    hi = x.astype(BF16)
    lo = (x - hi.astype(F32)).astype(BF16)
    w = ones_blk.shape[0]
    parts = [jnp.dot(hi[:, c0:c0 + w], ones_blk, preferred_element_type=F32)
             + jnp.dot(lo[:, c0:c0 + w], ones_blk, preferred_element_type=F32)
             for c0 in range(0, x.shape[1], w)]
    return jnp.concatenate(parts, axis=1)


def _rwkv_prepare(r, k, v, lw, a, kkw, kaw, valid, tri_incl, ones_blk):
    c = r.shape[0]
    kk = k * kkw
    kk = kk / jnp.maximum(jnp.sqrt(_segsum(kk * kk, ones_blk)), 1e-12)
    k2 = k * (1.0 + (a - 1.0) * kaw)
    b = kk * a
    if valid is not None:
        zero = lambda x: jnp.where(valid, x, 0.0)
        lw, kk, k2, b, v, r = (zero(x) for x in (lw, kk, k2, b, v, r))
    p1 = lw.astype(BF16)
    d1 = lw - p1.astype(F32)
    p2 = d1.astype(BF16)
    p3 = (d1 - p2.astype(F32)).astype(BF16)
    tri = tri_incl.astype(BF16)
    cum = sum(jnp.dot(tri, p, preferred_element_type=F32) for p in (p1, p2, p3))
    tot = cum[c - 1:c, :]
    e_in = jnp.exp(cum)
    e_out = jnp.exp(-cum)
    e_end = jnp.exp(tot - cum)
    return dict(r=r, k2=k2, v=v, kap_t=kk * jnp.exp(cum - lw), r_t=r * e_in, b_t=b * e_out,
                k_t=k2 * e_out, b_h=b * e_end, k_h=k2 * e_end, e_tot=jnp.exp(tot))


def _rwkv_local(kap_t, r_t, b_t, k_t, b_h, k_h, v, tri_incl, tri_strict, eye, n_doubling):
    c = kap_t[0].shape[0]
    each = lambda f, *ls: [f(*xs) for xs in zip(*ls)]

    rhs = each(lambda x, y: jnp.concatenate([x, y], axis=0), b_t, k_t)
    gk = each(_bdot_nt, kap_t, rhs)
    gr = each(_bdot_nt, r_t, rhs)
    n1 = each(lambda x: jnp.where(tri_strict, x[:, :c], 0.0), gk)
    n2 = each(lambda x: jnp.where(tri_strict, x[:, c:], 0.0), gk)
    m1 = each(lambda x: jnp.where(tri_incl > 0, x[:, :c], 0.0), gr)
    m2 = each(lambda x: jnp.where(tri_incl > 0, x[:, c:], 0.0), gr)

    x = each(lambda n: -n, n1)
    tinv = each(lambda x_: eye + x_, x)
    for _ in range(n_doubling):
        x = each(_bdot, x, x)
        tinv = each(lambda t_, x_: t_ + _bdot(t_, x_), tinv, x)

    nv = each(lambda n, m_, v_: _bdot(jnp.concatenate([n, m_], axis=0), v_), n2, m2, v)
    ty = each(lambda t_, kp, nv_: _bdot(t_, jnp.concatenate([kp, nv_[:c]], axis=1)),
              tinv, kap_t, nv)
    m1ty = each(_bdot, m1, ty)
    rq = each(lambda r_, m_: r_ - m_[:, :HEAD_A], r_t, m1ty)
    ol = each(lambda nv_, m_: nv_[c:] - m_[:, HEAD_A:], nv, m1ty)
    tb = each(_bdot_tn, ty, b_h)
    vk = each(_bdot_tn, v, k_h)
    gmat = each(lambda tb_: tb_[:HEAD_A], tb)
    hmat = each(lambda vk_, tb_: vk_ - tb_[HEAD_A:], vk, tb)
    return rq, ol, gmat, hmat


def _rwkv_chunk_body(za_ref, shift_ref, mu_ref, w0_ref, wup_ref, a0_ref, aup_ref, gup_ref,
                     kkw_ref, kaw_ref, rkw_ref, lnw_ref, lnb_ref, s0_ref, y_ref, sout_ref,
                     s_ref, last_ref, *, chunk, n_chunks, n_heads, n_valid, group):
    @pl.when(pl.program_id(1) == 0)
    def _():
        s_ref[...] = s0_ref[...]
        last_ref[...] = shift_ref[...]

    c = chunk
    d_a = n_heads * HEAD_A
    ri = lax.broadcasted_iota(jnp.int32, (c, c), 0)
    ci = lax.broadcasted_iota(jnp.int32, (c, c), 1)
    tri_incl = (ri >= ci).astype(F32)
    tri_strict = ri > ci
    eye = (ri == ci).astype(F32)
    bi = lax.broadcasted_iota(jnp.int32, (MXU_DIM, MXU_DIM), 0) // HEAD_A
    bj = lax.broadcasted_iota(jnp.int32, (MXU_DIM, MXU_DIM), 1) // HEAD_A
    ones_blk = (bi == bj).astype(BF16)
    valid = None
    if n_valid < c:
        valid = lax.broadcasted_iota(jnp.int32, (c, d_a), 0) < n_valid
    n_doubling = max(int(math.log2(c)) - 1, 0)
    heads = range(n_heads)
    sls = [slice(h * HEAD_A, (h + 1) * HEAD_A) for h in heads]

    n_seq = za_ref.shape[0]

    def one_group(ig, carry):
        rows = [pl.ds(pl.multiple_of((ig * group + gi) * c, c), c) for gi in range(group)]
        pre = []
        for bi in range(n_seq):
            prev_row = last_ref[bi]
            for gi in range(group):
                za = za_ref[bi, rows[gi], :]
                r, k, v, lw, a, g = _rwkv_project(za, prev_row, mu_ref[...], w0_ref[...],
                                                  wup_ref[...], a0_ref[...], aup_ref[...],
                                                  gup_ref[...], d_a)
                prev_row = za[c - 1:c, :]
                pre.append(dict(_rwkv_prepare(r, k, v, lw, a, kkw_ref[...], kaw_ref[...], valid,
                                              tri_incl, ones_blk), g=g))
            last_ref[bi] = prev_row
        chains = lambda name: [p[name][:, sl] for p in pre for sl in sls]
        rq, ol, gmat, hmat = _rwkv_local(
            chains('kap_t'), chains('r_t'), chains('b_t'), chains('k_t'), chains('b_h'),
            chains('k_h'), chains('v'), tri_incl, tri_strict, eye, n_doubling)
        s = [[s_ref[bi, h] for h in heads] for bi in range(n_seq)]
        for gi in range(group):
            for bi in range(n_seq):
                p = pre[bi * group + gi]
                at = lambda xs, h: xs[(bi * group + gi) * n_heads + h]
                o = jnp.concatenate([_bdot_nt(at(rq, h), s[bi][h]) + at(ol, h) for h in heads],
                                    axis=1)
                s[bi] = [s[bi][h] * p['e_tot'][:, sls[h]] - _bdot(s[bi][h], at(gmat, h))
                         + at(hmat, h) for h in heads]
                mu = _segsum(o, ones_blk) * (1.0 / HEAD_A)
                dev = o - mu
                var = _segsum(dev * dev, ones_blk) * (1.0 / HEAD_A)
                on = dev * lax.rsqrt(var + GN_EPS) * lnw_ref[...] + lnb_ref[...]
                bonus = _segsum(p['r'] * p['k2'] * rkw_ref[...], ones_blk) * p['v']
                y_ref[bi, rows[gi], :] = (on + bonus) * p['g']
        for bi in range(n_seq):
            for h in heads:
                s_ref[bi, h] = s[bi][h]
        return carry

    lax.fori_loop(0, n_chunks // group, one_group, 0)

    @pl.when(pl.program_id(1) == pl.num_programs(1) - 1)
    def _():
        sout_ref[...] = s_ref[...]


def _rwkv_chunked(za, shift_prev, mu, w0, w_up, a0, a_up, g_up, k_k, k_a, r_k, lnx_w, lnx_b, s0,
                  tb, chunk, n_valid):
    b, t, dp = za.shape
    d_a = w0.shape[0]
    n_heads = d_a // HEAD_A
    row = lambda n: pl.BlockSpec((1, n), lambda i, j: (0, 0))
    full = lambda w: pl.BlockSpec(w.shape, lambda i, j: (0, 0))
    group = min(RWKV_PARALLEL, tb // chunk)
    n_seq = max(d for d in range(1, RWKV_PARALLEL // group + 1) if b % d == 0)
    st = pl.BlockSpec((n_seq, n_heads, HEAD_A, HEAD_A), lambda i, j: (i, 0, 0, 0))
    return pl.pallas_call(
        functools.partial(_rwkv_chunk_body, chunk=chunk, n_chunks=tb // chunk,
                          n_heads=n_heads, n_valid=n_valid, group=group),
        grid=(b // n_seq, t // tb),
        in_specs=[pl.BlockSpec((n_seq, tb, dp), lambda i, j: (i, j, 0)),
                  pl.BlockSpec((n_seq, 1, dp), lambda i, j: (i, 0, 0)),
                  row(dp), row(d_a), full(w_up), row(d_a), full(a_up), full(g_up)]
                 + [row(d_a)] * 5 + [st],
        out_specs=(pl.BlockSpec((n_seq, tb, d_a), lambda i, j: (i, j, 0)), st),
        out_shape=(jax.ShapeDtypeStruct((b, t, d_a), F32),
                   jax.ShapeDtypeStruct((b, n_heads, HEAD_A, HEAD_A), F32)),
        scratch_shapes=[pltpu.VMEM((n_seq, n_heads, HEAD_A, HEAD_A), F32),
                        pltpu.VMEM((n_seq, 1, dp), F32)],
        compiler_params=_cparams(("arbitrary", "arbitrary")),
        name="rwkv_chunked",
    )(za, shift_prev.reshape(b, 1, dp), mu.reshape(1, dp), w0.reshape(1, d_a), w_up,
      a0.reshape(1, d_a), a_up, g_up, k_k.reshape(1, d_a), k_a.reshape(1, d_a),
      r_k.reshape(1, d_a), lnx_w.reshape(1, d_a), lnx_b.reshape(1, d_a), s0)


def _s5_disc_body(lr_ref, li_ref, ldt_ref, bre_ref, bim_ref, ar_ref, ai_ref, bbre_ref, bbim_ref):
    lr, li = lr_ref[...], li_ref[...]
    dt = jnp.exp(ldt_ref[...])
    mag = jnp.exp(lr * dt)
    ar = mag * jnp.cos(li * dt)
    ai = mag * jnp.sin(li * dt)
    den = lr * lr + li * li
    fr = ((ar - 1.0) * lr + ai * li) / den
    fi = (ai * lr - (ar - 1.0) * li) / den
    ar_ref[...] = ar
    ai_ref[...] = ai
    b_re, b_im = bre_ref[...], bim_ref[...]
    bb_re = fr[:, None, :] * b_re - fi[:, None, :] * b_im
    bb_im = fr[:, None, :] * b_im + fi[:, None, :] * b_re
    pr, pi = jnp.ones_like(ar), jnp.zeros_like(ai)
    for i in range(S5_FOLD):
        bbre_ref[i] = pr[:, None, :] * bb_re - pi[:, None, :] * bb_im
        bbim_ref[i] = pr[:, None, :] * bb_im + pi[:, None, :] * bb_re
        pr, pi = pr * ar - pi * ai, pr * ai + pi * ar


def _s5_discretise(lam_re, lam_im, log_dt, b_re, b_im):
    g, p = lam_re.shape
    n = b_re.shape[2]
    bt = lambda x: jnp.swapaxes(x, 1, 2)
    return pl.pallas_call(
        _s5_disc_body,
        out_shape=(jax.ShapeDtypeStruct((g, p), F32), jax.ShapeDtypeStruct((g, p), F32),
                   jax.ShapeDtypeStruct((S5_FOLD, g, n, p), F32),
                   jax.ShapeDtypeStruct((S5_FOLD, g, n, p), F32)),
        name="s5_discretise",
    )(lam_re, lam_im, log_dt.reshape(g, 1), bt(b_re), bt(b_im))


def _gelu_tanh(x):
    c = math.sqrt(2.0 / math.pi)
    return 0.5 * x * (1.0 + jnp.tanh(c * (x + 0.044715 * (x * x * x))))


def _s5_body(u_ref, h0r_ref, h0i_ref, ar_ref, ai_ref, bdre_ref, bdim_ref, cdre_ref, cdim_ref,
             d_ref, wglu_ref, bglu_ref, y_ref, hre_ref, him_ref,
             xr_ref, xi_ref, apr_ref, api_ref, mr_ref, mi_ref, cr_ref, ci_ref, *, last_row):
    j = pl.program_id(1)
    rows, width = xr_ref.shape
    half_u = u_ref.shape[1] // 2
    half_x = width // 2

    ar, ai = ar_ref[...], ai_ref[...]
    a2r, a2i = ar * ar - ai * ai, 2.0 * ar * ai
    a4r, a4i = a2r * a2r - a2i * a2i, 2.0 * a2r * a2i

    @pl.when(j == 0)
    def _():
        cr_ref[...] = h0r_ref[...]
        ci_ref[...] = h0i_ref[...]
        pr, pi = ar, ai
        for s in range(SUBLANES):
            apr_ref[s:s + 1, :] = pr
            api_ref[s:s + 1, :] = pi
            pr, pi = pr * ar - pi * ai, pr * ai + pi * ar
        sub8 = lax.broadcasted_iota(jnp.int32, (SUBLANES, width), 0)
        mr_ref[...] = jnp.where(sub8 >= S5_FOLD, a4r, 0.0)
        mi_ref[...] = jnp.where(sub8 >= S5_FOLD, a4i, 0.0)

    u = u_ref[...]
    n_parts = bdre_ref.shape[0]
    part_x = width // n_parts
    sub = lax.broadcasted_iota(jnp.int32, (rows, LANES), 0) % SUBLANES
    for pt in range(n_parts):
        up = u[:, pt * LANES:(pt + 1) * LANES]
        lags = [up] + [jnp.where(sub >= i, pltpu.roll(up, i, axis=0), 0.0)
                       for i in range(1, S5_FOLD)]
        lhs = jnp.concatenate(lags, axis=1).astype(BF16)
        cols = slice(pt * part_x, (pt + 1) * part_x)
        xr_ref[:, cols] = jnp.dot(lhs, bdre_ref[pt], preferred_element_type=F32)
        xi_ref[:, cols] = jnp.dot(lhs, bdim_ref[pt], preferred_element_type=F32)

    lane_chunk = 4 * LANES
    grouped = lambda x: x.reshape(rows // SUBLANES, SUBLANES, lane_chunk)
    for c0 in range(0, width, lane_chunk):
        cols = slice(c0, c0 + lane_chunk)
        xr, xi = xr_ref[:, cols], xi_ref[:, cols]
        pr, pi = mr_ref[:, cols][None], mi_ref[:, cols][None]
        sr = grouped(pltpu.roll(xr, S5_FOLD, axis=0))
        si = grouped(pltpu.roll(xi, S5_FOLD, axis=0))
        xr_ref[:, cols] = (grouped(xr) + (pr * sr - pi * si)).reshape(rows, lane_chunk)
        xi_ref[:, cols] = (grouped(xi) + (pr * si + pi * sr)).reshape(rows, lane_chunk)

    def group(n, carry):
        cr, ci = carry
        r0 = pl.multiple_of(n * SUBLANES, SUBLANES)
        rs = pl.ds(r0, SUBLANES)
        apr, api = apr_ref[...], api_ref[...]
        hr = xr_ref[rs, :] + apr * cr - api * ci
        hi = xi_ref[rs, :] + apr * ci + api * cr
        xr_ref[rs, :] = hr
        xi_ref[rs, :] = hi
        return hr[SUBLANES - 1:SUBLANES, :], hi[SUBLANES - 1:SUBLANES, :]

    cr, ci = lax.fori_loop(0, rows // SUBLANES, group, (cr_ref[...], ci_ref[...]))
    cr_ref[...] = cr
    ci_ref[...] = ci

    @pl.when(j == pl.num_programs(1) - 1)
    def _():
        hre_ref[...] = xr_ref[last_row:last_row + 1, :]
        him_ref[...] = xi_ref[last_row:last_row + 1, :]

    for hf in range(2):
        cols = slice(hf * half_x, (hf + 1) * half_x)
        ucols = slice(hf * half_u, (hf + 1) * half_u)
        y = (jnp.dot(xr_ref[:, cols].astype(BF16), cdre_ref[hf], preferred_element_type=F32)
             - jnp.dot(xi_ref[:, cols].astype(BF16), cdim_ref[hf], preferred_element_type=F32)
             + d_ref[:, ucols] * u[:, ucols])
        y_ref[:, ucols] = _gelu_tanh(y)
    y = y_ref[...]
    z = jnp.dot(y.astype(BF16), wglu_ref[...], preferred_element_type=F32) + bglu_ref[...]
    y_ref[...] = y * _sigmoid(z)


def _block_diag_parts(w_gab, transpose, n_parts=2):
    g = w_gab.shape[0]
    hg = g // n_parts
    eye = jnp.eye(hg, dtype=w_gab.dtype)
    parts = []
    for pt in range(n_parts):
        w = w_gab[pt * hg:(pt + 1) * hg]
        if transpose:
            w = jnp.swapaxes(w, 1, 2)
        a, b = w.shape[1], w.shape[2]
        parts.append(jnp.einsum('gab,gh->gahb', w, eye).reshape(hg * a, hg * b))
    return jnp.stack(parts)


def _s5_mix(u, h0_re, h0_im, ar, ai, bbt_re, bbt_im, c_re, c_im, d, w_glu, b_glu, rows, n_valid):
    b, t, d_b = u.shape
    g, p = ar.shape
    width = g * p
    n_parts = d_b // LANES
    lagged = lambda w: jnp.concatenate(
        [_block_diag_parts(w[i], False, n_parts) for i in range(S5_FOLD)], axis=1).astype(BF16)
    bd_re, bd_im = lagged(bbt_re), lagged(bbt_im)
    cd_re = _block_diag_parts(c_re, True).astype(BF16)
    cd_im = _block_diag_parts(c_im, True).astype(BF16)
    n_blocks = t // rows
    last_row = (n_valid - 1) - (n_blocks - 1) * rows
    full = lambda x: pl.BlockSpec(x.shape, lambda i, j: (0,) * x.ndim)
    st = pl.BlockSpec((None, 1, width), lambda i, j: (i, 0, 0))
    args = (u, h0_re.reshape(b, 1, width), h0_im.reshape(b, 1, width),
            ar.reshape(1, width), ai.reshape(1, width), bd_re, bd_im, cd_re, cd_im,
            d.reshape(1, d_b), w_glu.astype(BF16), b_glu.reshape(1, d_b))
    return pl.pallas_call(
        functools.partial(_s5_body, last_row=last_row),
        grid=(b, n_blocks),
        in_specs=[pl.BlockSpec((None, rows, d_b), lambda i, j: (i, j, 0)), st, st]
                 + [full(x) for x in args[3:]],
        out_specs=(pl.BlockSpec((None, rows, d_b), lambda i, j: (i, j, 0)), st, st),
        out_shape=(jax.ShapeDtypeStruct((b, t, d_b), F32),
                   jax.ShapeDtypeStruct((b, 1, width), F32),
                   jax.ShapeDtypeStruct((b, 1, width), F32)),
        scratch_shapes=[pltpu.VMEM((rows, width), F32), pltpu.VMEM((rows, width), F32),
                        pltpu.VMEM((SUBLANES, width), F32), pltpu.VMEM((SUBLANES, width), F32),
                        pltpu.VMEM((SUBLANES, width), F32), pltpu.VMEM((SUBLANES, width), F32),
                        pltpu.VMEM((1, width), F32), pltpu.VMEM((1, width), F32)],
        compiler_params=_cparams(("arbitrary", "arbitrary")),
        name="s5_mix",
    )(*args)


def _lambda_full(lq1, lk1, lq2, lk2):
    s1 = jnp.sum(lq1 * lk1, axis=-1, keepdims=True)
    s2 = jnp.sum(lq2 * lk2, axis=-1, keepdims=True)
    return jnp.exp(s1) - jnp.exp(s2) + LAMBDA_INIT


def _sub_ln(o, subln):
    ms = jnp.mean(o * o, axis=-1, keepdims=True)
    return o * lax.rsqrt(ms + RMS_EPS) * subln * (1.0 - LAMBDA_INIT)


def _attn_prompt_body(q_ref, k_ref, v_ref, lq1_ref, lk1_ref, lq2_ref, lk2_ref, subln_ref,
                      o_ref, kb_ref, vt_ref, m_ref, l_ref, acc_ref, sa_ref, sb_ref, *, tile,
                      n_tiles):
    halves = range(2)
    for c in range(n_tiles):
        rows = slice(c * tile, (c + 1) * tile)
        kb_ref[rows, :] = k_ref[rows, :].astype(BF16)
        vt_ref[c] = v_ref[rows, :].T.astype(BF16)
    lam = _lambda_full(lq1_ref[...], lk1_ref[...], lq2_ref[...], lk2_ref[...])
    s_even, s_odd = sa_ref, sb_ref

    wide = 2 * tile
    all_q = slice(0, wide)
    late_q = slice(tile, wide)

    def q_tile(qj, carry):
        q_rows = pl.ds(pl.multiple_of(qj * wide, wide), wide)
        q = q_ref[q_rows, :] * (SCALE_C * math.log2(math.e))
        lane = lax.broadcasted_iota(jnp.int32, q.shape, 1)
        qs = [jnp.where((lane >= HALF_C) == bool(c), q, 0.0).astype(BF16) for c in halves]
        m_ref[...] = jnp.full_like(m_ref, NEG_INF)
        l_ref[...] = jnp.zeros_like(l_ref)
        acc_ref[...] = jnp.zeros_like(acc_ref)

        def scores(kj, slot, cols):
            k0 = kj * tile if isinstance(kj, int) else pl.multiple_of(kj * tile, tile)
            kb = kb_ref[pl.ds(k0, tile), :]
            for c in halves:
                slot[c, :, cols] = lax.dot_general(kb, qs[c][cols], (((1,), (1,)), ((), ())),
                                                   preferred_element_type=F32)

        def kv_step(kj, slot, diagonal, cols):
            st = [slot[c, :, cols] for c in halves]
            if diagonal:
                krow = lax.broadcasted_iota(jnp.int32, st[0].shape, 0)
                qcol = lax.broadcasted_iota(jnp.int32, st[0].shape, 1)
                st = [jnp.where(krow <= qcol, x, NEG_INF) for x in st]
            m_prev = [m_ref[c, :, cols] for c in halves]
            m_new = [jnp.maximum(m_prev[c], jnp.max(st[c], axis=0, keepdims=True))
                     for c in halves]
            alpha = [jnp.exp2(m_prev[c] - m_new[c]) for c in halves]
            p = [jnp.exp2(st[c] - m_new[c]) for c in halves]
            vt = vt_ref[kj]
            pv = [jnp.dot(vt, p[c].astype(BF16), preferred_element_type=F32) for c in halves]
            for c in halves:
                l_ref[c, :, cols] = alpha[c] * l_ref[c, :, cols] + jnp.sum(p[c], axis=0,
                                                                           keepdims=True)
                acc_ref[c, :, cols] = alpha[c] * acc_ref[c, :, cols] + pv[c]
                m_ref[c, :, cols] = m_new[c]

        scores(0, s_even, all_q)

        def two_steps(i, c2):
            scores(2 * i + 1, s_odd, all_q)
            kv_step(2 * i, s_even, False, all_q)
            scores(2 * i + 2, s_even, all_q)
            kv_step(2 * i + 1, s_odd, False, all_q)
            return c2

        lax.fori_loop(0, qj, two_steps, 0)
        scores(2 * qj + 1, s_odd, late_q)
        kv_step(2 * qj, s_even, True, all_q)
        kv_step(2 * qj + 1, s_odd, True, late_q)

        ot = acc_ref[0] / l_ref[0] - lam * (acc_ref[1] / l_ref[1])
        ms = jnp.mean(ot * ot, axis=0, keepdims=True)
        ot = ot * lax.rsqrt(ms + RMS_EPS) * subln_ref[...] * (1.0 - LAMBDA_INIT)
        o_ref[q_rows, :] = ot.T
        return carry

    lax.fori_loop(0, n_tiles // 2, q_tile, 0)


def _attn_prompt(q, k, v, lq1, lk1, lq2, lk2, subln, tile):
    b, t, d = q.shape
    n_heads = d // HEAD_C
    n_tiles = t // tile
    assert t % (2 * tile) == 0, (t, tile)
    small = lambda n: pl.BlockSpec((1, n), lambda i, h: (0, 0))
    per_head = pl.BlockSpec((None, t, HEAD_C), lambda i, h: (i, 0, h))
    return pl.pallas_call(
        functools.partial(_attn_prompt_body, tile=tile, n_tiles=n_tiles),
        grid=(b, n_heads),
        in_specs=[per_head, per_head, per_head,
                  small(HALF_C), small(HALF_C), small(HALF_C), small(HALF_C),
                  pl.BlockSpec((HEAD_C, 1), lambda i, h: (0, 0))],
        out_specs=per_head,
        out_shape=jax.ShapeDtypeStruct((b, t, d), F32),
        scratch_shapes=[pltpu.VMEM((t, HEAD_C), BF16), pltpu.VMEM((n_tiles, HEAD_C, tile), BF16),
                        pltpu.VMEM((2, 1, 2 * tile), F32), pltpu.VMEM((2, 1, 2 * tile), F32),
                        pltpu.VMEM((2, HEAD_C, 2 * tile), F32),
                        pltpu.VMEM((2, tile, 2 * tile), F32),
                        pltpu.VMEM((2, tile, 2 * tile), F32)],
        compiler_params=_cparams(("arbitrary", "arbitrary")),
        name="attn_prompt",
    )(q, k, v, lq1.reshape(1, -1), lk1.reshape(1, -1), lq2.reshape(1, -1), lk2.reshape(1, -1),
      subln.reshape(-1, 1))


def _attn_sample_body(pt_ref, qrep_ref, knew_ref, vnew_ref, *rest, pages_per_step, n_heads,
                      t_new, page):
    kp_refs = rest[:pages_per_step]
    vp_refs = rest[pages_per_step:2 * pages_per_step]
    lq1_ref, lk1_ref, lq2_ref, lk2_ref, subln_ref = rest[2 * pages_per_step:2 * pages_per_step + 5]
    o_ref = rest[2 * pages_per_step + 5]
    qw_ref, m_ref, l_ref, acc_ref = rest[2 * pages_per_step + 6:]
    j = pl.program_id(1)
    pairs = range(n_heads // 2)
    q_rows = qrep_ref.shape[1]

    @pl.when(j == 0)
    def _():
        qr = qrep_ref[...]
        row = lax.broadcasted_iota(jnp.int32, qr.shape, 1)
        lane = lax.broadcasted_iota(jnp.int32, qr.shape, 2)
        qm = jnp.where(lane // HALF_C == row // t_new, qr * SCALE_C, 0.0)
        zero = jnp.zeros((q_rows, HEAD_C), F32)
        for pr in pairs:
            qw_ref[pr] = jnp.concatenate(
                [jnp.concatenate([qm[2 * pr], zero], axis=1),
                 jnp.concatenate([zero, qm[2 * pr + 1]], axis=1)], axis=0).astype(BF16)
        m_ref[...] = jnp.full_like(m_ref, NEG_INF)
        l_ref[...] = jnp.zeros_like(l_ref)
        acc_ref[...] = jnp.zeros_like(acc_ref)

    def pair_rows(ref, pr):
        return jnp.concatenate([ref[pl.ds(2 * pr + i, page, stride=n_heads), :] for i in range(2)],
                               axis=1).astype(BF16)

    def update(s, values):
        m_prev = [m_ref[pr] for pr in pairs]
        m_new = [jnp.maximum(m_prev[pr], jnp.max(s[pr], axis=-1, keepdims=True)) for pr in pairs]
        alpha = [jnp.exp(m_prev[pr] - m_new[pr]) for pr in pairs]
        p = [jnp.exp(s[pr] - m_new[pr]) for pr in pairs]
        pv = []
        for pr in pairs:
            w = p[pr].shape[1] // len(values[pr])
            acc = None
            for i, vb in enumerate(values[pr]):
                d = jnp.dot(p[pr][:, i * w:(i + 1) * w].astype(BF16), vb,
                            preferred_element_type=F32)
                acc = d if acc is None else acc + d
            pv.append(acc)
        for pr in pairs:
            l_ref[pr] = alpha[pr] * l_ref[pr] + jnp.sum(p[pr], axis=-1, keepdims=True)
            acc_ref[pr] = alpha[pr] * acc_ref[pr] + pv[pr]
            m_ref[pr] = m_new[pr]

    s = [jnp.concatenate(
            [lax.dot_general(qw_ref[pr], pair_rows(kp, pr), (((1,), (1,)), ((), ())),
                             preferred_element_type=F32) for kp in kp_refs], axis=1)
         for pr in pairs]
    update(s, [[pair_rows(vp, pr) for vp in vp_refs] for pr in pairs])

    @pl.when(j == pl.num_programs(1) - 1)
    def _():
        both = lambda ref, pr: jnp.concatenate([ref[2 * pr], ref[2 * pr + 1]], axis=1).astype(BF16)
        sn = []
        for pr in pairs:
            x = lax.dot_general(qw_ref[pr], both(knew_ref, pr), (((1,), (1,)), ((), ())),
                                preferred_element_type=F32)
            trow = lax.broadcasted_iota(jnp.int32, x.shape, 0) % t_new
            tcol = lax.broadcasted_iota(jnp.int32, x.shape, 1)
            sn.append(jnp.where(tcol <= trow, x, NEG_INF))
        update(sn, [[both(vnew_ref, pr)] for pr in pairs])
        lam = _lambda_full(lq1_ref[...], lk1_ref[...], lq2_ref[...], lk2_ref[...])
        for pr in pairs:
            on = acc_ref[pr] / l_ref[pr]
            for i in range(2):
                h = 2 * pr + i
                r0 = i * q_rows
                oh = on[r0:r0 + 2 * t_new, i * HEAD_C:(i + 1) * HEAD_C]
                o = oh[0:t_new] - lam * oh[t_new:2 * t_new]
                o_ref[:, h * HEAD_C:(h + 1) * HEAD_C] = _sub_ln(o, subln_ref[...])


def _attn_sample(q, k_new, v_new, cache_k, cache_v, page_table, lq1, lk1, lq2, lk2, subln,
                 pages_per_step):
    db, t_new, d = q.shape
    n_heads = d // HEAD_C
    n_pool, page = cache_k.shape[0], cache_k.shape[1]
    n_pages = page_table.shape[1]
    ck = cache_k.reshape(n_pool, page * n_heads, HEAD_C)
    cv = cache_v.reshape(n_pool, page * n_heads, HEAD_C)
    by_head = lambda x: jnp.swapaxes(x.reshape(db, t_new, n_heads, HEAD_C), 1, 2)
    q_rows = NEW_PAD
    qrep = jnp.tile(by_head(q), (1, 1, q_rows // t_new, 1))
    pad = ((0, 0), (0, 0), (0, NEW_PAD - t_new), (0, 0))
    knew = jnp.pad(by_head(k_new), pad)
    vnew = jnp.pad(by_head(v_new), pad)
    pt = page_table.reshape(-1)

    def page_spec(i):
        return pl.BlockSpec((None, page * n_heads, HEAD_C),
                            lambda b, j, pt_ref: (pt_ref[b * n_pages + j * pages_per_step + i], 0, 0))

    small = lambda n: pl.BlockSpec((1, n), lambda b, j, pt_ref: (0, 0))
    per_b = lambda r: pl.BlockSpec((None, n_heads, r, HEAD_C), lambda b, j, pt_ref: (b, 0, 0, 0))
    grid_spec = pltpu.PrefetchScalarGridSpec(
        num_scalar_prefetch=1,
        grid=(db, n_pages // pages_per_step),
        in_specs=[per_b(q_rows), per_b(NEW_PAD), per_b(NEW_PAD)]
                 + [page_spec(i) for i in range(pages_per_step)] * 2
                 + [small(HALF_C)] * 4 + [small(HEAD_C)],
        out_specs=pl.BlockSpec((None, t_new, d), lambda b, j, pt_ref: (b, 0, 0)),
        scratch_shapes=[pltpu.VMEM((n_heads // 2, 2 * q_rows, 2 * HEAD_C), BF16),
                        pltpu.VMEM((n_heads // 2, 2 * q_rows, 1), F32),
                        pltpu.VMEM((n_heads // 2, 2 * q_rows, 1), F32),
                        pltpu.VMEM((n_heads // 2, 2 * q_rows, 2 * HEAD_C), F32)])
    return pl.pallas_call(
        functools.partial(_attn_sample_body, pages_per_step=pages_per_step, n_heads=n_heads,
                          t_new=t_new, page=page),
        grid_spec=grid_spec,
        out_shape=jax.ShapeDtypeStruct((db, t_new, d), F32),
        compiler_params=_cparams(("arbitrary", "arbitrary")),
        name="attn_sample",
    )(pt, qrep, knew, vnew, *([ck] * pages_per_step), *([cv] * pages_per_step),
      lq1.reshape(1, -1), lk1.reshape(1, -1), lq2.reshape(1, -1), lk2.reshape(1, -1),
      subln.reshape(1, -1))


def _pad_time(x, t_pad):
    return jnp.pad(x, ((0, 0), (0, t_pad - x.shape[1]), (0, 0)))


def _trunk(x, mods, shift0, wkv0, sre0, sim0, attend, W, per_row_mod):
    b, t, d = x.shape
    m = b * t
    rows_mod = m if per_row_mod else t
    tm = min(512, rows_mod)
    tm_mlp = min(1024, rows_mod)
    d_a = W['rwkv_w0'].shape[0]
    d_ap = W['rwkv_mu'].shape[0]

    def mod_arg(v):
        if per_row_mod:
            return jnp.repeat(v, t, axis=0)
        return v.reshape(b, 1, d)

    x2 = x.reshape(m, d)

    sh_m, sc_m, g_m, sh_f, sc_f, g_f = (mod_arg(v) for v in mods[0])
    za2, u2 = _normmod_mm(x2, W['norm_mix'][0], sc_m, sh_m, W['w_in_ab_bf16'],
                          ((0, d_ap), (d_ap, W['w_in_ab_bf16'].shape[1])), tm, t)
    za = za2.reshape(b, t, d_ap)
    u = u2.reshape(b, t, -1)
    shift1 = za[:, -1]

    chunk = RWKV_CHUNK if t >= RWKV_CHUNK else RWKV_CHUNK_SHORT
    t_pad = -(-t // chunk) * chunk
    if t_pad != t:
        za_p, u_p = _pad_time(za, t_pad), _pad_time(u, t_pad)
    else:
        za_p, u_p = za, u
    tb = min(512, t_pad)
    y_a, wkv1 = _rwkv_chunked(za_p, shift0, W['rwkv_mu'], W['rwkv_w0'], W['rwkv_w_up'],
                              W['rwkv_a0'], W['rwkv_a_up'], W['rwkv_g_up'], W['rwkv_k_k'],
                              W['rwkv_k_a'], W['rwkv_r_k'], W['rwkv_lnx_w'], W['rwkv_lnx_b'],
                              wkv0, tb, chunk, chunk if t_pad == t else t)
    s5_rows = min(256, t_pad)
    y_b, sre1, sim1 = _s5_mix(u_p, sre0, sim0, W['s5_ar'], W['s5_ai'], W['s5_bbt_re'],
                              W['s5_bbt_im'], W['s5_c_re'], W['s5_c_im'], W['s5_d'],
                              W['s5_w_glu'], W['s5_b_glu'], s5_rows, t)
    y_a2 = y_a[:, :t].reshape(m, d_a)
    y_b2 = y_b[:, :t].reshape(m, -1)
    wo = W['w_out_bf16'][0]
    x2 = _mix_mlp(x2, g_m, (y_a2, y_b2), (wo[:d_a], wo[d_a:]), W['norm_mlp'][0], sc_f, sh_f, g_f,
                  W['w_up_bf16'][0], W['w_down_bf16'][0], W['norm_f'], tm_mlp, 1024, t, False)

    sh_m, sc_m, g_m, sh_f, sc_f, g_f = (mod_arg(v) for v in mods[1])
    q2, k2, v2 = _normmod_mm(x2, W['norm_mix'][1], sc_m, sh_m, W['diff_w_qkv_bf16'],
                             ((0, d), (d, 2 * d), (2 * d, 3 * d)), tm, t)
    q3, k3, v3 = (z.reshape(b, t, d) for z in (q2, k2, v2))
    o = attend(q3, k3, v3)
    y2 = _mix_mlp(x2, g_m, (o.reshape(m, d),), (W['w_out_bf16'][1],), W['norm_mlp'][1], sc_f, sh_f,
                  g_f, W['w_up_bf16'][1], W['w_down_bf16'][1], W['norm_f'], tm_mlp, 1024, t, True)

    n_heads_c = d // HEAD_C
    g_b, p_b = W['s5_ar'].shape
    return (y2.reshape(b, t, d), shift1, wkv1, sre1.reshape(b, g_b, p_b), sim1.reshape(b, g_b, p_b),
            k3.reshape(b, t, n_heads_c, HEAD_C), v3.reshape(b, t, n_heads_c, HEAD_C))


def kernel(x_prompt, x_sample, state_shift, state_wkv, state_ssm_re, state_ssm_im, cache_k, cache_v, page_table, c_prompt, c_sample, norm_mix, norm_mlp, norm_f, w_ada, b_ada, w_out, w_up, w_down, w_in_ab, rwkv_mu, rwkv_w0, rwkv_w_up, rwkv_a0, rwkv_a_up, rwkv_g_up, rwkv_k_k, rwkv_k_a, rwkv_r_k, rwkv_lnx_w, rwkv_lnx_b, s5_lam_re, s5_lam_im, s5_log_dt, s5_b_re, s5_b_im, s5_c_re, s5_c_im, s5_d, s5_w_glu, s5_b_glu, diff_w_qkv, diff_lq1, diff_lk1, diff_lq2, diff_lk2, diff_subln):
    bp, tp, d = x_prompt.shape
    db, ts, _ = x_sample.shape
    depth = w_ada.shape[0]
    n_heads_a = rwkv_r_k.shape[0]
    g_b, p_b = s5_lam_re.shape

    ar, ai, bbt_re, bbt_im = _s5_discretise(s5_lam_re, s5_lam_im, s5_log_dt, s5_b_re, s5_b_im)
    W = dict(norm_mix=norm_mix, norm_mlp=norm_mlp, norm_f=norm_f,
             w_out_bf16=w_out.astype(BF16), w_up_bf16=w_up.astype(BF16),
             w_down_bf16=w_down.astype(BF16), w_in_ab_bf16=w_in_ab.astype(BF16),
             diff_w_qkv_bf16=diff_w_qkv.astype(BF16),
             rwkv_mu=rwkv_mu, rwkv_w0=rwkv_w0, rwkv_w_up=rwkv_w_up, rwkv_a0=rwkv_a0,
             rwkv_a_up=rwkv_a_up, rwkv_g_up=rwkv_g_up, rwkv_k_k=rwkv_k_k, rwkv_k_a=rwkv_k_a,
             rwkv_r_k=rwkv_r_k.reshape(-1), rwkv_lnx_w=rwkv_lnx_w, rwkv_lnx_b=rwkv_lnx_b,
             s5_ar=ar, s5_ai=ai, s5_bbt_re=bbt_re, s5_bbt_im=bbt_im,
             s5_c_re=s5_c_re, s5_c_im=s5_c_im, s5_d=s5_d.reshape(-1), s5_w_glu=s5_w_glu,
             s5_b_glu=s5_b_glu)

    n_c = bp + db
    rows_c = -(-n_c // SUBLANES) * SUBLANES
    c_all = jnp.pad(jnp.concatenate([c_prompt, c_sample], axis=0), ((0, rows_c - n_c), (0, 0)))
    mod = _ada_mod(c_all, w_ada, b_ada)
    mods_p = [tuple(mod[l, :bp, i * d:(i + 1) * d] for i in range(6)) for l in range(depth)]
    mods_s = [tuple(mod[l, bp:n_c, i * d:(i + 1) * d] for i in range(6)) for l in range(depth)]

    lam_args = (diff_lq1, diff_lk1, diff_lq2, diff_lk2, diff_subln)
    attend_p = lambda q, k, v: _attn_prompt(q, k, v, *lam_args, tile=min(512, tp // 2))
    y_prompt, p_shift, p_wkv, p_re, p_im, p_k, p_v = _trunk(
        x_prompt, mods_p,
        jnp.zeros((bp, rwkv_mu.shape[0]), F32),
        jnp.zeros((bp, n_heads_a, HEAD_A, HEAD_A), F32),
        jnp.zeros((bp, g_b, p_b), F32), jnp.zeros((bp, g_b, p_b), F32),
        attend_p, W, per_row_mod=False)

    attend_s = lambda q, k, v: _attn_sample(q, k, v, cache_k, cache_v, page_table, *lam_args,
                                            pages_per_step=16)
    y_sample, s_shift, s_wkv, s_re, s_im, s_k, s_v = _trunk(
        x_sample, mods_s, state_shift, state_wkv, state_ssm_re, state_ssm_im,
        attend_s, W, per_row_mod=True)

    return (y_prompt, y_sample, p_shift, p_wkv, p_re, p_im, p_k, p_v,
            s_shift, s_wkv, s_re, s_im, s_k, s_v)
```

```python
import functools
import math

import jax
import jax.numpy as jnp
from jax import lax
from jax.experimental import pallas as pl
from jax.experimental.pallas import tpu as pltpu

F32 = jnp.float32
BF16 = jnp.bfloat16

HEAD_A = 64
LORA_W, LORA_A, LORA_G = 64, 64, 128
S5_GROUP = 16
S5_STATE = 64
HEAD_C = 128
HALF_C = HEAD_C // 2
SCALE_C = HALF_C ** -0.5
RMS_EPS = 1e-6
GN_EPS = 64e-5
DECAY_SCALE = math.exp(-0.5)
LAMBDA_INIT = 0.8 - 0.6 * math.exp(-0.3 * 1)
NEG_INF = -1e30

SUBLANES = 8
LANES = 128
MXU_DIM = 256
VMEM_LIMIT = 56 * 1024 * 1024

RWKV_CHUNK = 64
RWKV_CHUNK_SHORT = 16
RWKV_PARALLEL = 4
S5_FOLD = SUBLANES // 2
NEW_PAD = 16


def _cparams(sem):
    return pltpu.CompilerParams(dimension_semantics=sem, vmem_limit_bytes=VMEM_LIMIT)


def _bdot(a, b):
    return jnp.dot(a.astype(BF16), b.astype(BF16), preferred_element_type=F32)


def _bdot_nt(a, b):
    return lax.dot_general(a.astype(BF16), b.astype(BF16), (((1,), (1,)), ((), ())),
                           preferred_element_type=F32)


def _bdot_tn(a, b):
    return lax.dot_general(a.astype(BF16), b.astype(BF16), (((0,), (0,)), ((), ())),
                           preferred_element_type=F32)


def _sigmoid(x):
    return 1.0 / (1.0 + jnp.exp(-x))


def _ada_body(c_ref, w_ref, b_ref, o_ref):
    c = c_ref[...]
    cond = c * _sigmoid(c)
    w = w_ref[...]
    c_hi, w_hi = cond.astype(BF16), w.astype(BF16)
    c_lo = (cond - c_hi.astype(F32)).astype(BF16)
    w_lo = (w - w_hi.astype(F32)).astype(BF16)
    dot = lambda a, b: jnp.dot(a, b, preferred_element_type=F32)
    o_ref[...] = dot(c_hi, w_hi) + (dot(c_hi, w_lo) + dot(c_lo, w_hi)) + b_ref[...]


def _ada_mod(c, w_ada, b_ada):
    depth, d, n = w_ada.shape
    rows = c.shape[0]
    tn = 1536
    return pl.pallas_call(
        _ada_body,
        grid=(depth, n // tn),
        in_specs=[pl.BlockSpec((rows, d), lambda l, j: (0, 0)),
                  pl.BlockSpec((None, d, tn), lambda l, j: (l, 0, j)),
                  pl.BlockSpec((None, 1, tn), lambda l, j: (l, 0, j))],
        out_specs=pl.BlockSpec((None, rows, tn), lambda l, j: (l, 0, j)),
        out_shape=jax.ShapeDtypeStruct((depth, rows, n), F32),
        compiler_params=_cparams(("arbitrary", "arbitrary")),
        name="ada_mod",
    )(c, w_ada, b_ada.reshape(depth, 1, n))


def _mod_spec(mod, tm, rows_per_batch, d):
    if mod.ndim == 3:
        tiles = rows_per_batch // tm
        return pl.BlockSpec((None, 1, d), lambda i, *_: (i // tiles, 0, 0))
    return pl.BlockSpec((tm, d), lambda i, *_: (i, 0))


def _norm_mod(x, g, sc, sh):
    ms = jnp.mean(x * x, axis=-1, keepdims=True)
    h = (x * lax.rsqrt(ms + RMS_EPS)) * g
    return h * (1.0 + sc) + sh


def _normmod_mm_body(x_ref, g_ref, sc_ref, sh_ref, w_ref, *o_refs, splits, col_chunk):
    hb = _norm_mod(x_ref[...], g_ref[...], sc_ref[...], sh_ref[...]).astype(BF16)
    for o_ref, (c0, c1) in zip(o_refs, splits):
        for s in range(c0, c1, col_chunk):
            e = min(s + col_chunk, c1)
            o_ref[:, s - c0:e - c0] = jnp.dot(hb, w_ref[:, s:e], preferred_element_type=F32)


def _normmod_mm(x2, g, sc, sh, w_bf16, splits, tm, rows_per_batch):
    m, d = x2.shape
    n = w_bf16.shape[1]
    outs = tuple(jax.ShapeDtypeStruct((m, c1 - c0), F32) for c0, c1 in splits)
    return pl.pallas_call(
        functools.partial(_normmod_mm_body, splits=splits, col_chunk=512),
        grid=(m // tm,),
        in_specs=[pl.BlockSpec((tm, d), lambda i: (i, 0)),
                  pl.BlockSpec((1, d), lambda i: (0, 0)),
                  _mod_spec(sc, tm, rows_per_batch, d),
                  _mod_spec(sh, tm, rows_per_batch, d),
                  pl.BlockSpec((d, n), lambda i: (0, 0))],
        out_specs=tuple(pl.BlockSpec((tm, c1 - c0), lambda i: (i, 0)) for c0, c1 in splits),
        out_shape=outs,
        compiler_params=_cparams(("arbitrary",)),
        name="normmod_mm",
    )(x2, g.reshape(1, d), sc, sh, w_bf16)


def _mix_mlp_body(*refs, n_in, final_norm):
    x_ref, gm_ref = refs[0], refs[1]
    y_refs = refs[2:2 + n_in]
    w_refs = refs[2 + n_in:2 + 2 * n_in]
    (g_ref, sc_ref, sh_ref, gate_ref, wu_ref, wd_ref, gf_ref, o_ref,
     hb_ref, acc_ref, x1_ref) = refs[2 + 2 * n_in:]
    f = pl.program_id(1)

    @pl.when(f == 0)
    def _():
        mix = None
        for y_ref, w_ref in zip(y_refs, w_refs):
            p = jnp.dot(y_ref[...].astype(BF16), w_ref[...], preferred_element_type=F32)
            mix = p if mix is None else mix + p
        x1 = x_ref[...] + gm_ref[...] * mix
        x1_ref[...] = x1
        hb_ref[...] = _norm_mod(x1, g_ref[...], sc_ref[...], sh_ref[...]).astype(BF16)
        acc_ref[...] = jnp.zeros_like(acc_ref)

    up = jnp.dot(hb_ref[...], wu_ref[...], preferred_element_type=F32)
    act = jnp.square(jnp.maximum(up, 0.0)).astype(BF16)
    acc_ref[...] += jnp.dot(act, wd_ref[...], preferred_element_type=F32)

    @pl.when(f == pl.num_programs(1) - 1)
    def _():
        xn = x1_ref[...] + gate_ref[...] * acc_ref[...]
        if final_norm:
            ms = jnp.mean(xn * xn, axis=-1, keepdims=True)
            xn = (xn * lax.rsqrt(ms + RMS_EPS)) * gf_ref[...]
        o_ref[...] = xn


def _mix_mlp(x2, gate_m, ys, ws_bf16, g, sc, sh, gate, wu_bf16, wd_bf16, gfinal, tm, tf,
             rows_per_batch, final_norm):
    m, d = x2.shape
    ff = wu_bf16.shape[1]
    mod = lambda a: _mod_spec(a, tm, rows_per_batch, d)
    row = pl.BlockSpec((1, d), lambda i, f: (0, 0))
    in_specs = [pl.BlockSpec((tm, d), lambda i, f: (i, 0)), mod(gate_m)]
    in_specs += [pl.BlockSpec((tm, y.shape[1]), lambda i, f: (i, 0)) for y in ys]
    in_specs += [pl.BlockSpec(w.shape, lambda i, f: (0, 0)) for w in ws_bf16]
    in_specs += [row, mod(sc), mod(sh), mod(gate),
                 pl.BlockSpec((d, tf), lambda i, f: (0, f)),
                 pl.BlockSpec((tf, d), lambda i, f: (f, 0)), row]
    return pl.pallas_call(
        functools.partial(_mix_mlp_body, n_in=len(ys), final_norm=final_norm),
        grid=(m // tm, ff // tf),
        in_specs=in_specs,
        out_specs=pl.BlockSpec((tm, d), lambda i, f: (i, 0)),
        out_shape=jax.ShapeDtypeStruct((m, d), F32),
        scratch_shapes=[pltpu.VMEM((tm, d), BF16), pltpu.VMEM((tm, d), F32),
                        pltpu.VMEM((tm, d), F32)],
        compiler_params=_cparams(("arbitrary", "arbitrary")),
        name="mix_mlp",
    )(x2, gate_m, *ys, *ws_bf16, g.reshape(1, d), sc, sh, gate, wu_bf16, wd_bf16,
      gfinal.reshape(1, d))


def _rwkv_project(za, prev_row, mu, w0, w_up, a0, a_up, g_up, d_a):
    row = lax.broadcasted_iota(jnp.int32, za.shape, 0)
    prev = jnp.where(row == 0, prev_row, pltpu.roll(za, 1, axis=0))
    zs = za + mu * (prev - za)
    o_w = 3 * d_a
    o_a = o_w + LORA_W
    o_g = o_a + LORA_A
    lw = -DECAY_SCALE * _sigmoid(w0 + _bdot(jnp.tanh(zs[:, o_w:o_a]), w_up))
    a = _sigmoid(a0 + _bdot(zs[:, o_a:o_g], a_up))
    g = _bdot(_sigmoid(zs[:, o_g:o_g + LORA_G]), g_up)
    return zs[:, 0:d_a], zs[:, d_a:2 * d_a], zs[:, 2 * d_a:3 * d_a], lw, a, g


def _segsum(x, ones_blk):
    hi = x.astype(BF16)
    lo = (x - hi.astype(F32)).astype(BF16)
    w = ones_blk.shape[0]
    parts = [jnp.dot(hi[:, c0:c0 + w], ones_blk, preferred_element_type=F32)
             + jnp.dot(lo[:, c0:c0 + w], ones_blk, preferred_element_type=F32)
             for c0 in range(0, x.shape[1], w)]
    return jnp.concatenate(parts, axis=1)


def _rwkv_prepare(r, k, v, lw, a, kkw, kaw, valid, tri_incl, ones_blk):
    c = r.shape[0]
    kk = k * kkw
    kk = kk / jnp.maximum(jnp.sqrt(_segsum(kk * kk, ones_blk)), 1e-12)
    k2 = k * (1.0 + (a - 1.0) * kaw)
    b = kk * a
    if valid is not None:
        zero = lambda x: jnp.where(valid, x, 0.0)
        lw, kk, k2, b, v, r = (zero(x) for x in (lw, kk, k2, b, v, r))
    p1 = lw.astype(BF16)
    d1 = lw - p1.astype(F32)
    p2 = d1.astype(BF16)
    p3 = (d1 - p2.astype(F32)).astype(BF16)
    tri = tri_incl.astype(BF16)
    cum = sum(jnp.dot(tri, p, preferred_element_type=F32) for p in (p1, p2, p3))
    tot = cum[c - 1:c, :]
    e_in = jnp.exp(cum)
    e_out = jnp.exp(-cum)
    e_end = jnp.exp(tot - cum)
    return dict(r=r, k2=k2, v=v, kap_t=kk * jnp.exp(cum - lw), r_t=r * e_in, b_t=b * e_out,
                k_t=k2 * e_out, b_h=b * e_end, k_h=k2 * e_end, e_tot=jnp.exp(tot))


def _rwkv_local(kap_t, r_t, b_t, k_t, b_h, k_h, v, tri_incl, tri_strict, eye, n_doubling):
    c = kap_t[0].shape[0]
    each = lambda f, *ls: [f(*xs) for xs in zip(*ls)]

    rhs = each(lambda x, y: jnp.concatenate([x, y], axis=0), b_t, k_t)
    gk = each(_bdot_nt, kap_t, rhs)
    gr = each(_bdot_nt, r_t, rhs)
    n1 = each(lambda x: jnp.where(tri_strict, x[:, :c], 0.0), gk)
    n2 = each(lambda x: jnp.where(tri_strict, x[:, c:], 0.0), gk)
    m1 = each(lambda x: jnp.where(tri_incl > 0, x[:, :c], 0.0), gr)
    m2 = each(lambda x: jnp.where(tri_incl > 0, x[:, c:], 0.0), gr)

    x = each(lambda n: -n, n1)
    tinv = each(lambda x_: eye + x_, x)
    for _ in range(n_doubling):
        x = each(_bdot, x, x)
        tinv = each(lambda t_, x_: t_ + _bdot(t_, x_), tinv, x)

    nv = each(lambda n, m_, v_: _bdot(jnp.concatenate([n, m_], axis=0), v_), n2, m2, v)
    ty = each(lambda t_, kp, nv_: _bdot(t_, jnp.concatenate([kp, nv_[:c]], axis=1)),
              tinv, kap_t, nv)
    m1ty = each(_bdot, m1, ty)
    rq = each(lambda r_, m_: r_ - m_[:, :HEAD_A], r_t, m1ty)
    ol = each(lambda nv_, m_: nv_[c:] - m_[:, HEAD_A:], nv, m1ty)
    tb = each(_bdot_tn, ty, b_h)
    vk = each(_bdot_tn, v, k_h)
    gmat = each(lambda tb_: tb_[:HEAD_A], tb)
    hmat = each(lambda vk_, tb_: vk_ - tb_[HEAD_A:], vk, tb)
    return rq, ol, gmat, hmat


def _rwkv_chunk_body(za_ref, shift_ref, mu_ref, w0_ref, wup_ref, a0_ref, aup_ref, gup_ref,
                     kkw_ref, kaw_ref, rkw_ref, lnw_ref, lnb_ref, s0_ref, y_ref, sout_ref,
                     s_ref, last_ref, *, chunk, n_chunks, n_heads, n_valid, group):
    @pl.when(pl.program_id(1) == 0)
    def _():
        s_ref[...] = s0_ref[...]
        last_ref[...] = shift_ref[...]

    c = chunk
    d_a = n_heads * HEAD_A
    ri = lax.broadcasted_iota(jnp.int32, (c, c), 0)
    ci = lax.broadcasted_iota(jnp.int32, (c, c), 1)
    tri_incl = (ri >= ci).astype(F32)
    tri_strict = ri > ci
    eye = (ri == ci).astype(F32)
    bi = lax.broadcasted_iota(jnp.int32, (MXU_DIM, MXU_DIM), 0) // HEAD_A
    bj = lax.broadcasted_iota(jnp.int32, (MXU_DIM, MXU_DIM), 1) // HEAD_A
    ones_blk = (bi == bj).astype(BF16)
    valid = None
    if n_valid < c:
        valid = lax.broadcasted_iota(jnp.int32, (c, d_a), 0) < n_valid
    n_doubling = max(int(math.log2(c)) - 1, 0)
    heads = range(n_heads)
    sls = [slice(h * HEAD_A, (h + 1) * HEAD_A) for h in heads]

    n_seq = za_ref.shape[0]

    def one_group(ig, carry):
        rows = [pl.ds(pl.multiple_of((ig * group + gi) * c, c), c) for gi in range(group)]
        pre = []
        for bi in range(n_seq):
            prev_row = last_ref[bi]
            for gi in range(group):
                za = za_ref[bi, rows[gi], :]
                r, k, v, lw, a, g = _rwkv_project(za, prev_row, mu_ref[...], w0_ref[...],
                                                  wup_ref[...], a0_ref[...], aup_ref[...],
                                                  gup_ref[...], d_a)
                prev_row = za[c - 1:c, :]
                pre.append(dict(_rwkv_prepare(r, k, v, lw, a, kkw_ref[...], kaw_ref[...], valid,
                                              tri_incl, ones_blk), g=g))
            last_ref[bi] = prev_row
        chains = lambda name: [p[name][:, sl] for p in pre for sl in sls]
        rq, ol, gmat, hmat = _rwkv_local(
            chains('kap_t'), chains('r_t'), chains('b_t'), chains('k_t'), chains('b_h'),
            chains('k_h'), chains('v'), tri_incl, tri_strict, eye, n_doubling)
        s = [[s_ref[bi, h] for h in heads] for bi in range(n_seq)]
        for gi in range(group):
            for bi in range(n_seq):
                p = pre[bi * group + gi]
                at = lambda xs, h: xs[(bi * group + gi) * n_heads + h]
                o = jnp.concatenate([_bdot_nt(at(rq, h), s[bi][h]) + at(ol, h) for h in heads],
                                    axis=1)
                s[bi] = [s[bi][h] * p['e_tot'][:, sls[h]] - _bdot(s[bi][h], at(gmat, h))
                         + at(hmat, h) for h in heads]
                mu = _segsum(o, ones_blk) * (1.0 / HEAD_A)
                dev = o - mu
                var = _segsum(dev * dev, ones_blk) * (1.0 / HEAD_A)
                on = dev * lax.rsqrt(var + GN_EPS) * lnw_ref[...] + lnb_ref[...]
                bonus = _segsum(p['r'] * p['k2'] * rkw_ref[...], ones_blk) * p['v']
                y_ref[bi, rows[gi], :] = (on + bonus) * p['g']
        for bi in range(n_seq):
            for h in heads:
                s_ref[bi, h] = s[bi][h]
        return carry

    lax.fori_loop(0, n_chunks // group, one_group, 0)

    @pl.when(pl.program_id(1) == pl.num_programs(1) - 1)
    def _():
        sout_ref[...] = s_ref[...]


def _rwkv_chunked(za, shift_prev, mu, w0, w_up, a0, a_up, g_up, k_k, k_a, r_k, lnx_w, lnx_b, s0,
                  tb, chunk, n_valid):
    b, t, dp = za.shape
    d_a = w0.shape[0]
    n_heads = d_a // HEAD_A
    row = lambda n: pl.BlockSpec((1, n), lambda i, j: (0, 0))
    full = lambda w: pl.BlockSpec(w.shape, lambda i, j: (0, 0))
    group = min(RWKV_PARALLEL, tb // chunk)
    n_seq = max(d for d in range(1, RWKV_PARALLEL // group + 1) if b % d == 0)
    st = pl.BlockSpec((n_seq, n_heads, HEAD_A, HEAD_A), lambda i, j: (i, 0, 0, 0))
    return pl.pallas_call(
        functools.partial(_rwkv_chunk_body, chunk=chunk, n_chunks=tb // chunk,
                          n_heads=n_heads, n_valid=n_valid, group=group),
        grid=(b // n_seq, t // tb),
        in_specs=[pl.BlockSpec((n_seq, tb, dp), lambda i, j: (i, j, 0)),
                  pl.BlockSpec((n_seq, 1, dp), lambda i, j: (i, 0, 0)),
                  row(dp), row(d_a), full(w_up), row(d_a), full(a_up), full(g_up)]
                 + [row(d_a)] * 5 + [st],
        out_specs=(pl.BlockSpec((n_seq, tb, d_a), lambda i, j: (i, j, 0)), st),
        out_shape=(jax.ShapeDtypeStruct((b, t, d_a), F32),
                   jax.ShapeDtypeStruct((b, n_heads, HEAD_A, HEAD_A), F32)),
        scratch_shapes=[pltpu.VMEM((n_seq, n_heads, HEAD_A, HEAD_A), F32),
                        pltpu.VMEM((n_seq, 1, dp), F32)],
        compiler_params=_cparams(("arbitrary", "arbitrary")),
        name="rwkv_chunked",
    )(za, shift_prev.reshape(b, 1, dp), mu.reshape(1, dp), w0.reshape(1, d_a), w_up,
      a0.reshape(1, d_a), a_up, g_up, k_k.reshape(1, d_a), k_a.reshape(1, d_a),
      r_k.reshape(1, d_a), lnx_w.reshape(1, d_a), lnx_b.reshape(1, d_a), s0)


def _s5_disc_body(lr_ref, li_ref, ldt_ref, bre_ref, bim_ref, ar_ref, ai_ref, bbre_ref, bbim_ref):
    lr, li = lr_ref[...], li_ref[...]
    dt = jnp.exp(ldt_ref[...])
    mag = jnp.exp(lr * dt)
    ar = mag * jnp.cos(li * dt)
    ai = mag * jnp.sin(li * dt)
    den = lr * lr + li * li
    fr = ((ar - 1.0) * lr + ai * li) / den
    fi = (ai * lr - (ar - 1.0) * li) / den
    ar_ref[...] = ar
    ai_ref[...] = ai
    b_re, b_im = bre_ref[...], bim_ref[...]
    bb_re = fr[:, None, :] * b_re - fi[:, None, :] * b_im
    bb_im = fr[:, None, :] * b_im + fi[:, None, :] * b_re
    pr, pi = jnp.ones_like(ar), jnp.zeros_like(ai)
    for i in range(S5_FOLD):
        bbre_ref[i] = pr[:, None, :] * bb_re - pi[:, None, :] * bb_im
        bbim_ref[i] = pr[:, None, :] * bb_im + pi[:, None, :] * bb_re
        pr, pi = pr * ar - pi * ai, pr * ai + pi * ar


def _s5_discretise(lam_re, lam_im, log_dt, b_re, b_im):
    g, p = lam_re.shape
    n = b_re.shape[2]
    bt = lambda x: jnp.swapaxes(x, 1, 2)
    return pl.pallas_call(
        _s5_disc_body,
        out_shape=(jax.ShapeDtypeStruct((g, p), F32), jax.ShapeDtypeStruct((g, p), F32),
                   jax.ShapeDtypeStruct((S5_FOLD, g, n, p), F32),
                   jax.ShapeDtypeStruct((S5_FOLD, g, n, p), F32)),
        name="s5_discretise",
    )(lam_re, lam_im, log_dt.reshape(g, 1), bt(b_re), bt(b_im))


def _gelu_tanh(x):
    c = math.sqrt(2.0 / math.pi)
    return 0.5 * x * (1.0 + jnp.tanh(c * (x + 0.044715 * (x * x * x))))


def _s5_body(u_ref, h0r_ref, h0i_ref, ar_ref, ai_ref, bdre_ref, bdim_ref, cdre_ref, cdim_ref,
             d_ref, wglu_ref, bglu_ref, y_ref, hre_ref, him_ref,
             xr_ref, xi_ref, apr_ref, api_ref, mr_ref, mi_ref, cr_ref, ci_ref, *, last_row):
    j = pl.program_id(1)
    rows, width = xr_ref.shape
    half_u = u_ref.shape[1] // 2
    half_x = width // 2

    ar, ai = ar_ref[...], ai_ref[...]
    a2r, a2i = ar * ar - ai * ai, 2.0 * ar * ai
    a4r, a4i = a2r * a2r - a2i * a2i, 2.0 * a2r * a2i

    @pl.when(j == 0)
    def _():
        cr_ref[...] = h0r_ref[...]
        ci_ref[...] = h0i_ref[...]
        pr, pi = ar, ai
        for s in range(SUBLANES):
            apr_ref[s:s + 1, :] = pr
            api_ref[s:s + 1, :] = pi
            pr, pi = pr * ar - pi * ai, pr * ai + pi * ar
        sub8 = lax.broadcasted_iota(jnp.int32, (SUBLANES, width), 0)
        mr_ref[...] = jnp.where(sub8 >= S5_FOLD, a4r, 0.0)
        mi_ref[...] = jnp.where(sub8 >= S5_FOLD, a4i, 0.0)

    u = u_ref[...]
    n_parts = bdre_ref.shape[0]
    part_x = width // n_parts
    sub = lax.broadcasted_iota(jnp.int32, (rows, LANES), 0) % SUBLANES
    for pt in range(n_parts):
        up = u[:, pt * LANES:(pt + 1) * LANES]
        lags = [up] + [jnp.where(sub >= i, pltpu.roll(up, i, axis=0), 0.0)
                       for i in range(1, S5_FOLD)]
        lhs = jnp.concatenate(lags, axis=1).astype(BF16)
        cols = slice(pt * part_x, (pt + 1) * part_x)
        xr_ref[:, cols] = jnp.dot(lhs, bdre_ref[pt], preferred_element_type=F32)
        xi_ref[:, cols] = jnp.dot(lhs, bdim_ref[pt], preferred_element_type=F32)

    lane_chunk = 4 * LANES
    grouped = lambda x: x.reshape(rows // SUBLANES, SUBLANES, lane_chunk)
    for c0 in range(0, width, lane_chunk):
        cols = slice(c0, c0 + lane_chunk)
        xr, xi = xr_ref[:, cols], xi_ref[:, cols]
        pr, pi = mr_ref[:, cols][None], mi_ref[:, cols][None]
        sr = grouped(pltpu.roll(xr, S5_FOLD, axis=0))
        si = grouped(pltpu.roll(xi, S5_FOLD, axis=0))
        xr_ref[:, cols] = (grouped(xr) + (pr * sr - pi * si)).reshape(rows, lane_chunk)
        xi_ref[:, cols] = (grouped(xi) + (pr * si + pi * sr)).reshape(rows, lane_chunk)

    def group(n, carry):
        cr, ci = carry
        r0 = pl.multiple_of(n * SUBLANES, SUBLANES)
        rs = pl.ds(r0, SUBLANES)
        apr, api = apr_ref[...], api_ref[...]
        hr = xr_ref[rs, :] + apr * cr - api * ci
        hi = xi_ref[rs, :] + apr * ci + api * cr
        xr_ref[rs, :] = hr
        xi_ref[rs, :] = hi
        return hr[SUBLANES - 1:SUBLANES, :], hi[SUBLANES - 1:SUBLANES, :]

    cr, ci = lax.fori_loop(0, rows // SUBLANES, group, (cr_ref[...], ci_ref[...]))
    cr_ref[...] = cr
    ci_ref[...] = ci

    @pl.when(j == pl.num_programs(1) - 1)
    def _():
        hre_ref[...] = xr_ref[last_row:last_row + 1, :]
        him_ref[...] = xi_ref[last_row:last_row + 1, :]

    for hf in range(2):
        cols = slice(hf * half_x, (hf + 1) * half_x)
        ucols = slice(hf * half_u, (hf + 1) * half_u)
        y = (jnp.dot(xr_ref[:, cols].astype(BF16), cdre_ref[hf], preferred_element_type=F32)
             - jnp.dot(xi_ref[:, cols].astype(BF16), cdim_ref[hf], preferred_element_type=F32)
             + d_ref[:, ucols] * u[:, ucols])
        y_ref[:, ucols] = _gelu_tanh(y)
    y = y_ref[...]
    z = jnp.dot(y.astype(BF16), wglu_ref[...], preferred_element_type=F32) + bglu_ref[...]
    y_ref[...] = y * _sigmoid(z)


def _block_diag_parts(w_gab, transpose, n_parts=2):
    g = w_gab.shape[0]
    hg = g // n_parts
    eye = jnp.eye(hg, dtype=w_gab.dtype)
    parts = []
    for pt in range(n_parts):
        w = w_gab[pt * hg:(pt + 1) * hg]
        if transpose:
            w = jnp.swapaxes(w, 1, 2)
        a, b = w.shape[1], w.shape[2]
        parts.append(jnp.einsum('gab,gh->gahb', w, eye).reshape(hg * a, hg * b))
    return jnp.stack(parts)


def _s5_mix(u, h0_re, h0_im, ar, ai, bbt_re, bbt_im, c_re, c_im, d, w_glu, b_glu, rows, n_valid):
    b, t, d_b = u.shape
    g, p = ar.shape
    width = g * p
    n_parts = d_b // LANES
    lagged = lambda w: jnp.concatenate(
        [_block_diag_parts(w[i], False, n_parts) for i in range(S5_FOLD)], axis=1).astype(BF16)
    bd_re, bd_im = lagged(bbt_re), lagged(bbt_im)
    cd_re = _block_diag_parts(c_re, True).astype(BF16)
    cd_im = _block_diag_parts(c_im, True).astype(BF16)
    n_blocks = t // rows
    last_row = (n_valid - 1) - (n_blocks - 1) * rows
    full = lambda x: pl.BlockSpec(x.shape, lambda i, j: (0,) * x.ndim)
    st = pl.BlockSpec((None, 1, width), lambda i, j: (i, 0, 0))
    args = (u, h0_re.reshape(b, 1, width), h0_im.reshape(b, 1, width),
            ar.reshape(1, width), ai.reshape(1, width), bd_re, bd_im, cd_re, cd_im,
            d.reshape(1, d_b), w_glu.astype(BF16), b_glu.reshape(1, d_b))
    return pl.pallas_call(
        functools.partial(_s5_body, last_row=last_row),
        grid=(b, n_blocks),
        in_specs=[pl.BlockSpec((None, rows, d_b), lambda i, j: (i, j, 0)), st, st]
                 + [full(x) for x in args[3:]],
        out_specs=(pl.BlockSpec((None, rows, d_b), lambda i, j: (i, j, 0)), st, st),
        out_shape=(jax.ShapeDtypeStruct((b, t, d_b), F32),
                   jax.ShapeDtypeStruct((b, 1, width), F32),
                   jax.ShapeDtypeStruct((b, 1, width), F32)),
        scratch_shapes=[pltpu.VMEM((rows, width), F32), pltpu.VMEM((rows, width), F32),
                        pltpu.VMEM((SUBLANES, width), F32), pltpu.VMEM((SUBLANES, width), F32),
                        pltpu.VMEM((SUBLANES, width), F32), pltpu.VMEM((SUBLANES, width), F32),
                        pltpu.VMEM((1, width), F32), pltpu.VMEM((1, width), F32)],
        compiler_params=_cparams(("arbitrary", "arbitrary")),
        name="s5_mix",
    )(*args)


def _lambda_full(lq1, lk1, lq2, lk2):
    s1 = jnp.sum(lq1 * lk1, axis=-1, keepdims=True)
    s2 = jnp.sum(lq2 * lk2, axis=-1, keepdims=True)
    return jnp.exp(s1) - jnp.exp(s2) + LAMBDA_INIT


def _sub_ln(o, subln):
    ms = jnp.mean(o * o, axis=-1, keepdims=True)
    return o * lax.rsqrt(ms + RMS_EPS) * subln * (1.0 - LAMBDA_INIT)


def _attn_prompt_body(q_ref, k_ref, v_ref, lq1_ref, lk1_ref, lq2_ref, lk2_ref, subln_ref,
                      o_ref, kb_ref, vt_ref, m_ref, l_ref, acc_ref, sa_ref, sb_ref, *, tile,
                      n_tiles):
    halves = range(2)
    for c in range(n_tiles):
        rows = slice(c * tile, (c + 1) * tile)
        kb_ref[rows, :] = k_ref[rows, :].astype(BF16)
        vt_ref[c] = v_ref[rows, :].T.astype(BF16)
    lam = _lambda_full(lq1_ref[...], lk1_ref[...], lq2_ref[...], lk2_ref[...])
    s_even, s_odd = sa_ref, sb_ref

    wide = 2 * tile
    all_q = slice(0, wide)
    late_q = slice(tile, wide)

    def q_tile(qj, carry):
        q_rows = pl.ds(pl.multiple_of(qj * wide, wide), wide)
        q = q_ref[q_rows, :] * (SCALE_C * math.log2(math.e))
        lane = lax.broadcasted_iota(jnp.int32, q.shape, 1)
        qs = [jnp.where((lane >= HALF_C) == bool(c), q, 0.0).astype(BF16) for c in halves]
        m_ref[...] = jnp.full_like(m_ref, NEG_INF)
        l_ref[...] = jnp.zeros_like(l_ref)
        acc_ref[...] = jnp.zeros_like(acc_ref)

        def scores(kj, slot, cols):
            k0 = kj * tile if isinstance(kj, int) else pl.multiple_of(kj * tile, tile)
            kb = kb_ref[pl.ds(k0, tile), :]
            for c in halves:
                slot[c, :, cols] = lax.dot_general(kb, qs[c][cols], (((1,), (1,)), ((), ())),
                                                   preferred_element_type=F32)

        def kv_step(kj, slot, diagonal, cols):
            st = [slot[c, :, cols] for c in halves]
            if diagonal:
                krow = lax.broadcasted_iota(jnp.int32, st[0].shape, 0)
                qcol = lax.broadcasted_iota(jnp.int32, st[0].shape, 1)
                st = [jnp.where(krow <= qcol, x, NEG_INF) for x in st]
            m_prev = [m_ref[c, :, cols] for c in halves]
            m_new = [jnp.maximum(m_prev[c], jnp.max(st[c], axis=0, keepdims=True))
                     for c in halves]
            alpha = [jnp.exp2(m_prev[c] - m_new[c]) for c in halves]
            p = [jnp.exp2(st[c] - m_new[c]) for c in halves]
            vt = vt_ref[kj]
            pv = [jnp.dot(vt, p[c].astype(BF16), preferred_element_type=F32) for c in halves]
            for c in halves:
                l_ref[c, :, cols] = alpha[c] * l_ref[c, :, cols] + jnp.sum(p[c], axis=0,
                                                                           keepdims=True)
                acc_ref[c, :, cols] = alpha[c] * acc_ref[c, :, cols] + pv[c]
                m_ref[c, :, cols] = m_new[c]

        scores(0, s_even, all_q)

        def two_steps(i, c2):
            scores(2 * i + 1, s_odd, all_q)
            kv_step(2 * i, s_even, False, all_q)
            scores(2 * i + 2, s_even, all_q)
            kv_step(2 * i + 1, s_odd, False, all_q)
            return c2

        lax.fori_loop(0, qj, two_steps, 0)
        scores(2 * qj + 1, s_odd, late_q)
        kv_step(2 * qj, s_even, True, all_q)
        kv_step(2 * qj + 1, s_odd, True, late_q)

        ot = acc_ref[0] / l_ref[0] - lam * (acc_ref[1] / l_ref[1])
        ms = jnp.mean(ot * ot, axis=0, keepdims=True)
        ot = ot * lax.rsqrt(ms + RMS_EPS) * subln_ref[...] * (1.0 - LAMBDA_INIT)
        o_ref[q_rows, :] = ot.T
        return carry

    lax.fori_loop(0, n_tiles // 2, q_tile, 0)


def _attn_prompt(q, k, v, lq1, lk1, lq2, lk2, subln, tile):
    b, t, d = q.shape
    n_heads = d // HEAD_C
    n_tiles = t // tile
    assert t % (2 * tile) == 0, (t, tile)
    small = lambda n: pl.BlockSpec((1, n), lambda i, h: (0, 0))
    per_head = pl.BlockSpec((None, t, HEAD_C), lambda i, h: (i, 0, h))
    return pl.pallas_call(
        functools.partial(_attn_prompt_body, tile=tile, n_tiles=n_tiles),
        grid=(b, n_heads),
        in_specs=[per_head, per_head, per_head,
                  small(HALF_C), small(HALF_C), small(HALF_C), small(HALF_C),
                  pl.BlockSpec((HEAD_C, 1), lambda i, h: (0, 0))],
        out_specs=per_head,
        out_shape=jax.ShapeDtypeStruct((b, t, d), F32),
        scratch_shapes=[pltpu.VMEM((t, HEAD_C), BF16), pltpu.VMEM((n_tiles, HEAD_C, tile), BF16),
                        pltpu.VMEM((2, 1, 2 * tile), F32), pltpu.VMEM((2, 1, 2 * tile), F32),
                        pltpu.VMEM((2, HEAD_C, 2 * tile), F32),
                        pltpu.VMEM((2, tile, 2 * tile), F32),
                        pltpu.VMEM((2, tile, 2 * tile), F32)],
        compiler_params=_cparams(("arbitrary", "arbitrary")),
        name="attn_prompt",
    )(q, k, v, lq1.reshape(1, -1), lk1.reshape(1, -1), lq2.reshape(1, -1), lk2.reshape(1, -1),
      subln.reshape(-1, 1))


def _attn_sample_body(pt_ref, qrep_ref, knew_ref, vnew_ref, *rest, pages_per_step, n_heads,
                      t_new, page):
    kp_refs = rest[:pages_per_step]
    vp_refs = rest[pages_per_step:2 * pages_per_step]
    lq1_ref, lk1_ref, lq2_ref, lk2_ref, subln_ref = rest[2 * pages_per_step:2 * pages_per_step + 5]
    o_ref = rest[2 * pages_per_step + 5]
    qw_ref, m_ref, l_ref, acc_ref = rest[2 * pages_per_step + 6:]
    j = pl.program_id(1)
    pairs = range(n_heads // 2)
    q_rows = qrep_ref.shape[1]

    @pl.when(j == 0)
    def _():
        qr = qrep_ref[...]
        row = lax.broadcasted_iota(jnp.int32, qr.shape, 1)
        lane = lax.broadcasted_iota(jnp.int32, qr.shape, 2)
        qm = jnp.where(lane // HALF_C == row // t_new, qr * SCALE_C, 0.0)
        zero = jnp.zeros((q_rows, HEAD_C), F32)
        for pr in pairs:
            qw_ref[pr] = jnp.concatenate(
                [jnp.concatenate([qm[2 * pr], zero], axis=1),
                 jnp.concatenate([zero, qm[2 * pr + 1]], axis=1)], axis=0).astype(BF16)
        m_ref[...] = jnp.full_like(m_ref, NEG_INF)
        l_ref[...] = jnp.zeros_like(l_ref)
        acc_ref[...] = jnp.zeros_like(acc_ref)

    def pair_rows(ref, pr):
        return jnp.concatenate([ref[pl.ds(2 * pr + i, page, stride=n_heads), :] for i in range(2)],
                               axis=1).astype(BF16)

    def update(s, values):
        m_prev = [m_ref[pr] for pr in pairs]
        m_new = [jnp.maximum(m_prev[pr], jnp.max(s[pr], axis=-1, keepdims=True)) for pr in pairs]
        alpha = [jnp.exp(m_prev[pr] - m_new[pr]) for pr in pairs]
        p = [jnp.exp(s[pr] - m_new[pr]) for pr in pairs]
        pv = []
        for pr in pairs:
            w = p[pr].shape[1] // len(values[pr])
            acc = None
            for i, vb in enumerate(values[pr]):
                d = jnp.dot(p[pr][:, i * w:(i + 1) * w].astype(BF16), vb,
                            preferred_element_type=F32)
                acc = d if acc is None else acc + d
            pv.append(acc)
        for pr in pairs:
            l_ref[pr] = alpha[pr] * l_ref[pr] + jnp.sum(p[pr], axis=-1, keepdims=True)
            acc_ref[pr] = alpha[pr] * acc_ref[pr] + pv[pr]
            m_ref[pr] = m_new[pr]

    s = [jnp.concatenate(
            [lax.dot_general(qw_ref[pr], pair_rows(kp, pr), (((1,), (1,)), ((), ())),
                             preferred_element_type=F32) for kp in kp_refs], axis=1)
         for pr in pairs]
    update(s, [[pair_rows(vp, pr) for vp in vp_refs] for pr in pairs])

    @pl.when(j == pl.num_programs(1) - 1)
    def _():
        both = lambda ref, pr: jnp.concatenate([ref[2 * pr], ref[2 * pr + 1]], axis=1).astype(BF16)
        sn = []
        for pr in pairs:
            x = lax.dot_general(qw_ref[pr], both(knew_ref, pr), (((1,), (1,)), ((), ())),
                                preferred_element_type=F32)
            trow = lax.broadcasted_iota(jnp.int32, x.shape, 0) % t_new
            tcol = lax.broadcasted_iota(jnp.int32, x.shape, 1)
            sn.append(jnp.where(tcol <= trow, x, NEG_INF))
        update(sn, [[both(vnew_ref, pr)] for pr in pairs])
        lam = _lambda_full(lq1_ref[...], lk1_ref[...], lq2_ref[...], lk2_ref[...])
        for pr in pairs:
            on = acc_ref[pr] / l_ref[pr]
            for i in range(2):
                h = 2 * pr + i
                r0 = i * q_rows
                oh = on[r0:r0 + 2 * t_new, i * HEAD_C:(i + 1) * HEAD_C]
                o = oh[0:t_new] - lam * oh[t_new:2 * t_new]
                o_ref[:, h * HEAD_C:(h + 1) * HEAD_C] = _sub_ln(o, subln_ref[...])


def _attn_sample(q, k_new, v_new, cache_k, cache_v, page_table, lq1, lk1, lq2, lk2, subln,
                 pages_per_step):
    db, t_new, d = q.shape
    n_heads = d // HEAD_C
    n_pool, page = cache_k.shape[0], cache_k.shape[1]
    n_pages = page_table.shape[1]
    ck = cache_k.reshape(n_pool, page * n_heads, HEAD_C)
    cv = cache_v.reshape(n_pool, page * n_heads, HEAD_C)
    by_head = lambda x: jnp.swapaxes(x.reshape(db, t_new, n_heads, HEAD_C), 1, 2)
    q_rows = NEW_PAD
    qrep = jnp.tile(by_head(q), (1, 1, q_rows // t_new, 1))
    pad = ((0, 0), (0, 0), (0, NEW_PAD - t_new), (0, 0))
    knew = jnp.pad(by_head(k_new), pad)
    vnew = jnp.pad(by_head(v_new), pad)
    pt = page_table.reshape(-1)

    def page_spec(i):
        return pl.BlockSpec((None, page * n_heads, HEAD_C),
                            lambda b, j, pt_ref: (pt_ref[b * n_pages + j * pages_per_step + i], 0, 0))

    small = lambda n: pl.BlockSpec((1, n), lambda b, j, pt_ref: (0, 0))
    per_b = lambda r: pl.BlockSpec((None, n_heads, r, HEAD_C), lambda b, j, pt_ref: (b, 0, 0, 0))
    grid_spec = pltpu.PrefetchScalarGridSpec(
        num_scalar_prefetch=1,
        grid=(db, n_pages // pages_per_step),
        in_specs=[per_b(q_rows), per_b(NEW_PAD), per_b(NEW_PAD)]
                 + [page_spec(i) for i in range(pages_per_step)] * 2
                 + [small(HALF_C)] * 4 + [small(HEAD_C)],
        out_specs=pl.BlockSpec((None, t_new, d), lambda b, j, pt_ref: (b, 0, 0)),
        scratch_shapes=[pltpu.VMEM((n_heads // 2, 2 * q_rows, 2 * HEAD_C), BF16),
                        pltpu.VMEM((n_heads // 2, 2 * q_rows, 1), F32),
                        pltpu.VMEM((n_heads // 2, 2 * q_rows, 1), F32),
                        pltpu.VMEM((n_heads // 2, 2 * q_rows, 2 * HEAD_C), F32)])
    return pl.pallas_call(
        functools.partial(_attn_sample_body, pages_per_step=pages_per_step, n_heads=n_heads,
                          t_new=t_new, page=page),
        grid_spec=grid_spec,
        out_shape=jax.ShapeDtypeStruct((db, t_new, d), F32),
        compiler_params=_cparams(("arbitrary", "arbitrary")),
        name="attn_sample",
    )(pt, qrep, knew, vnew, *([ck] * pages_per_step), *([cv] * pages_per_step),
      lq1.reshape(1, -1), lk1.reshape(1, -1), lq2.reshape(1, -1), lk2.reshape(1, -1),
      subln.reshape(1, -1))


def _pad_time(x, t_pad):
    return jnp.pad(x, ((0, 0), (0, t_pad - x.shape[1]), (0, 0)))


def _trunk(x, mods, shift0, wkv0, sre0, sim0, attend, W, per_row_mod):
    b, t, d = x.shape
    m = b * t
    rows_mod = m if per_row_mod else t
    tm = min(512, rows_mod)
    tm_mlp = min(1024, rows_mod)
    d_a = W['rwkv_w0'].shape[0]
    d_ap = W['rwkv_mu'].shape[0]

    def mod_arg(v):
        if per_row_mod:
            return jnp.repeat(v, t, axis=0)
        return v.reshape(b, 1, d)

    x2 = x.reshape(m, d)

    sh_m, sc_m, g_m, sh_f, sc_f, g_f = (mod_arg(v) for v in mods[0])
    za2, u2 = _normmod_mm(x2, W['norm_mix'][0], sc_m, sh_m, W['w_in_ab_bf16'],
                          ((0, d_ap), (d_ap, W['w_in_ab_bf16'].shape[1])), tm, t)
    za = za2.reshape(b, t, d_ap)
    u = u2.reshape(b, t, -1)
    shift1 = za[:, -1]

    chunk = RWKV_CHUNK if t >= RWKV_CHUNK else RWKV_CHUNK_SHORT
    t_pad = -(-t // chunk) * chunk
    if t_pad != t:
        za_p, u_p = _pad_time(za, t_pad), _pad_time(u, t_pad)
    else:
        za_p, u_p = za, u
    tb = min(512, t_pad)
    y_a, wkv1 = _rwkv_chunked(za_p, shift0, W['rwkv_mu'], W['rwkv_w0'], W['rwkv_w_up'],
                              W['rwkv_a0'], W['rwkv_a_up'], W['rwkv_g_up'], W['rwkv_k_k'],
                              W['rwkv_k_a'], W['rwkv_r_k'], W['rwkv_lnx_w'], W['rwkv_lnx_b'],
                              wkv0, tb, chunk, chunk if t_pad == t else t)
    s5_rows = min(512, t_pad)
    y_b, sre1, sim1 = _s5_mix(u_p, sre0, sim0, W['s5_ar'], W['s5_ai'], W['s5_bbt_re'],
                              W['s5_bbt_im'], W['s5_c_re'], W['s5_c_im'], W['s5_d'],
                              W['s5_w_glu'], W['s5_b_glu'], s5_rows, t)
    y_a2 = y_a[:, :t].reshape(m, d_a)
    y_b2 = y_b[:, :t].reshape(m, -1)
    wo = W['w_out_bf16'][0]
    x2 = _mix_mlp(x2, g_m, (y_a2, y_b2), (wo[:d_a], wo[d_a:]), W['norm_mlp'][0], sc_f, sh_f, g_f,
                  W['w_up_bf16'][0], W['w_down_bf16'][0], W['norm_f'], tm_mlp, 1024, t, False)

    sh_m, sc_m, g_m, sh_f, sc_f, g_f = (mod_arg(v) for v in mods[1])
    q2, k2, v2 = _normmod_mm(x2, W['norm_mix'][1], sc_m, sh_m, W['diff_w_qkv_bf16'],
                             ((0, d), (d, 2 * d), (2 * d, 3 * d)), tm, t)
    q3, k3, v3 = (z.reshape(b, t, d) for z in (q2, k2, v2))
    o = attend(q3, k3, v3)
    y2 = _mix_mlp(x2, g_m, (o.reshape(m, d),), (W['w_out_bf16'][1],), W['norm_mlp'][1], sc_f, sh_f,
                  g_f, W['w_up_bf16'][1], W['w_down_bf16'][1], W['norm_f'], tm_mlp, 1024, t, True)

    n_heads_c = d // HEAD_C
    g_b, p_b = W['s5_ar'].shape
    return (y2.reshape(b, t, d), shift1, wkv1, sre1.reshape(b, g_b, p_b), sim1.reshape(b, g_b, p_b),
            k3.reshape(b, t, n_heads_c, HEAD_C), v3.reshape(b, t, n_heads_c, HEAD_C))


def kernel(x_prompt, x_sample, state_shift, state_wkv, state_ssm_re, state_ssm_im, cache_k, cache_v, page_table, c_prompt, c_sample, norm_mix, norm_mlp, norm_f, w_ada, b_ada, w_out, w_up, w_down, w_in_ab, rwkv_mu, rwkv_w0, rwkv_w_up, rwkv_a0, rwkv_a_up, rwkv_g_up, rwkv_k_k, rwkv_k_a, rwkv_r_k, rwkv_lnx_w, rwkv_lnx_b, s5_lam_re, s5_lam_im, s5_log_dt, s5_b_re, s5_b_im, s5_c_re, s5_c_im, s5_d, s5_w_glu, s5_b_glu, diff_w_qkv, diff_lq1, diff_lk1, diff_lq2, diff_lk2, diff_subln):
    bp, tp, d = x_prompt.shape
    db, ts, _ = x_sample.shape
    depth = w_ada.shape[0]
    n_heads_a = rwkv_r_k.shape[0]
    g_b, p_b = s5_lam_re.shape

    ar, ai, bbt_re, bbt_im = _s5_discretise(s5_lam_re, s5_lam_im, s5_log_dt, s5_b_re, s5_b_im)
    W = dict(norm_mix=norm_mix, norm_mlp=norm_mlp, norm_f=norm_f,
             w_out_bf16=w_out.astype(BF16), w_up_bf16=w_up.astype(BF16),
             w_down_bf16=w_down.astype(BF16), w_in_ab_bf16=w_in_ab.astype(BF16),
             diff_w_qkv_bf16=diff_w_qkv.astype(BF16),
             rwkv_mu=rwkv_mu, rwkv_w0=rwkv_w0, rwkv_w_up=rwkv_w_up, rwkv_a0=rwkv_a0,
             rwkv_a_up=rwkv_a_up, rwkv_g_up=rwkv_g_up, rwkv_k_k=rwkv_k_k, rwkv_k_a=rwkv_k_a,
             rwkv_r_k=rwkv_r_k.reshape(-1), rwkv_lnx_w=rwkv_lnx_w, rwkv_lnx_b=rwkv_lnx_b,
             s5_ar=ar, s5_ai=ai, s5_bbt_re=bbt_re, s5_bbt_im=bbt_im,
             s5_c_re=s5_c_re, s5_c_im=s5_c_im, s5_d=s5_d.reshape(-1), s5_w_glu=s5_w_glu,
             s5_b_glu=s5_b_glu)

    n_c = bp + db
    rows_c = -(-n_c // SUBLANES) * SUBLANES
    c_all = jnp.pad(jnp.concatenate([c_prompt, c_sample], axis=0), ((0, rows_c - n_c), (0, 0)))
    mod = _ada_mod(c_all, w_ada, b_ada)
    mods_p = [tuple(mod[l, :bp, i * d:(i + 1) * d] for i in range(6)) for l in range(depth)]
    mods_s = [tuple(mod[l, bp:n_c, i * d:(i + 1) * d] for i in range(6)) for l in range(depth)]

    lam_args = (diff_lq1, diff_lk1, diff_lq2, diff_lk2, diff_subln)
    attend_p = lambda q, k, v: _attn_prompt(q, k, v, *lam_args, tile=min(512, tp // 2))
    y_prompt, p_shift, p_wkv, p_re, p_im, p_k, p_v = _trunk(
        x_prompt, mods_p,
        jnp.zeros((bp, rwkv_mu.shape[0]), F32),
        jnp.zeros((bp, n_heads_a, HEAD_A, HEAD_A), F32),
        jnp.zeros((bp, g_b, p_b), F32), jnp.zeros((bp, g_b, p_b), F32),
        attend_p, W, per_row_mod=False)

    attend_s = lambda q, k, v: _attn_sample(q, k, v, cache_k, cache_v, page_table, *lam_args,
                                            pages_per_step=16)
    y_sample, s_shift, s_wkv, s_re, s_im, s_k, s_v = _trunk(
        x_sample, mods_s, state_shift, state_wkv, state_ssm_re, state_ssm_im,
        attend_s, W, per_row_mod=True)

    return (y_prompt, y_sample, p_shift, p_wkv, p_re, p_im, p_k, p_v,
            s_shift, s_wkv, s_re, s_im, s_k, s_v)
```

```python
import functools
import math

import jax
import jax.numpy as jnp
from jax import lax
from jax.experimental import pallas as pl
from jax.experimental.pallas import tpu as pltpu

F32 = jnp.float32
BF16 = jnp.bfloat16

HEAD_A = 64
LORA_W, LORA_A, LORA_G = 64, 64, 128
HEAD_C = 128
HALF_C = HEAD_C // 2
SCALE_C = HALF_C ** -0.5
RMS_EPS = 1e-6
GN_EPS = 64e-5
DECAY_SCALE = math.exp(-0.5)
LAMBDA_INIT = 0.8 - 0.6 * math.exp(-0.3 * 1)
NEG_INF = -1e30

SUBLANES = 8
LANES = 128
MXU_DIM = 256
VMEM_LIMIT = 56 * 1024 * 1024

RWKV_CHUNK = 64
RWKV_CHUNK_SHORT = 16
RWKV_PARALLEL = 4
S5_FOLD = SUBLANES // 2
NEW_PAD = 16

ROW_TILE = 512
MLP_ROW_TILE = 1024
MLP_FF_TILE = 1024
PROJ_COL_CHUNK = 512
ADA_COL_TILE = 1536
RWKV_BLOCK = 512
S5_BLOCK = 512
ATTN_KEY_TILE = 512
PAGES_PER_STEP = 16


def _cparams(sem):
    return pltpu.CompilerParams(dimension_semantics=sem, vmem_limit_bytes=VMEM_LIMIT)


def _bdot(a, b):
    return jnp.dot(a.astype(BF16), b.astype(BF16), preferred_element_type=F32)


def _bdot_nt(a, b):
    return lax.dot_general(a.astype(BF16), b.astype(BF16), (((1,), (1,)), ((), ())),
                           preferred_element_type=F32)


def _bdot_tn(a, b):
    return lax.dot_general(a.astype(BF16), b.astype(BF16), (((0,), (0,)), ((), ())),
                           preferred_element_type=F32)


def _sigmoid(x):
    return 1.0 / (1.0 + jnp.exp(-x))


def _ada_body(c_ref, w_ref, b_ref, o_ref):
    c = c_ref[...]
    cond = c * _sigmoid(c)
    w = w_ref[...]
    c_hi, w_hi = cond.astype(BF16), w.astype(BF16)
    c_lo = (cond - c_hi.astype(F32)).astype(BF16)
    w_lo = (w - w_hi.astype(F32)).astype(BF16)
    dot = lambda a, b: jnp.dot(a, b, preferred_element_type=F32)
    o_ref[...] = dot(c_hi, w_hi) + (dot(c_hi, w_lo) + dot(c_lo, w_hi)) + b_ref[...]


def _ada_mod(c, w_ada, b_ada):
    depth, d, n = w_ada.shape
    rows = c.shape[0]
    tn = min(ADA_COL_TILE, n)
    return pl.pallas_call(
        _ada_body,
        grid=(depth, n // tn),
        in_specs=[pl.BlockSpec((rows, d), lambda l, j: (0, 0)),
                  pl.BlockSpec((None, d, tn), lambda l, j: (l, 0, j)),
                  pl.BlockSpec((None, 1, tn), lambda l, j: (l, 0, j))],
        out_specs=pl.BlockSpec((None, rows, tn), lambda l, j: (l, 0, j)),
        out_shape=jax.ShapeDtypeStruct((depth, rows, n), F32),
        compiler_params=_cparams(("arbitrary", "arbitrary")),
        name="ada_mod",
    )(c, w_ada, b_ada.reshape(depth, 1, n))


def _mod_spec(mod, tm, rows_per_batch, d):
    if mod.ndim == 3:
        tiles = rows_per_batch // tm
        return pl.BlockSpec((None, 1, d), lambda i, *_: (i // tiles, 0, 0))
    return pl.BlockSpec((tm, d), lambda i, *_: (i, 0))


def _norm_mod(x, g, sc, sh):
    ms = jnp.mean(x * x, axis=-1, keepdims=True)
    h = (x * lax.rsqrt(ms + RMS_EPS)) * g
    return h * (1.0 + sc) + sh


def _normmod_mm_body(x_ref, g_ref, sc_ref, sh_ref, w_ref, *o_refs, splits, col_chunk):
    hb = _norm_mod(x_ref[...], g_ref[...], sc_ref[...], sh_ref[...]).astype(BF16)
    for o_ref, (c0, c1) in zip(o_refs, splits):
        for s in range(c0, c1, col_chunk):
            e = min(s + col_chunk, c1)
            o_ref[:, s - c0:e - c0] = jnp.dot(hb, w_ref[:, s:e], preferred_element_type=F32)


def _normmod_mm(x2, g, sc, sh, w_bf16, splits, tm, rows_per_batch):
    m, d = x2.shape
    n = w_bf16.shape[1]
    outs = tuple(jax.ShapeDtypeStruct((m, c1 - c0), F32) for c0, c1 in splits)
    return pl.pallas_call(
        functools.partial(_normmod_mm_body, splits=splits, col_chunk=PROJ_COL_CHUNK),
        grid=(m // tm,),
        in_specs=[pl.BlockSpec((tm, d), lambda i: (i, 0)),
                  pl.BlockSpec((1, d), lambda i: (0, 0)),
                  _mod_spec(sc, tm, rows_per_batch, d),
                  _mod_spec(sh, tm, rows_per_batch, d),
                  pl.BlockSpec((d, n), lambda i: (0, 0))],
        out_specs=tuple(pl.BlockSpec((tm, c1 - c0), lambda i: (i, 0)) for c0, c1 in splits),
        out_shape=outs,
        compiler_params=_cparams(("arbitrary",)),
        name="normmod_mm",
    )(x2, g.reshape(1, d), sc, sh, w_bf16)


def _mix_mlp_body(*refs, n_in, final_norm):
    x_ref, gm_ref = refs[0], refs[1]
    y_refs = refs[2:2 + n_in]
    w_refs = refs[2 + n_in:2 + 2 * n_in]
    (g_ref, sc_ref, sh_ref, gate_ref, wu_ref, wd_ref, gf_ref, o_ref,
     hb_ref, acc_ref, x1_ref) = refs[2 + 2 * n_in:]
    f = pl.program_id(1)

    @pl.when(f == 0)
    def _():
        mix = None
        for y_ref, w_ref in zip(y_refs, w_refs):
            p = jnp.dot(y_ref[...].astype(BF16), w_ref[...], preferred_element_type=F32)
            mix = p if mix is None else mix + p
        x1 = x_ref[...] + gm_ref[...] * mix
        x1_ref[...] = x1
        hb_ref[...] = _norm_mod(x1, g_ref[...], sc_ref[...], sh_ref[...]).astype(BF16)
        acc_ref[...] = jnp.zeros_like(acc_ref)

    up = jnp.dot(hb_ref[...], wu_ref[...], preferred_element_type=F32)
    act = jnp.square(jnp.maximum(up, 0.0)).astype(BF16)
    acc_ref[...] += jnp.dot(act, wd_ref[...], preferred_element_type=F32)

    @pl.when(f == pl.num_programs(1) - 1)
    def _():
        xn = x1_ref[...] + gate_ref[...] * acc_ref[...]
        if final_norm:
            ms = jnp.mean(xn * xn, axis=-1, keepdims=True)
            xn = (xn * lax.rsqrt(ms + RMS_EPS)) * gf_ref[...]
        o_ref[...] = xn


def _mix_mlp(x2, gate_m, ys, ws_bf16, g, sc, sh, gate, wu_bf16, wd_bf16, gfinal, tm, tf,
             rows_per_batch, final_norm):
    m, d = x2.shape
    ff = wu_bf16.shape[1]
    mod = lambda a: _mod_spec(a, tm, rows_per_batch, d)
    row = pl.BlockSpec((1, d), lambda i, f: (0, 0))
    in_specs = [pl.BlockSpec((tm, d), lambda i, f: (i, 0)), mod(gate_m)]
    in_specs += [pl.BlockSpec((tm, y.shape[1]), lambda i, f: (i, 0)) for y in ys]
    in_specs += [pl.BlockSpec(w.shape, lambda i, f: (0, 0)) for w in ws_bf16]
    in_specs += [row, mod(sc), mod(sh), mod(gate),
                 pl.BlockSpec((d, tf), lambda i, f: (0, f)),
                 pl.BlockSpec((tf, d), lambda i, f: (f, 0)), row]
    return pl.pallas_call(
        functools.partial(_mix_mlp_body, n_in=len(ys), final_norm=final_norm),
        grid=(m // tm, ff // tf),
        in_specs=in_specs,
        out_specs=pl.BlockSpec((tm, d), lambda i, f: (i, 0)),
        out_shape=jax.ShapeDtypeStruct((m, d), F32),
        scratch_shapes=[pltpu.VMEM((tm, d), BF16), pltpu.VMEM((tm, d), F32),
                        pltpu.VMEM((tm, d), F32)],
        compiler_params=_cparams(("arbitrary", "arbitrary")),
        name="mix_mlp",
    )(x2, gate_m, *ys, *ws_bf16, g.reshape(1, d), sc, sh, gate, wu_bf16, wd_bf16,
      gfinal.reshape(1, d))


def _rwkv_project(za, prev_row, mu, w0, w_up, a0, a_up, g_up, d_a):
    row = lax.broadcasted_iota(jnp.int32, za.shape, 0)
    prev = jnp.where(row == 0, prev_row, pltpu.roll(za, 1, axis=0))
    zs = za + mu * (prev - za)
    o_w = 3 * d_a
    o_a = o_w + LORA_W
    o_g = o_a + LORA_A
    lw = -DECAY_SCALE * _sigmoid(w0 + _bdot(jnp.tanh(zs[:, o_w:o_a]), w_up))
    a = _sigmoid(a0 + _bdot(zs[:, o_a:o_g], a_up))
    g = _bdot(_sigmoid(zs[:, o_g:o_g + LORA_G]), g_up)
    return zs[:, 0:d_a], zs[:, d_a:2 * d_a], zs[:, 2 * d_a:3 * d_a], lw, a, g


def _segsum(x, ones_blk):
    hi = x.astype(BF16)
    lo = (x - hi.astype(F32)).astype(BF16)
    w = ones_blk.shape[0]
    parts = [jnp.dot(hi[:, c0:c0 + w], ones_blk, preferred_element_type=F32)
             + jnp.dot(lo[:, c0:c0 + w], ones_blk, preferred_element_type=F32)
             for c0 in range(0, x.shape[1], w)]
    return jnp.concatenate(parts, axis=1)


def _rwkv_prepare(r, k, v, lw, a, kkw, kaw, valid, tri_incl, ones_blk):
    c = r.shape[0]
    kk = k * kkw
    kk = kk / jnp.maximum(jnp.sqrt(_segsum(kk * kk, ones_blk)), 1e-12)
    k2 = k * (1.0 + (a - 1.0) * kaw)
    b = kk * a
    if valid is not None:
        zero = lambda x: jnp.where(valid, x, 0.0)
        lw, kk, k2, b, v, r = (zero(x) for x in (lw, kk, k2, b, v, r))
    p1 = lw.astype(BF16)
    d1 = lw - p1.astype(F32)
    p2 = d1.astype(BF16)
    p3 = (d1 - p2.astype(F32)).astype(BF16)
    tri = tri_incl.astype(BF16)
    cum = sum(jnp.dot(tri, p, preferred_element_type=F32) for p in (p1, p2, p3))
    tot = cum[c - 1:c, :]
    e_in = jnp.exp(cum)
    e_out = jnp.exp(-cum)
    e_end = jnp.exp(tot - cum)
    return dict(r=r, k2=k2, v=v, kap_t=kk * jnp.exp(cum - lw), r_t=r * e_in, b_t=b * e_out,
                k_t=k2 * e_out, b_h=b * e_end, k_h=k2 * e_end, e_tot=jnp.exp(tot))


def _rwkv_local(kap_t, r_t, b_t, k_t, b_h, k_h, v, tri_incl, tri_strict, eye, n_doubling):
    c = kap_t[0].shape[0]
    each = lambda f, *ls: [f(*xs) for xs in zip(*ls)]

    rhs = each(lambda x, y: jnp.concatenate([x, y], axis=0), b_t, k_t)
    gk = each(_bdot_nt, kap_t, rhs)
    gr = each(_bdot_nt, r_t, rhs)
    n1 = each(lambda x: jnp.where(tri_strict, x[:, :c], 0.0), gk)
    n2 = each(lambda x: jnp.where(tri_strict, x[:, c:], 0.0), gk)
    m1 = each(lambda x: jnp.where(tri_incl > 0, x[:, :c], 0.0), gr)
    m2 = each(lambda x: jnp.where(tri_incl > 0, x[:, c:], 0.0), gr)

    x = each(lambda n: -n, n1)
    tinv = each(lambda x_: eye + x_, x)
    for _ in range(n_doubling):
        x = each(_bdot, x, x)
        tinv = each(lambda t_, x_: t_ + _bdot(t_, x_), tinv, x)

    nv = each(lambda n, m_, v_: _bdot(jnp.concatenate([n, m_], axis=0), v_), n2, m2, v)
    ty = each(lambda t_, kp, nv_: _bdot(t_, jnp.concatenate([kp, nv_[:c]], axis=1)),
              tinv, kap_t, nv)
    m1ty = each(_bdot, m1, ty)
    rq = each(lambda r_, m_: r_ - m_[:, :HEAD_A], r_t, m1ty)
    ol = each(lambda nv_, m_: nv_[c:] - m_[:, HEAD_A:], nv, m1ty)
    tb = each(_bdot_tn, ty, b_h)
    vk = each(_bdot_tn, v, k_h)
    gmat = each(lambda tb_: tb_[:HEAD_A], tb)
    hmat = each(lambda vk_, tb_: vk_ - tb_[HEAD_A:], vk, tb)
    return rq, ol, gmat, hmat


def _rwkv_chunk_body(za_ref, shift_ref, mu_ref, w0_ref, wup_ref, a0_ref, aup_ref, gup_ref,
                     kkw_ref, kaw_ref, rkw_ref, lnw_ref, lnb_ref, s0_ref, y_ref, sout_ref,
                     s_ref, last_ref, *, chunk, n_chunks, n_heads, n_valid, group):
    @pl.when(pl.program_id(1) == 0)
    def _():
        s_ref[...] = s0_ref[...]
        last_ref[...] = shift_ref[...]

    c = chunk
    d_a = n_heads * HEAD_A
    ri = lax.broadcasted_iota(jnp.int32, (c, c), 0)
    ci = lax.broadcasted_iota(jnp.int32, (c, c), 1)
    tri_incl = (ri >= ci).astype(F32)
    tri_strict = ri > ci
    eye = (ri == ci).astype(F32)
    bi = lax.broadcasted_iota(jnp.int32, (MXU_DIM, MXU_DIM), 0) // HEAD_A
    bj = lax.broadcasted_iota(jnp.int32, (MXU_DIM, MXU_DIM), 1) // HEAD_A
    ones_blk = (bi == bj).astype(BF16)
    valid = None
    if n_valid < c:
        valid = lax.broadcasted_iota(jnp.int32, (c, d_a), 0) < n_valid
    n_doubling = max(int(math.log2(c)) - 1, 0)
    heads = range(n_heads)
    sls = [slice(h * HEAD_A, (h + 1) * HEAD_A) for h in heads]

    n_seq = za_ref.shape[0]

    def one_group(ig, carry):
        rows = [pl.ds(pl.multiple_of((ig * group + gi) * c, c), c) for gi in range(group)]
        pre = []
        for bi in range(n_seq):
            prev_row = last_ref[bi]
            for gi in range(group):
                za = za_ref[bi, rows[gi], :]
                r, k, v, lw, a, g = _rwkv_project(za, prev_row, mu_ref[...], w0_ref[...],
                                                  wup_ref[...], a0_ref[...], aup_ref[...],
                                                  gup_ref[...], d_a)
                prev_row = za[c - 1:c, :]
                pre.append(dict(_rwkv_prepare(r, k, v, lw, a, kkw_ref[...], kaw_ref[...], valid,
                                              tri_incl, ones_blk), g=g))
            last_ref[bi] = prev_row
        chains = lambda name: [p[name][:, sl] for p in pre for sl in sls]
        rq, ol, gmat, hmat = _rwkv_local(
            chains('kap_t'), chains('r_t'), chains('b_t'), chains('k_t'), chains('b_h'),
            chains('k_h'), chains('v'), tri_incl, tri_strict, eye, n_doubling)
        s = [[s_ref[bi, h] for h in heads] for bi in range(n_seq)]
        for gi in range(group):
            for bi in range(n_seq):
                p = pre[bi * group + gi]
                at = lambda xs, h: xs[(bi * group + gi) * n_heads + h]
                o = jnp.concatenate([_bdot_nt(at(rq, h), s[bi][h]) + at(ol, h) for h in heads],
                                    axis=1)
                s[bi] = [s[bi][h] * p['e_tot'][:, sls[h]] - _bdot(s[bi][h], at(gmat, h))
                         + at(hmat, h) for h in heads]
                mu = _segsum(o, ones_blk) * (1.0 / HEAD_A)
                dev = o - mu
                var = _segsum(dev * dev, ones_blk) * (1.0 / HEAD_A)
                on = dev * lax.rsqrt(var + GN_EPS) * lnw_ref[...] + lnb_ref[...]
                bonus = _segsum(p['r'] * p['k2'] * rkw_ref[...], ones_blk) * p['v']
                y_ref[bi, rows[gi], :] = (on + bonus) * p['g']
        for bi in range(n_seq):
            for h in heads:
                s_ref[bi, h] = s[bi][h]
        return carry

    lax.fori_loop(0, n_chunks // group, one_group, 0)

    @pl.when(pl.program_id(1) == pl.num_programs(1) - 1)
    def _():
        sout_ref[...] = s_ref[...]


def _rwkv_chunked(za, shift_prev, mu, w0, w_up, a0, a_up, g_up, k_k, k_a, r_k, lnx_w, lnx_b, s0,
                  tb, chunk, n_valid):
    b, t, dp = za.shape
    d_a = w0.shape[0]
    n_heads = d_a // HEAD_A
    row = lambda n: pl.BlockSpec((1, n), lambda i, j: (0, 0))
    full = lambda w: pl.BlockSpec(w.shape, lambda i, j: (0, 0))
    group = min(RWKV_PARALLEL, tb // chunk)
    n_seq = max(d for d in range(1, RWKV_PARALLEL // group + 1) if b % d == 0)
    st = pl.BlockSpec((n_seq, n_heads, HEAD_A, HEAD_A), lambda i, j: (i, 0, 0, 0))
    return pl.pallas_call(
        functools.partial(_rwkv_chunk_body, chunk=chunk, n_chunks=tb // chunk,
                          n_heads=n_heads, n_valid=n_valid, group=group),
        grid=(b // n_seq, t // tb),
        in_specs=[pl.BlockSpec((n_seq, tb, dp), lambda i, j: (i, j, 0)),
                  pl.BlockSpec((n_seq, 1, dp), lambda i, j: (i, 0, 0)),
                  row(dp), row(d_a), full(w_up), row(d_a), full(a_up), full(g_up)]
                 + [row(d_a)] * 5 + [st],
        out_specs=(pl.BlockSpec((n_seq, tb, d_a), lambda i, j: (i, j, 0)), st),
        out_shape=(jax.ShapeDtypeStruct((b, t, d_a), F32),
                   jax.ShapeDtypeStruct((b, n_heads, HEAD_A, HEAD_A), F32)),
        scratch_shapes=[pltpu.VMEM((n_seq, n_heads, HEAD_A, HEAD_A), F32),
                        pltpu.VMEM((n_seq, 1, dp), F32)],
        compiler_params=_cparams(("arbitrary", "arbitrary")),
        name="rwkv_chunked",
    )(za, shift_prev.reshape(b, 1, dp), mu.reshape(1, dp), w0.reshape(1, d_a), w_up,
      a0.reshape(1, d_a), a_up, g_up, k_k.reshape(1, d_a), k_a.reshape(1, d_a),
      r_k.reshape(1, d_a), lnx_w.reshape(1, d_a), lnx_b.reshape(1, d_a), s0)


def _s5_disc_body(lr_ref, li_ref, ldt_ref, bre_ref, bim_ref, ar_ref, ai_ref, bbre_ref, bbim_ref):
    lr, li = lr_ref[...], li_ref[...]
    dt = jnp.exp(ldt_ref[...])
    mag = jnp.exp(lr * dt)
    ar = mag * jnp.cos(li * dt)
    ai = mag * jnp.sin(li * dt)
    den = lr * lr + li * li
    fr = ((ar - 1.0) * lr + ai * li) / den
    fi = (ai * lr - (ar - 1.0) * li) / den
    ar_ref[...] = ar
    ai_ref[...] = ai
    b_re, b_im = bre_ref[...], bim_ref[...]
    bb_re = fr[:, None, :] * b_re - fi[:, None, :] * b_im
    bb_im = fr[:, None, :] * b_im + fi[:, None, :] * b_re
    pr, pi = jnp.ones_like(ar), jnp.zeros_like(ai)
    for i in range(S5_FOLD):
        bbre_ref[i] = pr[:, None, :] * bb_re - pi[:, None, :] * bb_im
        bbim_ref[i] = pr[:, None, :] * bb_im + pi[:, None, :] * bb_re
        pr, pi = pr * ar - pi * ai, pr * ai + pi * ar


def _s5_discretise(lam_re, lam_im, log_dt, b_re, b_im):
    g, p = lam_re.shape
    n = b_re.shape[2]
    bt = lambda x: jnp.swapaxes(x, 1, 2)
    return pl.pallas_call(
        _s5_disc_body,
        out_shape=(jax.ShapeDtypeStruct((g, p), F32), jax.ShapeDtypeStruct((g, p), F32),
                   jax.ShapeDtypeStruct((S5_FOLD, g, n, p), F32),
                   jax.ShapeDtypeStruct((S5_FOLD, g, n, p), F32)),
        name="s5_discretise",
    )(lam_re, lam_im, log_dt.reshape(g, 1), bt(b_re), bt(b_im))


def _gelu_tanh(x):
    c = math.sqrt(2.0 / math.pi)
    return 0.5 * x * (1.0 + jnp.tanh(c * (x + 0.044715 * (x * x * x))))


def _s5_body(u_ref, h0r_ref, h0i_ref, ar_ref, ai_ref, bdre_ref, bdim_ref, cdre_ref, cdim_ref,
             d_ref, wglu_ref, bglu_ref, y_ref, hre_ref, him_ref,
             xr_ref, xi_ref, apr_ref, api_ref, mr_ref, mi_ref, cr_ref, ci_ref, *, last_row):
    j = pl.program_id(1)
    rows, width = xr_ref.shape
    half_u = u_ref.shape[1] // 2
    half_x = width // 2

    ar, ai = ar_ref[...], ai_ref[...]
    a2r, a2i = ar * ar - ai * ai, 2.0 * ar * ai
    a4r, a4i = a2r * a2r - a2i * a2i, 2.0 * a2r * a2i

    @pl.when(j == 0)
    def _():
        cr_ref[...] = h0r_ref[...]
        ci_ref[...] = h0i_ref[...]
        pr, pi = ar, ai
        for s in range(SUBLANES):
            apr_ref[s:s + 1, :] = pr
            api_ref[s:s + 1, :] = pi
            pr, pi = pr * ar - pi * ai, pr * ai + pi * ar
        sub8 = lax.broadcasted_iota(jnp.int32, (SUBLANES, width), 0)
        mr_ref[...] = jnp.where(sub8 >= S5_FOLD, a4r, 0.0)
        mi_ref[...] = jnp.where(sub8 >= S5_FOLD, a4i, 0.0)

    u = u_ref[...]
    n_parts = bdre_ref.shape[0]
    part_x = width // n_parts
    sub = lax.broadcasted_iota(jnp.int32, (rows, LANES), 0) % SUBLANES
    for pt in range(n_parts):
        up = u[:, pt * LANES:(pt + 1) * LANES]
        lags = [up] + [jnp.where(sub >= i, pltpu.roll(up, i, axis=0), 0.0)
                       for i in range(1, S5_FOLD)]
        lhs = jnp.concatenate(lags, axis=1).astype(BF16)
        cols = slice(pt * part_x, (pt + 1) * part_x)
        xr_ref[:, cols] = jnp.dot(lhs, bdre_ref[pt], preferred_element_type=F32)
        xi_ref[:, cols] = jnp.dot(lhs, bdim_ref[pt], preferred_element_type=F32)

    lane_chunk = 4 * LANES
    grouped = lambda x: x.reshape(rows // SUBLANES, SUBLANES, lane_chunk)
    for c0 in range(0, width, lane_chunk):
        cols = slice(c0, c0 + lane_chunk)
        xr, xi = xr_ref[:, cols], xi_ref[:, cols]
        pr, pi = mr_ref[:, cols][None], mi_ref[:, cols][None]
        sr = grouped(pltpu.roll(xr, S5_FOLD, axis=0))
        si = grouped(pltpu.roll(xi, S5_FOLD, axis=0))
        xr_ref[:, cols] = (grouped(xr) + (pr * sr - pi * si)).reshape(rows, lane_chunk)
        xi_ref[:, cols] = (grouped(xi) + (pr * si + pi * sr)).reshape(rows, lane_chunk)

    def group(n, carry):
        cr, ci = carry
        r0 = pl.multiple_of(n * SUBLANES, SUBLANES)
        rs = pl.ds(r0, SUBLANES)
        apr, api = apr_ref[...], api_ref[...]
        hr = xr_ref[rs, :] + apr * cr - api * ci
        hi = xi_ref[rs, :] + apr * ci + api * cr
        xr_ref[rs, :] = hr
        xi_ref[rs, :] = hi
        return hr[SUBLANES - 1:SUBLANES, :], hi[SUBLANES - 1:SUBLANES, :]

    cr, ci = lax.fori_loop(0, rows // SUBLANES, group, (cr_ref[...], ci_ref[...]))
    cr_ref[...] = cr
    ci_ref[...] = ci

    @pl.when(j == pl.num_programs(1) - 1)
    def _():
        hre_ref[...] = xr_ref[last_row:last_row + 1, :]
        him_ref[...] = xi_ref[last_row:last_row + 1, :]

    for hf in range(2):
        cols = slice(hf * half_x, (hf + 1) * half_x)
        ucols = slice(hf * half_u, (hf + 1) * half_u)
        y = (jnp.dot(xr_ref[:, cols].astype(BF16), cdre_ref[hf], preferred_element_type=F32)
             - jnp.dot(xi_ref[:, cols].astype(BF16), cdim_ref[hf], preferred_element_type=F32)
             + d_ref[:, ucols] * u[:, ucols])
        y_ref[:, ucols] = _gelu_tanh(y)
    y = y_ref[...]
    z = jnp.dot(y.astype(BF16), wglu_ref[...], preferred_element_type=F32) + bglu_ref[...]
    y_ref[...] = y * _sigmoid(z)


def _block_diag_parts(w_gab, transpose, n_parts=2):
    g = w_gab.shape[0]
    hg = g // n_parts
    eye = jnp.eye(hg, dtype=w_gab.dtype)
    parts = []
    for pt in range(n_parts):
        w = w_gab[pt * hg:(pt + 1) * hg]
        if transpose:
            w = jnp.swapaxes(w, 1, 2)
        a, b = w.shape[1], w.shape[2]
        parts.append(jnp.einsum('gab,gh->gahb', w, eye).reshape(hg * a, hg * b))
    return jnp.stack(parts)


def _s5_mix(u, h0_re, h0_im, ar, ai, bbt_re, bbt_im, c_re, c_im, d, w_glu, b_glu, rows, n_valid):
    b, t, d_b = u.shape
    g, p = ar.shape
    width = g * p
    n_parts = d_b // LANES
    lagged = lambda w: jnp.concatenate(
        [_block_diag_parts(w[i], False, n_parts) for i in range(S5_FOLD)], axis=1).astype(BF16)
    bd_re, bd_im = lagged(bbt_re), lagged(bbt_im)
    cd_re = _block_diag_parts(c_re, True).astype(BF16)
    cd_im = _block_diag_parts(c_im, True).astype(BF16)
    n_blocks = t // rows
    last_row = (n_valid - 1) - (n_blocks - 1) * rows
    full = lambda x: pl.BlockSpec(x.shape, lambda i, j: (0,) * x.ndim)
    st = pl.BlockSpec((None, 1, width), lambda i, j: (i, 0, 0))
    args = (u, h0_re.reshape(b, 1, width), h0_im.reshape(b, 1, width),
            ar.reshape(1, width), ai.reshape(1, width), bd_re, bd_im, cd_re, cd_im,
            d.reshape(1, d_b), w_glu.astype(BF16), b_glu.reshape(1, d_b))
    return pl.pallas_call(
        functools.partial(_s5_body, last_row=last_row),
        grid=(b, n_blocks),
        in_specs=[pl.BlockSpec((None, rows, d_b), lambda i, j: (i, j, 0)), st, st]
                 + [full(x) for x in args[3:]],
        out_specs=(pl.BlockSpec((None, rows, d_b), lambda i, j: (i, j, 0)), st, st),
        out_shape=(jax.ShapeDtypeStruct((b, t, d_b), F32),
                   jax.ShapeDtypeStruct((b, 1, width), F32),
                   jax.ShapeDtypeStruct((b, 1, width), F32)),
        scratch_shapes=[pltpu.VMEM((rows, width), F32), pltpu.VMEM((rows, width), F32),
                        pltpu.VMEM((SUBLANES, width), F32), pltpu.VMEM((SUBLANES, width), F32),
                        pltpu.VMEM((SUBLANES, width), F32), pltpu.VMEM((SUBLANES, width), F32),
                        pltpu.VMEM((1, width), F32), pltpu.VMEM((1, width), F32)],
        compiler_params=_cparams(("arbitrary", "arbitrary")),
        name="s5_mix",
    )(*args)


def _lambda_full(lq1, lk1, lq2, lk2):
    s1 = jnp.sum(lq1 * lk1, axis=-1, keepdims=True)
    s2 = jnp.sum(lq2 * lk2, axis=-1, keepdims=True)
    return jnp.exp(s1) - jnp.exp(s2) + LAMBDA_INIT


def _sub_ln(o, subln):
    ms = jnp.mean(o * o, axis=-1, keepdims=True)
    return o * lax.rsqrt(ms + RMS_EPS) * subln * (1.0 - LAMBDA_INIT)


def _attn_prompt_body(q_ref, k_ref, v_ref, lq1_ref, lk1_ref, lq2_ref, lk2_ref, subln_ref,
                      o_ref, kb_ref, vt_ref, m_ref, l_ref, acc_ref, sa_ref, sb_ref, *, tile,
                      n_tiles):
    halves = range(2)
    for c in range(n_tiles):
        rows = slice(c * tile, (c + 1) * tile)
        kb_ref[rows, :] = k_ref[rows, :].astype(BF16)
        vt_ref[c] = v_ref[rows, :].T.astype(BF16)
    lam = _lambda_full(lq1_ref[...], lk1_ref[...], lq2_ref[...], lk2_ref[...])
    s_even, s_odd = sa_ref, sb_ref

    wide = 2 * tile
    all_q = slice(0, wide)
    late_q = slice(tile, wide)

    def q_tile(qj, carry):
        q_rows = pl.ds(pl.multiple_of(qj * wide, wide), wide)
        q = q_ref[q_rows, :] * (SCALE_C * math.log2(math.e))
        lane = lax.broadcasted_iota(jnp.int32, q.shape, 1)
        qs = [jnp.where((lane >= HALF_C) == bool(c), q, 0.0).astype(BF16) for c in halves]
        m_ref[...] = jnp.full_like(m_ref, NEG_INF)
        l_ref[...] = jnp.zeros_like(l_ref)
        acc_ref[...] = jnp.zeros_like(acc_ref)

        def scores(kj, slot, cols):
            k0 = kj * tile if isinstance(kj, int) else pl.multiple_of(kj * tile, tile)
            kb = kb_ref[pl.ds(k0, tile), :]
            for c in halves:
                slot[c, :, cols] = lax.dot_general(kb, qs[c][cols], (((1,), (1,)), ((), ())),
                                                   preferred_element_type=F32)

        def kv_step(kj, slot, diagonal, cols):
            st = [slot[c, :, cols] for c in halves]
            if diagonal:
                krow = lax.broadcasted_iota(jnp.int32, st[0].shape, 0)
                qcol = lax.broadcasted_iota(jnp.int32, st[0].shape, 1)
                st = [jnp.where(krow <= qcol, x, NEG_INF) for x in st]
            m_prev = [m_ref[c, :, cols] for c in halves]
            m_new = [jnp.maximum(m_prev[c], jnp.max(st[c], axis=0, keepdims=True))
                     for c in halves]
            alpha = [jnp.exp2(m_prev[c] - m_new[c]) for c in halves]
            p = [jnp.exp2(st[c] - m_new[c]) for c in halves]
            vt = vt_ref[kj]
            pv = [jnp.dot(vt, p[c].astype(BF16), preferred_element_type=F32) for c in halves]
            for c in halves:
                l_ref[c, :, cols] = alpha[c] * l_ref[c, :, cols] + jnp.sum(p[c], axis=0,
                                                                           keepdims=True)
                acc_ref[c, :, cols] = alpha[c] * acc_ref[c, :, cols] + pv[c]
                m_ref[c, :, cols] = m_new[c]

        scores(0, s_even, all_q)

        def two_steps(i, c2):
            scores(2 * i + 1, s_odd, all_q)
            kv_step(2 * i, s_even, False, all_q)
            scores(2 * i + 2, s_even, all_q)
            kv_step(2 * i + 1, s_odd, False, all_q)
            return c2

        lax.fori_loop(0, qj, two_steps, 0)
        scores(2 * qj + 1, s_odd, late_q)
        kv_step(2 * qj, s_even, True, all_q)
        kv_step(2 * qj + 1, s_odd, True, late_q)

        ot = acc_ref[0] / l_ref[0] - lam * (acc_ref[1] / l_ref[1])
        ms = jnp.mean(ot * ot, axis=0, keepdims=True)
        ot = ot * lax.rsqrt(ms + RMS_EPS) * subln_ref[...] * (1.0 - LAMBDA_INIT)
        o_ref[q_rows, :] = ot.T
        return carry

    lax.fori_loop(0, n_tiles // 2, q_tile, 0)


def _attn_prompt(q, k, v, lq1, lk1, lq2, lk2, subln, tile):
    b, t, d = q.shape
    n_heads = d // HEAD_C
    n_tiles = t // tile
    assert t % (2 * tile) == 0, (t, tile)
    small = lambda n: pl.BlockSpec((1, n), lambda i, h: (0, 0))
    per_head = pl.BlockSpec((None, t, HEAD_C), lambda i, h: (i, 0, h))
    return pl.pallas_call(
        functools.partial(_attn_prompt_body, tile=tile, n_tiles=n_tiles),
        grid=(b, n_heads),
        in_specs=[per_head, per_head, per_head,
                  small(HALF_C), small(HALF_C), small(HALF_C), small(HALF_C),
                  pl.BlockSpec((HEAD_C, 1), lambda i, h: (0, 0))],
        out_specs=per_head,
        out_shape=jax.ShapeDtypeStruct((b, t, d), F32),
        scratch_shapes=[pltpu.VMEM((t, HEAD_C), BF16), pltpu.VMEM((n_tiles, HEAD_C, tile), BF16),
                        pltpu.VMEM((2, 1, 2 * tile), F32), pltpu.VMEM((2, 1, 2 * tile), F32),
                        pltpu.VMEM((2, HEAD_C, 2 * tile), F32),
                        pltpu.VMEM((2, tile, 2 * tile), F32),
                        pltpu.VMEM((2, tile, 2 * tile), F32)],
        compiler_params=_cparams(("arbitrary", "arbitrary")),
        name="attn_prompt",
    )(q, k, v, lq1.reshape(1, -1), lk1.reshape(1, -1), lq2.reshape(1, -1), lk2.reshape(1, -1),
      subln.reshape(-1, 1))


def _attn_sample_body(pt_ref, qrep_ref, knew_ref, vnew_ref, *rest, pages_per_step, n_heads,
                      t_new, page):
    kp_refs = rest[:pages_per_step]
    vp_refs = rest[pages_per_step:2 * pages_per_step]
    lq1_ref, lk1_ref, lq2_ref, lk2_ref, subln_ref = rest[2 * pages_per_step:2 * pages_per_step + 5]
    o_ref = rest[2 * pages_per_step + 5]
    qw_ref, m_ref, l_ref, acc_ref = rest[2 * pages_per_step + 6:]
    j = pl.program_id(1)
    pairs = range(n_heads // 2)
    q_rows = qrep_ref.shape[1]

    @pl.when(j == 0)
    def _():
        qr = qrep_ref[...]
        row = lax.broadcasted_iota(jnp.int32, qr.shape, 1)
        lane = lax.broadcasted_iota(jnp.int32, qr.shape, 2)
        qm = jnp.where(lane // HALF_C == row // t_new, qr * SCALE_C, 0.0)
        zero = jnp.zeros((q_rows, HEAD_C), F32)
        for pr in pairs:
            qw_ref[pr] = jnp.concatenate(
                [jnp.concatenate([qm[2 * pr], zero], axis=1),
                 jnp.concatenate([zero, qm[2 * pr + 1]], axis=1)], axis=0).astype(BF16)
        m_ref[...] = jnp.full_like(m_ref, NEG_INF)
        l_ref[...] = jnp.zeros_like(l_ref)
        acc_ref[...] = jnp.zeros_like(acc_ref)

    def pair_rows(ref, pr):
        return jnp.concatenate([ref[pl.ds(2 * pr + i, page, stride=n_heads), :] for i in range(2)],
                               axis=1).astype(BF16)

    def update(s, values):
        m_prev = [m_ref[pr] for pr in pairs]
        m_new = [jnp.maximum(m_prev[pr], jnp.max(s[pr], axis=-1, keepdims=True)) for pr in pairs]
        alpha = [jnp.exp(m_prev[pr] - m_new[pr]) for pr in pairs]
        p = [jnp.exp(s[pr] - m_new[pr]) for pr in pairs]
        pv = []
        for pr in pairs:
            w = p[pr].shape[1] // len(values[pr])
            acc = None
            for i, vb in enumerate(values[pr]):
                d = jnp.dot(p[pr][:, i * w:(i + 1) * w].astype(BF16), vb,
                            preferred_element_type=F32)
                acc = d if acc is None else acc + d
            pv.append(acc)
        for pr in pairs:
            l_ref[pr] = alpha[pr] * l_ref[pr] + jnp.sum(p[pr], axis=-1, keepdims=True)
            acc_ref[pr] = alpha[pr] * acc_ref[pr] + pv[pr]
            m_ref[pr] = m_new[pr]

    s = [jnp.concatenate(
            [lax.dot_general(qw_ref[pr], pair_rows(kp, pr), (((1,), (1,)), ((), ())),
                             preferred_element_type=F32) for kp in kp_refs], axis=1)
         for pr in pairs]
    update(s, [[pair_rows(vp, pr) for vp in vp_refs] for pr in pairs])

    @pl.when(j == pl.num_programs(1) - 1)
    def _():
        both = lambda ref, pr: jnp.concatenate([ref[2 * pr], ref[2 * pr + 1]], axis=1).astype(BF16)
        sn = []
        for pr in pairs:
            x = lax.dot_general(qw_ref[pr], both(knew_ref, pr), (((1,), (1,)), ((), ())),
                                preferred_element_type=F32)
            trow = lax.broadcasted_iota(jnp.int32, x.shape, 0) % t_new
            tcol = lax.broadcasted_iota(jnp.int32, x.shape, 1)
            sn.append(jnp.where(tcol <= trow, x, NEG_INF))
        update(sn, [[both(vnew_ref, pr)] for pr in pairs])
        lam = _lambda_full(lq1_ref[...], lk1_ref[...], lq2_ref[...], lk2_ref[...])
        for pr in pairs:
            on = acc_ref[pr] / l_ref[pr]
            for i in range(2):
                h = 2 * pr + i
                r0 = i * q_rows
                oh = on[r0:r0 + 2 * t_new, i * HEAD_C:(i + 1) * HEAD_C]
                o = oh[0:t_new] - lam * oh[t_new:2 * t_new]
                o_ref[:, h * HEAD_C:(h + 1) * HEAD_C] = _sub_ln(o, subln_ref[...])


def _attn_sample(q, k_new, v_new, cache_k, cache_v, page_table, lq1, lk1, lq2, lk2, subln,
                 pages_per_step):
    db, t_new, d = q.shape
    n_heads = d // HEAD_C
    n_pool, page = cache_k.shape[0], cache_k.shape[1]
    n_pages = page_table.shape[1]
    ck = cache_k.reshape(n_pool, page * n_heads, HEAD_C)
    cv = cache_v.reshape(n_pool, page * n_heads, HEAD_C)
    by_head = lambda x: jnp.swapaxes(x.reshape(db, t_new, n_heads, HEAD_C), 1, 2)
    q_rows = NEW_PAD
    qrep = jnp.tile(by_head(q), (1, 1, q_rows // t_new, 1))
    pad = ((0, 0), (0, 0), (0, NEW_PAD - t_new), (0, 0))
    knew = jnp.pad(by_head(k_new), pad)
    vnew = jnp.pad(by_head(v_new), pad)
    pt = page_table.reshape(-1)

    def page_spec(i):
        return pl.BlockSpec((None, page * n_heads, HEAD_C),
                            lambda b, j, pt_ref: (pt_ref[b * n_pages + j * pages_per_step + i], 0, 0))

    small = lambda n: pl.BlockSpec((1, n), lambda b, j, pt_ref: (0, 0))
    per_b = lambda r: pl.BlockSpec((None, n_heads, r, HEAD_C), lambda b, j, pt_ref: (b, 0, 0, 0))
    grid_spec = pltpu.PrefetchScalarGridSpec(
        num_scalar_prefetch=1,
        grid=(db, n_pages // pages_per_step),
        in_specs=[per_b(q_rows), per_b(NEW_PAD), per_b(NEW_PAD)]
                 + [page_spec(i) for i in range(pages_per_step)] * 2
                 + [small(HALF_C)] * 4 + [small(HEAD_C)],
        out_specs=pl.BlockSpec((None, t_new, d), lambda b, j, pt_ref: (b, 0, 0)),
        scratch_shapes=[pltpu.VMEM((n_heads // 2, 2 * q_rows, 2 * HEAD_C), BF16),
                        pltpu.VMEM((n_heads // 2, 2 * q_rows, 1), F32),
                        pltpu.VMEM((n_heads // 2, 2 * q_rows, 1), F32),
                        pltpu.VMEM((n_heads // 2, 2 * q_rows, 2 * HEAD_C), F32)])
    return pl.pallas_call(
        functools.partial(_attn_sample_body, pages_per_step=pages_per_step, n_heads=n_heads,
                          t_new=t_new, page=page),
        grid_spec=grid_spec,
        out_shape=jax.ShapeDtypeStruct((db, t_new, d), F32),
        compiler_params=_cparams(("arbitrary", "arbitrary")),
        name="attn_sample",
    )(pt, qrep, knew, vnew, *([ck] * pages_per_step), *([cv] * pages_per_step),
      lq1.reshape(1, -1), lk1.reshape(1, -1), lq2.reshape(1, -1), lk2.reshape(1, -1),
      subln.reshape(1, -1))


def _pad_time(x, t_pad):
    return jnp.pad(x, ((0, 0), (0, t_pad - x.shape[1]), (0, 0)))


def _trunk(x, mods, shift0, wkv0, sre0, sim0, attend, W, per_row_mod):
    b, t, d = x.shape
    m = b * t
    rows_mod = m if per_row_mod else t
    tm = min(ROW_TILE, rows_mod)
    tm_mlp = min(MLP_ROW_TILE, rows_mod)
    d_a = W['rwkv_w0'].shape[0]
    d_ap = W['rwkv_mu'].shape[0]

    def mod_arg(v):
        if per_row_mod:
            return jnp.repeat(v, t, axis=0)
        return v.reshape(b, 1, d)

    x2 = x.reshape(m, d)

    sh_m, sc_m, g_m, sh_f, sc_f, g_f = (mod_arg(v) for v in mods[0])
    za2, u2 = _normmod_mm(x2, W['norm_mix'][0], sc_m, sh_m, W['w_in_ab_bf16'],
                          ((0, d_ap), (d_ap, W['w_in_ab_bf16'].shape[1])), tm, t)
    za = za2.reshape(b, t, d_ap)
    u = u2.reshape(b, t, -1)
    shift1 = za[:, -1]

    chunk = RWKV_CHUNK if t >= RWKV_CHUNK else RWKV_CHUNK_SHORT
    t_pad = -(-t // chunk) * chunk
    if t_pad != t:
        za_p, u_p = _pad_time(za, t_pad), _pad_time(u, t_pad)
    else:
        za_p, u_p = za, u
    tb = min(RWKV_BLOCK, t_pad)
    y_a, wkv1 = _rwkv_chunked(za_p, shift0, W['rwkv_mu'], W['rwkv_w0'], W['rwkv_w_up'],
                              W['rwkv_a0'], W['rwkv_a_up'], W['rwkv_g_up'], W['rwkv_k_k'],
                              W['rwkv_k_a'], W['rwkv_r_k'], W['rwkv_lnx_w'], W['rwkv_lnx_b'],
                              wkv0, tb, chunk, chunk if t_pad == t else t)
    s5_rows = min(S5_BLOCK, t_pad)
    y_b, sre1, sim1 = _s5_mix(u_p, sre0, sim0, W['s5_ar'], W['s5_ai'], W['s5_bbt_re'],
                              W['s5_bbt_im'], W['s5_c_re'], W['s5_c_im'], W['s5_d'],
                              W['s5_w_glu'], W['s5_b_glu'], s5_rows, t)
    y_a2 = y_a[:, :t].reshape(m, d_a)
    y_b2 = y_b[:, :t].reshape(m, -1)
    wo = W['w_out_bf16'][0]
    x2 = _mix_mlp(x2, g_m, (y_a2, y_b2), (wo[:d_a], wo[d_a:]), W['norm_mlp'][0], sc_f, sh_f, g_f,
                  W['w_up_bf16'][0], W['w_down_bf16'][0], W['norm_f'], tm_mlp, MLP_FF_TILE, t,
                  False)

    sh_m, sc_m, g_m, sh_f, sc_f, g_f = (mod_arg(v) for v in mods[1])
    q2, k2, v2 = _normmod_mm(x2, W['norm_mix'][1], sc_m, sh_m, W['diff_w_qkv_bf16'],
                             ((0, d), (d, 2 * d), (2 * d, 3 * d)), tm, t)
    q3, k3, v3 = (z.reshape(b, t, d) for z in (q2, k2, v2))
    o = attend(q3, k3, v3)
    y2 = _mix_mlp(x2, g_m, (o.reshape(m, d),), (W['w_out_bf16'][1],), W['norm_mlp'][1], sc_f, sh_f,
                  g_f, W['w_up_bf16'][1], W['w_down_bf16'][1], W['norm_f'], tm_mlp, MLP_FF_TILE,
                  t, True)

    n_heads_c = d // HEAD_C
    g_b, p_b = W['s5_ar'].shape
    return (y2.reshape(b, t, d), shift1, wkv1, sre1.reshape(b, g_b, p_b), sim1.reshape(b, g_b, p_b),
            k3.reshape(b, t, n_heads_c, HEAD_C), v3.reshape(b, t, n_heads_c, HEAD_C))


def kernel(x_prompt, x_sample, state_shift, state_wkv, state_ssm_re, state_ssm_im, cache_k, cache_v, page_table, c_prompt, c_sample, norm_mix, norm_mlp, norm_f, w_ada, b_ada, w_out, w_up, w_down, w_in_ab, rwkv_mu, rwkv_w0, rwkv_w_up, rwkv_a0, rwkv_a_up, rwkv_g_up, rwkv_k_k, rwkv_k_a, rwkv_r_k, rwkv_lnx_w, rwkv_lnx_b, s5_lam_re, s5_lam_im, s5_log_dt, s5_b_re, s5_b_im, s5_c_re, s5_c_im, s5_d, s5_w_glu, s5_b_glu, diff_w_qkv, diff_lq1, diff_lk1, diff_lq2, diff_lk2, diff_subln):
    bp, tp, d = x_prompt.shape
    db, ts, _ = x_sample.shape
    depth = w_ada.shape[0]
    n_heads_a = rwkv_r_k.shape[0]
    g_b, p_b = s5_lam_re.shape

    ar, ai, bbt_re, bbt_im = _s5_discretise(s5_lam_re, s5_lam_im, s5_log_dt, s5_b_re, s5_b_im)
    W = dict(norm_mix=norm_mix, norm_mlp=norm_mlp, norm_f=norm_f,
             w_out_bf16=w_out.astype(BF16), w_up_bf16=w_up.astype(BF16),
             w_down_bf16=w_down.astype(BF16), w_in_ab_bf16=w_in_ab.astype(BF16),
             diff_w_qkv_bf16=diff_w_qkv.astype(BF16),
             rwkv_mu=rwkv_mu, rwkv_w0=rwkv_w0, rwkv_w_up=rwkv_w_up, rwkv_a0=rwkv_a0,
             rwkv_a_up=rwkv_a_up, rwkv_g_up=rwkv_g_up, rwkv_k_k=rwkv_k_k, rwkv_k_a=rwkv_k_a,
             rwkv_r_k=rwkv_r_k.reshape(-1), rwkv_lnx_w=rwkv_lnx_w, rwkv_lnx_b=rwkv_lnx_b,
             s5_ar=ar, s5_ai=ai, s5_bbt_re=bbt_re, s5_bbt_im=bbt_im,
             s5_c_re=s5_c_re, s5_c_im=s5_c_im, s5_d=s5_d.reshape(-1), s5_w_glu=s5_w_glu,
             s5_b_glu=s5_b_glu)

    n_c = bp + db
    rows_c = -(-n_c // SUBLANES) * SUBLANES
    c_all = jnp.pad(jnp.concatenate([c_prompt, c_sample], axis=0), ((0, rows_c - n_c), (0, 0)))
    mod = _ada_mod(c_all, w_ada, b_ada)
    mods_p = [tuple(mod[l, :bp, i * d:(i + 1) * d] for i in range(6)) for l in range(depth)]
    mods_s = [tuple(mod[l, bp:n_c, i * d:(i + 1) * d] for i in range(6)) for l in range(depth)]

    lam_args = (diff_lq1, diff_lk1, diff_lq2, diff_lk2, diff_subln)
    attend_p = lambda q, k, v: _attn_prompt(q, k, v, *lam_args, tile=min(ATTN_KEY_TILE, tp // 2))
    y_prompt, p_shift, p_wkv, p_re, p_im, p_k, p_v = _trunk(
        x_prompt, mods_p,
        jnp.zeros((bp, rwkv_mu.shape[0]), F32),
        jnp.zeros((bp, n_heads_a, HEAD_A, HEAD_A), F32),
        jnp.zeros((bp, g_b, p_b), F32), jnp.zeros((bp, g_b, p_b), F32),
        attend_p, W, per_row_mod=False)

    attend_s = lambda q, k, v: _attn_sample(q, k, v, cache_k, cache_v, page_table, *lam_args,
                                            pages_per_step=PAGES_PER_STEP)
    y_sample, s_shift, s_wkv, s_re, s_im, s_k, s_v = _trunk(
        x_sample, mods_s, state_shift, state_wkv, state_ssm_re, state_ssm_im,
        attend_s, W, per_row_mod=True)

    return (y_prompt, y_sample, p_shift, p_wkv, p_re, p_im, p_k, p_v,
            s_shift, s_wkv, s_re, s_im, s_k, s_v)
```
